```python
import numpy as np
import jax
import jax.numpy as jnp
from jax import lax

D_MODEL = 1024
BATCH = 32
SEQ = 2048
DEPTH = 2

NSA_HEADS = 8
NSA_KV_HEADS = 2
NSA_HEAD_DIM = 64
NSA_GROUP = NSA_HEADS // NSA_KV_HEADS
NSA_WIDTH = NSA_HEADS * NSA_HEAD_DIM
NSA_KV_WIDTH = NSA_KV_HEADS * NSA_HEAD_DIM
CMP_BLOCK = 32
CMP_STRIDE = 16
CMP_HIDDEN = 256
SEL_BLOCK = 64
N_SELECT = 8
WINDOW = 512
QUERY_BLOCK = 64

GDN_HEADS = 4
GDN_HEAD_DIM = 128
GDN_WIDTH = GDN_HEADS * GDN_HEAD_DIM
GDN_CONV = 4
GDN_CHUNK = 64

MIX_WIDTH = NSA_WIDTH + GDN_WIDTH
D_FF = 2816
FFN_CONV = 3
EPS = 1e-6

SPLIT_SIZES = (NSA_WIDTH, 6 * NSA_KV_WIDTH, 3 * NSA_HEADS, 3 * GDN_WIDTH, GDN_WIDTH, GDN_HEADS, GDN_HEADS)
IN_WIDTH = sum(SPLIT_SIZES)
SPLIT_POINTS = tuple(int(v) for v in np.cumsum(SPLIT_SIZES)[:-1])

kernel_name = "hymba_nsa_gdn_convffn"


def rmsnorm(x, g):
    xf = x.astype(jnp.float32)
    y = xf * lax.rsqrt(jnp.mean(xf * xf, axis=-1, keepdims=True) + EPS)
    return (y * g.astype(jnp.float32)).astype(x.dtype)


def l2norm(x):
    xf = x.astype(jnp.float32)
    return xf * lax.rsqrt(jnp.sum(xf * xf, axis=-1, keepdims=True) + EPS)


def causal_dwconv(x, w):
    width = w.shape[0]
    s = x.shape[1]
    xp = jnp.pad(x, ((0, 0), (width - 1, 0), (0, 0)))
    y = xp[:, 0:s] * w[0]
    for k in range(1, width):
        y = y + xp[:, k:k + s] * w[k]
    return y


def masked_softmax(s, mask):
    s = jnp.where(mask, s.astype(jnp.float32), -jnp.inf)
    m = jnp.max(s, axis=-1, keepdims=True)
    m = jnp.where(jnp.isfinite(m), m, 0.0)
    p = jnp.where(mask, jnp.exp(s - m), 0.0)
    return p / jnp.maximum(jnp.sum(p, axis=-1, keepdims=True), 1e-30)


def compress(kv, pos, w1, b1, w2):
    b, s, h, dh = kv.shape
    n_cmp = (s - CMP_BLOCK) // CMP_STRIDE + 1
    idx = jnp.arange(n_cmp)[:, None] * CMP_STRIDE + jnp.arange(CMP_BLOCK)[None, :]
    blocks = kv[:, idx] + pos[None, None, :, None, :]
    flat = blocks.transpose(0, 3, 1, 2, 4).reshape(b, h, n_cmp, CMP_BLOCK * dh)
    return jax.nn.gelu(flat @ w1 + b1) @ w2


def nsa_mixer(q, kv, gates, pos_k, pos_v, ck_w1, ck_b1, ck_w2, cv_w1, cv_b1, cv_w2):
    b, s = q.shape[:2]
    h, g, dh = NSA_KV_HEADS, NSA_GROUP, NSA_HEAD_DIM
    kv = kv.reshape(b, s, 6, h, dh)
    k_cmp, v_cmp, k_slc, v_slc, k_win, v_win = (kv[:, :, i] for i in range(6))
    kc = compress(k_cmp, pos_k, ck_w1, ck_b1, ck_w2)
    vc = compress(v_cmp, pos_v, cv_w1, cv_b1, cv_w2)
    n_cmp = kc.shape[2]
    n_sb = s // SEL_BLOCK
    top = min(N_SELECT, n_sb)
    cmp_start = jnp.arange(n_cmp) * CMP_STRIDE
    sel_start = jnp.arange(n_sb) * SEL_BLOCK
    overlap = ((cmp_start[:, None] < sel_start[None, :] + SEL_BLOCK)
               & (sel_start[None, :] < cmp_start[:, None] + CMP_BLOCK)).astype(jnp.float32)
    ks_blk = k_slc.reshape(b, n_sb, SEL_BLOCK, h, dh).transpose(0, 3, 1, 2, 4)
    vs_blk = v_slc.reshape(b, n_sb, SEL_BLOCK, h, dh).transpose(0, 3, 1, 2, 4)
    kw_pad = jnp.pad(k_win, ((0, 0), (WINDOW, 0), (0, 0), (0, 0)))
    vw_pad = jnp.pad(v_win, ((0, 0), (WINDOW, 0), (0, 0), (0, 0)))
    qg = q.reshape(b, s, h, g, dh) * (NSA_HEAD_DIM ** -0.5)
    bi = jnp.arange(b)[:, None, None, None]
    hi = jnp.arange(h)[None, :, None, None]
    j_blk = jnp.arange(n_sb)

    def query_block(s0):
        t = s0 + jnp.arange(QUERY_BLOCK)
        qb = lax.dynamic_slice_in_dim(qg, s0, QUERY_BLOCK, axis=1)
        m_cmp = (cmp_start[None, :] + CMP_BLOCK - 1) <= t[:, None]
        p_cmp = masked_softmax(jnp.einsum('bqhgd,bhnd->bhgqn', qb, kc), m_cmp)
        o_cmp = jnp.einsum('bhgqn,bhnd->bqhgd', p_cmp, vc)
        imp = jnp.einsum('bhgqn,nj->bhqj', p_cmp, overlap)
        cur = t // SEL_BLOCK
        valid = sel_start[None, :] <= t[:, None]
        forced = (j_blk[None, :] == 0) | (j_blk[None, :] == cur[:, None]) | (j_blk[None, :] == cur[:, None] - 1)
        score = jnp.where(forced, jnp.inf, jnp.where(valid, imp, -jnp.inf))
        top_val, top_idx = lax.top_k(score, top)
        k_sel = ks_blk[bi, hi, top_idx].reshape(b, h, QUERY_BLOCK, top * SEL_BLOCK, dh)
        v_sel = vs_blk[bi, hi, top_idx].reshape(b, h, QUERY_BLOCK, top * SEL_BLOCK, dh)
        tok = (top_idx[..., None] * SEL_BLOCK + jnp.arange(SEL_BLOCK)).reshape(b, h, QUERY_BLOCK, top * SEL_BLOCK)
        m_sel = jnp.repeat(top_val > -jnp.inf, SEL_BLOCK, axis=-1) & (tok <= t[:, None])
        p_sel = masked_softmax(jnp.einsum('bqhgd,bhqnd->bhgqn', qb, k_sel), m_sel[:, :, None])
        o_sel = jnp.einsum('bhgqn,bhqnd->bqhgd', p_sel, v_sel)
        kw = lax.dynamic_slice_in_dim(kw_pad, s0, WINDOW + QUERY_BLOCK, axis=1)
        vw = lax.dynamic_slice_in_dim(vw_pad, s0, WINDOW + QUERY_BLOCK, axis=1)
        kpos = s0 - WINDOW + jnp.arange(WINDOW + QUERY_BLOCK)
        m_win = (kpos[None, :] >= 0) & (kpos[None, :] <= t[:, None]) & (kpos[None, :] > t[:, None] - WINDOW)
        p_win = masked_softmax(jnp.einsum('bqhgd,bkhd->bhgqk', qb, kw), m_win)
        o_win = jnp.einsum('bhgqk,bkhd->bqhgd', p_win, vw)
        return jnp.stack([o_cmp, o_sel, o_win], axis=-2)

    outs = lax.map(query_block, jnp.arange(s // QUERY_BLOCK) * QUERY_BLOCK)
    o = outs.transpose(1, 0, 2, 3, 4, 5, 6).reshape(b, s, NSA_HEADS, 3, dh)
    gate = jax.nn.sigmoid(gates.astype(jnp.float32).reshape(b, s, NSA_HEADS, 3))
    return jnp.einsum('bshc,bshcd->bshd', gate, o).reshape(b, s, NSA_WIDTH).astype(q.dtype)


def gated_delta_rule_chunked(q, k, v, g, beta):
    b, s, h, dk = q.shape
    dv = v.shape[-1]
    c = GDN_CHUNK
    n = s // c
    chunks = lambda a: a.reshape(b, n, c, h, -1).transpose(1, 0, 3, 2, 4)
    qc, kc, vc = chunks(q), chunks(k), chunks(v)
    gc = g.reshape(b, n, c, h).transpose(1, 0, 3, 2)
    bc = beta.reshape(b, n, c, h).transpose(1, 0, 3, 2)
    gcum = jnp.cumsum(gc, axis=-1)
    tril = jnp.tril(jnp.ones((c, c), dtype=bool))
    strict = jnp.tril(jnp.ones((c, c), dtype=bool), -1)
    decay = jnp.exp(jnp.where(tril, gcum[..., :, None] - gcum[..., None, :], -jnp.inf))
    kb = kc * bc[..., None]
    vb = vc * bc[..., None]
    a_mat = jnp.where(strict, jnp.einsum('nbhid,nbhjd->nbhij', kb, kc) * decay, 0.0)
    eye = jnp.eye(c, dtype=jnp.float32)
    rhs = jnp.concatenate([vb, kb * jnp.exp(gcum)[..., None]], axis=-1)
    sol = lax.linalg.triangular_solve(a_mat + eye, rhs, left_side=True, lower=True, unit_diagonal=True)
    u, w = sol[..., :dv], sol[..., dv:]
    attn = jnp.einsum('nbhid,nbhjd->nbhij', qc, kc) * decay

    def step(state, xs):
        q_i, k_i, u_i, w_i, g_i, a_i = xs
        v_new = u_i - jnp.einsum('bhck,bhkv->bhcv', w_i, state)
        o = (jnp.einsum('bhck,bhkv->bhcv', q_i * jnp.exp(g_i)[..., None], state)
             + jnp.einsum('bhcj,bhjv->bhcv', a_i, v_new))
        g_last = g_i[..., -1:]
        state = (state * jnp.exp(g_last)[..., None]
                 + jnp.einsum('bhck,bhcv->bhkv', k_i * jnp.exp(g_last - g_i)[..., None], v_new))
        return state, o

    state0 = jnp.zeros((b, h, dk, dv), jnp.float32)
    _, o = lax.scan(step, state0, (qc, kc, u, w, gcum, attn))
    return o.transpose(1, 0, 3, 2, 4).reshape(b, s, h, dv)


def gdn_mixer(qkv, z, b_in, a_in, conv_w, a_log, dt_bias, norm_g):
    b, s = qkv.shape[:2]
    h, d = GDN_HEADS, GDN_HEAD_DIM
    qkv = jax.nn.silu(causal_dwconv(qkv, conv_w))
    q, k, v = jnp.split(qkv, 3, axis=-1)
    q = l2norm(q.reshape(b, s, h, d)) * (d ** -0.5)
    k = l2norm(k.reshape(b, s, h, d))
    v = v.reshape(b, s, h, d).astype(jnp.float32)
    beta = jax.nn.sigmoid(b_in.astype(jnp.float32))
    g = -jnp.exp(a_log.astype(jnp.float32)) * jax.nn.softplus(a_in.astype(jnp.float32) + dt_bias.astype(jnp.float32))
    o = gated_delta_rule_chunked(q, k, v, g, beta)
    o = rmsnorm(o, norm_g) * jax.nn.silu(z.reshape(b, s, h, d).astype(jnp.float32))
    return o.reshape(b, s, GDN_WIDTH).astype(z.dtype)


def setup_inputs(seed: int = 0) -> dict:
    key = jax.random.key(seed)
    ks = jax.random.split(key, 24)
    L = DEPTH
    f32 = jnp.float32
    nrm = lambda k, shape, scale: jax.random.normal(k, shape, f32) * scale
    dt = jnp.exp(jax.random.uniform(ks[13], (L, GDN_HEADS), f32, np.log(1e-3), np.log(1e-1)))
    return {
        "x": nrm(ks[0], (BATCH, SEQ, D_MODEL), 1.0),
        "norm_mix": 1.0 + nrm(ks[1], (L, D_MODEL), 0.02),
        "w_in": nrm(ks[2], (L, D_MODEL, IN_WIDTH), D_MODEL ** -0.5),
        "cmp_pos_k": nrm(ks[3], (L, CMP_BLOCK, NSA_HEAD_DIM), 0.1),
        "cmp_pos_v": nrm(ks[4], (L, CMP_BLOCK, NSA_HEAD_DIM), 0.1),
        "cmp_k_w1": nrm(ks[5], (L, CMP_BLOCK * NSA_HEAD_DIM, CMP_HIDDEN), (CMP_BLOCK * NSA_HEAD_DIM) ** -0.5),
        "cmp_k_b1": nrm(ks[6], (L, CMP_HIDDEN), 0.01),
        "cmp_k_w2": nrm(ks[7], (L, CMP_HIDDEN, NSA_HEAD_DIM), CMP_HIDDEN ** -0.5),
        "cmp_v_w1": nrm(ks[8], (L, CMP_BLOCK * NSA_HEAD_DIM, CMP_HIDDEN), (CMP_BLOCK * NSA_HEAD_DIM) ** -0.5),
        "cmp_v_b1": nrm(ks[9], (L, CMP_HIDDEN), 0.01),
        "cmp_v_w2": nrm(ks[10], (L, CMP_HIDDEN, NSA_HEAD_DIM), CMP_HIDDEN ** -0.5),
        "nsa_norm": 1.0 + nrm(ks[11], (L, NSA_WIDTH), 0.02),
        "gdn_conv": nrm(ks[12], (L, GDN_CONV, 3 * GDN_WIDTH), GDN_CONV ** -0.5),
        "gdn_a_log": jnp.log(jax.random.uniform(ks[14], (L, GDN_HEADS), f32, 1.0, 16.0)),
        "gdn_dt_bias": dt + jnp.log(-jnp.expm1(-dt)),
        "gdn_norm": 1.0 + nrm(ks[15], (L, GDN_HEAD_DIM), 0.02),
        "w_out": nrm(ks[16], (L, MIX_WIDTH, D_MODEL), MIX_WIDTH ** -0.5),
        "norm_ffn": 1.0 + nrm(ks[17], (L, D_MODEL), 0.02),
        "ffn_up": nrm(ks[18], (L, D_MODEL, 2 * D_FF), D_MODEL ** -0.5),
        "ffn_conv": nrm(ks[19], (L, FFN_CONV, 2 * D_FF), FFN_CONV ** -0.5),
        "ffn_down": nrm(ks[20], (L, D_FF, D_MODEL), D_FF ** -0.5),
        "norm_final": 1.0 + nrm(ks[21], (D_MODEL,), 0.02),
    }


def reference(x, norm_mix, w_in, cmp_pos_k, cmp_pos_v, cmp_k_w1, cmp_k_b1, cmp_k_w2, cmp_v_w1, cmp_v_b1, cmp_v_w2,
              nsa_norm, gdn_conv, gdn_a_log, gdn_dt_bias, gdn_norm, w_out, norm_ffn, ffn_up, ffn_conv, ffn_down,
              norm_final):
    for l in range(DEPTH):
        h = rmsnorm(x, norm_mix[l])
        proj = h @ w_in[l]
        q_nsa, kv_nsa, gate_nsa, qkv_gdn, z_gdn, b_gdn, a_gdn = jnp.split(proj, SPLIT_POINTS, axis=-1)
        o_nsa = nsa_mixer(q_nsa, kv_nsa, gate_nsa, cmp_pos_k[l], cmp_pos_v[l],
                          cmp_k_w1[l], cmp_k_b1[l], cmp_k_w2[l], cmp_v_w1[l], cmp_v_b1[l], cmp_v_w2[l])
        o_nsa = rmsnorm(o_nsa, nsa_norm[l])
        o_gdn = gdn_mixer(qkv_gdn, z_gdn, b_gdn, a_gdn, gdn_conv[l], gdn_a_log[l], gdn_dt_bias[l], gdn_norm[l])
        x = x + jnp.concatenate([o_nsa, o_gdn], axis=-1) @ w_out[l]
        h = rmsnorm(x, norm_ffn[l])
        u = causal_dwconv(h @ ffn_up[l], ffn_conv[l])
        gate, up = jnp.split(u, 2, axis=-1)
        x = x + (jax.nn.silu(gate) * up) @ ffn_down[l]
    return rmsnorm(x, norm_final)
```

```python
import functools

import jax
import jax.numpy as jnp
from jax import lax
from jax.experimental import pallas as pl
from jax.experimental.pallas import tpu as pltpu

F32 = jnp.float32
BF16 = jnp.bfloat16
EPS = 1e-6
NEG_INF = float("-inf")

NSA_HEADS = 8
NSA_KV_HEADS = 2
NSA_GROUP = NSA_HEADS // NSA_KV_HEADS
NSA_DH = 64
NSA_WIDTH = NSA_HEADS * NSA_DH
NSA_KV_WIDTH = NSA_KV_HEADS * NSA_DH
CMP_BLOCK = 32
CMP_STRIDE = 16
SEL_BLOCK = 64
SEL_SHIFT = SEL_BLOCK.bit_length() - 1
N_SELECT = 8
WINDOW = 512
GDN_HEADS = 4
GDN_DH = 128
GDN_WIDTH = GDN_HEADS * GDN_DH
GDN_CHUNK = 64

LANES = 128
VMEM_LIMIT_BYTES = 56 * 1024 * 1024

ROW_TILE = 512
FFN_HALO = 8
FFN_COL_CHUNK = 256
SEL_KEY_CHUNK = 512
WIN_KEYS = WINDOW + 2 * SEL_BLOCK
GDN_SEQ_TILE = 512

SM_GATE0 = 0
SM_BETA0 = 3 * NSA_HEADS
SM_A0 = SM_BETA0 + GDN_HEADS


def _dot(a, b, precision=None):
    return jnp.dot(a, b, preferred_element_type=F32, precision=precision)


def _dot_nt(a, b, precision=None):
    return lax.dot_general(a, b, (((1,), (1,)), ((), ())), preferred_element_type=F32, precision=precision)


def _dot_tn(a, b, precision=None):
    return lax.dot_general(a, b, (((0,), (0,)), ((), ())), preferred_element_type=F32, precision=precision)


def _lane_col(x, idx):
    lane = lax.broadcasted_iota(jnp.int32, x.shape, 1)
    return jnp.sum(jnp.where(lane == idx, x, 0.0), axis=-1, keepdims=True)


def _const_spec(shape):
    zeros = (0,) * len(shape)
    return pl.BlockSpec(shape, lambda *_: zeros, pipeline_mode=pl.Buffered(1))


def _params(*semantics):
    return pltpu.CompilerParams(dimension_semantics=semantics, vmem_limit_bytes=VMEM_LIMIT_BYTES)


def _inproj_kernel(x_ref, g_ref, wq_ref, wkv_ref, wgdn_ref, wz_ref, wsm_ref,
                   q_ref, kv_ref, gdn_ref, z_ref, sm_ref):
    x = x_ref[...]
    ms = jnp.mean(x * x, axis=-1, keepdims=True)
    h = (x * lax.rsqrt(ms + EPS) * g_ref[...]).astype(BF16)
    q_ref[...] = _dot(h, wq_ref[...]).astype(BF16)
    kv_ref[...] = _dot(h, wkv_ref[...]).astype(BF16)
    gdn_ref[...] = _dot(h, wgdn_ref[...])
    z_ref[...] = _dot(h, wz_ref[...])
    sm_ref[...] = _dot(h, wsm_ref[...])


def _inproj(x2, g, wq, wkv, wgdn, wz, wsm):
    n, d = x2.shape
    tm = ROW_TILE
    row = lambda w: pl.BlockSpec((tm, w), lambda i: (i, 0))
    widths = (wq.shape[1], wkv.shape[1], wgdn.shape[1], wz.shape[1], wsm.shape[1])
    dtypes = (BF16, BF16, F32, F32, F32)
    return pl.pallas_call(
        _inproj_kernel,
        grid=(n // tm,),
        in_specs=[row(d), _const_spec((1, d))] + [_const_spec(w.shape) for w in (wq, wkv, wgdn, wz, wsm)],
        out_specs=[row(w) for w in widths],
        out_shape=[jax.ShapeDtypeStruct((n, w), dt) for w, dt in zip(widths, dtypes)],
        compiler_params=_params("parallel"),
        name="inproj",
    )(x2, g, wq, wkv, wgdn, wz, wsm)


def _compress_kernel(g_ref, posk_ref, posv_ref, w1k_ref, b1k_ref, w2k_ref, w1v_ref, b1v_ref, w2v_ref,
                     kc_ref, vc_ref):
    ng = g_ref.shape[2]
    half = w1k_ref.shape[0] // 2
    row = lax.broadcasted_iota(jnp.int32, (ng, 1), 0)
    for sel, (pos_ref, w1_ref, b1_ref, w2_ref, out_ref) in enumerate(
            ((posk_ref, w1k_ref, b1k_ref, w2k_ref, kc_ref), (posv_ref, w1v_ref, b1v_ref, w2v_ref, vc_ref))):
        w1 = w1_ref[...]
        pos8 = jnp.broadcast_to(pos_ref[...], (8, pos_ref.shape[1])).astype(BF16)
        bias = _dot(pos8, w1)[0:1, :] + b1_ref[...]
        acc = jnp.zeros((ng, LANES), F32)
        for h in range(NSA_KV_HEADS):
            grp = g_ref[0, sel * NSA_KV_HEADS + h]
            top = _dot(grp, w1[:half])
            bot = _dot(grp, w1[half:])
            hid = top + pltpu.roll(bot, ng - 1, axis=0) + bias
            act = jax.nn.gelu(hid, approximate=True).astype(BF16)
            acc = acc + _dot(act, w2_ref[h])
        out_ref[0] = jnp.where(row < ng - 1, acc, 0.0).astype(BF16)


def _compress(groups, posk, posv, w1k, b1k, w2k, w1v, b1v, w2v):
    b, _, ng, gw = groups.shape
    out = jax.ShapeDtypeStruct((b, ng, LANES), BF16)
    return pl.pallas_call(
        _compress_kernel,
        grid=(b,),
        in_specs=[pl.BlockSpec((1, 2 * NSA_KV_HEADS, ng, gw), lambda i: (i, 0, 0, 0))]
        + [_const_spec(a.shape) for a in (posk, posv, w1k, b1k, w2k, w1v, b1v, w2v)],
        out_specs=[pl.BlockSpec((1, ng, LANES), lambda i: (i, 0, 0))] * 2,
        out_shape=[out, out],
        compiler_params=_params("parallel"),
        name="nsa_compress",
    )(groups, posk, posv, w1k, b1k, w2k, w1v, b1v, w2v)


def _softmax_parts(s, mask):
    sm = jnp.where(mask, s, NEG_INF)
    m = jnp.max(sm, axis=-1, keepdims=True)
    m = jnp.where(jnp.isfinite(m), m, 0.0)
    e = jnp.where(mask, jnp.exp(sm - m), 0.0)
    return e, jnp.sum(e, axis=-1, keepdims=True)


def _nsa_kernel(q_ref, sm_ref, kc_ref, vc_ref, ks_ref, vs_ref, kw_ref, vw_ref, o_ref):
    qb = q_ref.shape[1]
    seq = ks_ref.shape[1]
    ncmp = kc_ref.shape[1]
    nsb = seq // SEL_BLOCK
    rows = NSA_GROUP * qb
    blk = pl.program_id(1)
    s0 = blk * qb

    gates = jax.nn.sigmoid(sm_ref[0])
    q_all = q_ref[0]
    t_rows = s0 + (lax.broadcasted_iota(jnp.int32, (rows, 1), 0) & (qb - 1))

    jn = lax.broadcasted_iota(jnp.int32, (nsb, ncmp), 0)
    nn = lax.broadcasted_iota(jnp.int32, (nsb, ncmp), 1)
    overlap_t = jnp.where((nn * CMP_STRIDE < jn * SEL_BLOCK + SEL_BLOCK)
                          & (jn * SEL_BLOCK < nn * CMP_STRIDE + CMP_BLOCK) & (nn < ncmp - 1), 1.0, 0.0)
    jj = lax.broadcasted_iota(jnp.int32, (nsb, qb), 0)
    eye_q = jnp.where(lax.broadcasted_iota(jnp.int32, (qb, qb), 0)
                      == lax.broadcasted_iota(jnp.int32, (qb, qb), 1), 1.0, 0.0).astype(BF16)
    lane = lax.broadcasted_iota(jnp.int32, (qb, LANES), 1)

    n_cmp_idx = lax.broadcasted_iota(jnp.int32, (1, ncmp), 1)
    cmp_mask = (n_cmp_idx * CMP_STRIDE + (CMP_BLOCK - 1) <= t_rows) & (n_cmp_idx < ncmp - 1)

    w_lo = jnp.minimum(jnp.maximum(s0 - WINDOW, 0), seq - WIN_KEYS)
    w_lo = pl.multiple_of(w_lo, SEL_BLOCK)
    kpos_w = w_lo + lax.broadcasted_iota(jnp.int32, (1, WIN_KEYS), 1)
    win_mask = (kpos_w <= t_rows) & (kpos_w > t_rows - WINDOW)

    for h in range(NSA_KV_HEADS):
        qs = jnp.concatenate([q_all[:, (h * NSA_GROUP + g) * LANES:(h * NSA_GROUP + g + 1) * LANES]
                              for g in range(NSA_GROUP)], axis=0)

        e, l = _softmax_parts(_dot_nt(qs, kc_ref[0]), cmp_mask)
        p_cmp = e / jnp.maximum(l, 1e-30)
        o_cmp = _dot(p_cmp.astype(BF16), vc_ref[0])

        p_sum = p_cmp[0:qb]
        for g in range(1, NSA_GROUP):
            p_sum = p_sum + p_cmp[g * qb:(g + 1) * qb]
        imp_t = _dot_nt(overlap_t, p_sum, precision=lax.Precision.HIGHEST)
        forced = (jj == 0) | (jj == blk) | (jj == blk - 1)
        score = jnp.where(forced, jnp.inf, jnp.where(jj <= blk, imp_t, NEG_INF))
        rank = jnp.zeros((nsb, qb), F32)
        for j2 in range(nsb):
            other = score[j2:j2 + 1, :]
            ahead = (other > score) | ((other == score) & (jj > j2))
            rank = rank + jnp.where(ahead, 1.0, 0.0)
        sel_t = jnp.where((rank < N_SELECT) & (score > NEG_INF), 1.0, 0.0)
        sel = _dot_nt(eye_q, sel_t.astype(BF16)).astype(BF16)
        sel_rows = jnp.concatenate([sel] * NSA_GROUP, axis=0)

        def sel_step(c, carry):
            m_prev, l_prev, acc = carry
            k0 = pl.multiple_of(c * SEL_KEY_CHUNK, SEL_KEY_CHUNK)
            s = _dot_nt(qs, ks_ref[0, pl.ds(k0, SEL_KEY_CHUNK), :])
            kpos = k0 + lax.broadcasted_iota(jnp.int32, (1, SEL_KEY_CHUNK), 1)
            expand = jnp.where((k0 + lax.broadcasted_iota(jnp.int32, (nsb, SEL_KEY_CHUNK), 1)) >> SEL_SHIFT
                               == lax.broadcasted_iota(jnp.int32, (nsb, SEL_KEY_CHUNK), 0), 1.0, 0.0).astype(BF16)
            mask = (_dot(sel_rows, expand) > 0.5) & (kpos <= t_rows)
            sm = jnp.where(mask, s, NEG_INF)
            m_new = jnp.maximum(m_prev, jnp.max(sm, axis=-1, keepdims=True))
            m_safe = jnp.where(jnp.isfinite(m_new), m_new, 0.0)
            p = jnp.where(mask, jnp.exp(sm - m_safe), 0.0)
            alpha = jnp.exp(m_prev - m_safe)
            l_new = alpha * l_prev + jnp.sum(p, axis=-1, keepdims=True)
            acc_new = alpha * acc + _dot(p.astype(BF16), vs_ref[0, pl.ds(k0, SEL_KEY_CHUNK), :])
            return m_new, l_new, acc_new

        n_chunks = (s0 + qb + SEL_KEY_CHUNK - 1) // SEL_KEY_CHUNK
        _, l_sel, acc_sel = lax.fori_loop(
            0, n_chunks, sel_step,
            (jnp.full((rows, 1), NEG_INF, F32), jnp.zeros((rows, 1), F32), jnp.zeros((rows, LANES), F32)))
        o_sel = acc_sel / jnp.maximum(l_sel, 1e-30)

        e, l = _softmax_parts(_dot_nt(qs, kw_ref[0, pl.ds(w_lo, WIN_KEYS), :]), win_mask)
        o_win = _dot(e.astype(BF16), vw_ref[0, pl.ds(w_lo, WIN_KEYS), :]) / jnp.maximum(l, 1e-30)

        mixed = []
        for g in range(NSA_GROUP):
            hq = h * NSA_GROUP + g
            r = slice(g * qb, (g + 1) * qb)
            mixed.append(_lane_col(gates, SM_GATE0 + 3 * hq) * o_cmp[r]
                         + _lane_col(gates, SM_GATE0 + 3 * hq + 1) * o_sel[r]
                         + _lane_col(gates, SM_GATE0 + 3 * hq + 2) * o_win[r])
        for pair in range(NSA_GROUP // 2):
            a, b = mixed[2 * pair], mixed[2 * pair + 1]
            if h == 0:
                b = pltpu.roll(b, NSA_DH, axis=1)
            else:
                a = pltpu.roll(a, NSA_DH, axis=1)
            c0 = (h * NSA_GROUP + 2 * pair) * NSA_DH
            o_ref[0, :, c0:c0 + LANES] = jnp.where(lane < NSA_DH, a, b)


def _nsa_attention(q, sm, kc, vc, kv):
    b, s, qw = q.shape
    qb = SEL_BLOCK
    ncmp = kc.shape[1]
    kv_spec = lambda col: pl.BlockSpec((1, s, LANES), lambda i, j: (i, 0, col))
    return pl.pallas_call(
        _nsa_kernel,
        grid=(b, s // qb),
        in_specs=[pl.BlockSpec((1, qb, qw), lambda i, j: (i, j, 0)),
                  pl.BlockSpec((1, qb, LANES), lambda i, j: (i, j, 0)),
                  pl.BlockSpec((1, ncmp, LANES), lambda i, j: (i, 0, 0)),
                  pl.BlockSpec((1, ncmp, LANES), lambda i, j: (i, 0, 0)),
                  kv_spec(2), kv_spec(3), kv_spec(4), kv_spec(5)],
        out_specs=pl.BlockSpec((1, qb, NSA_WIDTH), lambda i, j: (i, j, 0)),
        out_shape=jax.ShapeDtypeStruct((b, s, NSA_WIDTH), F32),
        compiler_params=_params("parallel", "arbitrary"),
        name="nsa_attention",
    )(q, sm, kc, vc, kv, kv, kv, kv)


def _gdn_conv_kernel(x_ref, w_ref, o_ref):
    x = x_ref[0]
    w = w_ref[...]
    width = w.shape[0]
    row = lax.broadcasted_iota(jnp.int32, (x.shape[0], 1), 0)
    y = jnp.where(row >= width - 1, pltpu.roll(x, width - 1, axis=0), 0.0) * w[0:1]
    for k in range(1, width - 1):
        shift = width - 1 - k
        y = y + jnp.where(row >= shift, pltpu.roll(x, shift, axis=0), 0.0) * w[k:k + 1]
    y = y + x * w[width - 1:width]
    y = jax.nn.silu(y)
    part = pl.program_id(1) // GDN_HEADS

    @pl.when(part == 2)
    def _():
        o_ref[0] = y

    @pl.when(part < 2)
    def _():
        scale = jnp.where(part == 0, GDN_DH ** -0.5, 1.0)
        o_ref[0] = y * lax.rsqrt(jnp.sum(y * y, axis=-1, keepdims=True) + EPS) * scale


def _gdn_conv(qkv, w):
    b, s, width = qkv.shape
    return pl.pallas_call(
        _gdn_conv_kernel,
        grid=(b, width // GDN_DH),
        in_specs=[pl.BlockSpec((1, s, GDN_DH), lambda i, j: (i, 0, j)),
                  pl.BlockSpec((w.shape[0], GDN_DH), lambda i, j: (0, j))],
        out_specs=pl.BlockSpec((1, s, GDN_DH), lambda i, j: (i, 0, j)),
        out_shape=jax.ShapeDtypeStruct((b, s, width), F32),
        compiler_params=_params("parallel", "parallel"),
        name="gdn_conv",
    )(qkv, w)


def _unit_lower_inverse(a, r, c):
    eye = jnp.where(r == c, 1.0, 0.0)
    blk16 = (r >> 4) == (c >> 4)
    blk32 = (r >> 5) == (c >> 5)
    mm = lambda x, y: _dot(x.astype(BF16), y.astype(BF16))
    n1 = jnp.where(blk16, -a, 0.0)
    n2 = mm(n1, n1)
    n4 = mm(n2, n2)
    n8 = mm(n4, n4)
    t = eye + n1
    t = t + mm(t, n2)
    t = t + mm(t, n4)
    t = t + mm(t, n8)
    l1 = jnp.where(blk32 & jnp.logical_not(blk16), a, 0.0)
    t = t - mm(t, mm(l1, t))
    l2 = jnp.where(blk32, 0.0, a)
    return t - mm(t, mm(l2, t))


def _gdn_scan_kernel(q_ref, k_ref, v_ref, z_ref, sm_ref, alog_ref, dtb_ref, ng_ref, o_ref, state_ref):
    c = GDN_CHUNK
    n_chunks = q_ref.shape[1] // c

    @pl.when(pl.program_id(1) == 0)
    def _():
        state_ref[...] = jnp.zeros_like(state_ref)

    r = lax.broadcasted_iota(jnp.int32, (c, c), 0)
    cc = lax.broadcasted_iota(jnp.int32, (c, c), 1)
    tril = r >= cc
    strict = r > cc
    cum_mat = jnp.where(tril, 1.0, 0.0)
    pick = jnp.where(lax.broadcasted_iota(jnp.int32, (8, LANES), 1)
                     == lax.broadcasted_iota(jnp.int32, (8, LANES), 0) + SM_A0, 1.0, 0.0)
    hi = lax.Precision.HIGHEST

    def chunk_step(ci, _):
        t0 = pl.multiple_of(ci * c, c)
        sm = sm_ref[0, pl.ds(t0, c), :]
        beta = jax.nn.sigmoid(sm)
        g = -jnp.exp(alog_ref[...]) * jax.nn.softplus(sm + dtb_ref[...])
        gcum = _dot(cum_mat, g, precision=hi)
        gcum_rows = _dot_nt(pick, gcum, precision=hi)
        for h in range(GDN_HEADS):
            cols = slice(h * GDN_DH, (h + 1) * GDN_DH)
            q = q_ref[0, pl.ds(t0, c), cols]
            k = k_ref[0, pl.ds(t0, c), cols]
            v = v_ref[0, pl.ds(t0, c), cols]
            gc = _lane_col(gcum, SM_A0 + h)
            gr = gcum_rows[h:h + 1, :]
            bcol = _lane_col(beta, SM_BETA0 + h)
            decay = jnp.exp(jnp.where(tril, gc - gr, NEG_INF))
            kb = k * bcol
            vb = v * bcol
            kbf = k.astype(BF16)
            a_mat = jnp.where(strict, _dot_nt(kb.astype(BF16), kbf) * decay, 0.0)
            t_inv = _unit_lower_inverse(a_mat, r, cc).astype(BF16)
            u = _dot(t_inv, vb.astype(BF16))
            w = _dot(t_inv, (kb * jnp.exp(gc)).astype(BF16))
            attn = _dot_nt(q.astype(BF16), kbf) * decay
            state = state_ref[h]
            sb = state.astype(BF16)
            v_new = u - _dot(w.astype(BF16), sb)
            vnb = v_new.astype(BF16)
            o = _dot((q * jnp.exp(gc)).astype(BF16), sb) + _dot(attn.astype(BF16), vnb)
            g_last = gc[c - 1:c, :]
            state_ref[h] = state * jnp.exp(g_last) + _dot_tn((k * jnp.exp(g_last - gc)).astype(BF16), vnb)
            y = o * lax.rsqrt(jnp.mean(o * o, axis=-1, keepdims=True) + EPS) * ng_ref[...]
            o_ref[0, pl.ds(t0, c), cols] = y * jax.nn.silu(z_ref[0, pl.ds(t0, c), cols])
        return 0

    lax.fori_loop(0, n_chunks, chunk_step, 0)


def _gdn_scan(qkv, z, sm, alog_vec, dtb_vec, norm_g):
    b, s, _ = qkv.shape
    ts = GDN_SEQ_TILE
    part = lambda p: pl.BlockSpec((1, ts, GDN_WIDTH), lambda i, j: (i, j, p))
    return pl.pallas_call(
        _gdn_scan_kernel,
        grid=(b, s // ts),
        in_specs=[part(0), part(1), part(2),
                  pl.BlockSpec((1, ts, GDN_WIDTH), lambda i, j: (i, j, 0)),
                  pl.BlockSpec((1, ts, LANES), lambda i, j: (i, j, 0)),
                  _const_spec((1, LANES)), _const_spec((1, LANES)), _const_spec((1, GDN_DH))],
        out_specs=pl.BlockSpec((1, ts, GDN_WIDTH), lambda i, j: (i, j, 0)),
        out_shape=jax.ShapeDtypeStruct((b, s, GDN_WIDTH), F32),
        scratch_shapes=[pltpu.VMEM((GDN_HEADS, GDN_DH, GDN_DH), F32)],
        compiler_params=_params("parallel", "arbitrary"),
        name="gdn_scan",
    )(qkv, qkv, qkv, z, sm, alog_vec, dtb_vec, norm_g)


def _outproj_kernel(on_ref, og_ref, x_ref, g_ref, wa_ref, wb_ref, o_ref):
    on = on_ref[...]
    ms = jnp.mean(on * on, axis=-1, keepdims=True)
    hn = (on * lax.rsqrt(ms + EPS) * g_ref[...]).astype(BF16)
    o_ref[...] = x_ref[...] + _dot(hn, wa_ref[...]) + _dot(og_ref[...].astype(BF16), wb_ref[...])


def _outproj(o_nsa, o_gdn, x2, g, wa, wb):
    n, d = x2.shape
    tm = ROW_TILE
    row = lambda w: pl.BlockSpec((tm, w), lambda i: (i, 0))
    return pl.pallas_call(
        _outproj_kernel,
        grid=(n // tm,),
        in_specs=[row(o_nsa.shape[1]), row(o_gdn.shape[1]), row(d),
                  _const_spec(g.shape), _const_spec(wa.shape), _const_spec(wb.shape)],
        out_specs=row(d),
        out_shape=jax.ShapeDtypeStruct((n, d), F32),
        compiler_params=_params("parallel"),
        name="outproj",
    )(o_nsa, o_gdn, x2, g, wa, wb)


def _ffn_kernel(x_ref, halo_ref, g_ref, wup_ref, cw_ref, wdn_ref, o_ref):
    tm = x_ref.shape[1]
    d_ff = wdn_ref.shape[0]
    width = cw_ref.shape[0]
    x = x_ref[0]
    xin = jnp.concatenate([halo_ref[0], x], axis=0)
    ms = jnp.mean(xin * xin, axis=-1, keepdims=True)
    h = (xin * lax.rsqrt(ms + EPS) * g_ref[...]).astype(BF16)
    row = lax.broadcasted_iota(jnp.int32, (FFN_HALO + tm, 1), 0)
    keep = row >= jnp.where(pl.program_id(1) > 0, 0, FFN_HALO)

    def conv(u, c0):
        u = jnp.where(keep, u, 0.0)
        y = u[FFN_HALO:] * cw_ref[width - 1:width, c0:c0 + FFN_COL_CHUNK]
        for k in range(width - 1):
            shifted = pltpu.roll(u, width - 1 - k, axis=0)[FFN_HALO:]
            y = y + shifted * cw_ref[k:k + 1, c0:c0 + FFN_COL_CHUNK]
        return y

    acc = jnp.zeros((tm, x.shape[1]), F32)
    for j in range(d_ff // FFN_COL_CHUNK):
        c0 = j * FFN_COL_CHUNK
        gate = conv(_dot(h, wup_ref[:, c0:c0 + FFN_COL_CHUNK]), c0)
        up = conv(_dot(h, wup_ref[:, d_ff + c0:d_ff + c0 + FFN_COL_CHUNK]), d_ff + c0)
        act = (jax.nn.silu(gate) * up).astype(BF16)
        acc = acc + _dot(act, wdn_ref[c0:c0 + FFN_COL_CHUNK, :])
    o_ref[0] = x + acc


def _ffn(x3, g, wup, cw, wdn):
    b, s, d = x3.shape
    tm = ROW_TILE
    per = tm // FFN_HALO
    return pl.pallas_call(
        _ffn_kernel,
        grid=(b, s // tm),
        in_specs=[pl.BlockSpec((1, tm, d), lambda i, j: (i, j, 0)),
                  pl.BlockSpec((1, FFN_HALO, d), lambda i, j: (i, jnp.maximum(j * per - 1, 0), 0)),
                  _const_spec(g.shape), _const_spec(wup.shape), _const_spec(cw.shape), _const_spec(wdn.shape)],
        out_specs=pl.BlockSpec((1, tm, d), lambda i, j: (i, j, 0)),
        out_shape=jax.ShapeDtypeStruct((b, s, d), F32),
        compiler_params=_params("parallel", "parallel"),
        name="convffn",
    )(x3, x3, g, wup, cw, wdn)


def _rmsnorm_kernel(x_ref, g_ref, o_ref):
    x = x_ref[...]
    o_ref[...] = x * lax.rsqrt(jnp.mean(x * x, axis=-1, keepdims=True) + EPS) * g_ref[...]


def _rmsnorm(x2, g):
    n, d = x2.shape
    tm = ROW_TILE
    return pl.pallas_call(
        _rmsnorm_kernel,
        grid=(n // tm,),
        in_specs=[pl.BlockSpec((tm, d), lambda i: (i, 0)), _const_spec(g.shape)],
        out_specs=pl.BlockSpec((tm, d), lambda i: (i, 0)),
        out_shape=jax.ShapeDtypeStruct((n, d), F32),
        compiler_params=_params("parallel"),
        name="final_rmsnorm",
    )(x2, g)


def _split_in_weights(w_in):
    d = w_in.shape[0]
    o = 0
    wq = w_in[:, o:o + NSA_WIDTH]; o += NSA_WIDTH
    wkv = w_in[:, o:o + 6 * NSA_KV_WIDTH]; o += 6 * NSA_KV_WIDTH
    wgate = w_in[:, o:o + 3 * NSA_HEADS]; o += 3 * NSA_HEADS
    wgdn = w_in[:, o:o + 3 * GDN_WIDTH]; o += 3 * GDN_WIDTH
    wz = w_in[:, o:o + GDN_WIDTH]; o += GDN_WIDTH
    wb = w_in[:, o:o + GDN_HEADS]; o += GDN_HEADS
    wa = w_in[:, o:o + GDN_HEADS]
    wq = wq.reshape(d, NSA_HEADS, NSA_DH) * (NSA_DH ** -0.5)
    zero = jnp.zeros_like(wq)
    kv_head = (jnp.arange(NSA_HEADS) // NSA_GROUP)[None, :, None]
    wq_pad = jnp.concatenate([jnp.where(kv_head == 0, wq, zero), jnp.where(kv_head == 1, wq, zero)], axis=-1)
    wq_pad = wq_pad.reshape(d, NSA_HEADS * LANES)
    wsm = jnp.concatenate([wgate, wb, wa, jnp.zeros((d, LANES - SM_A0 - GDN_HEADS), w_in.dtype)], axis=-1)
    return tuple(w.astype(BF16) for w in (wq_pad, wkv, wgdn, wz, wsm))


def _lane_vec(v, offset):
    return jnp.zeros((1, LANES), F32).at[0, offset:offset + v.shape[0]].set(v.astype(F32))


def _head_padded_w2(w2):
    z = jnp.zeros_like(w2)
    return jnp.stack([jnp.concatenate([w2, z], axis=-1), jnp.concatenate([z, w2], axis=-1)]).astype(BF16)


def kernel(x, norm_mix, w_in, cmp_pos_k, cmp_pos_v, cmp_k_w1, cmp_k_b1, cmp_k_w2, cmp_v_w1, cmp_v_b1, cmp_v_w2,
           nsa_norm, gdn_conv, gdn_a_log, gdn_dt_bias, gdn_norm, w_out, norm_ffn, ffn_up, ffn_conv, ffn_down,
           norm_final):
    b, s, d = x.shape
    depth = w_in.shape[0]
    group = CMP_BLOCK // 2
    assert CMP_STRIDE == group and s % GDN_SEQ_TILE == 0 and s >= WIN_KEYS and (b * s) % ROW_TILE == 0
    x2 = x.reshape(b * s, d)
    for l in range(depth):
        wq, wkv, wgdn, wz, wsm = _split_in_weights(w_in[l])
        q, kv, gdn, z, sm = _inproj(x2, norm_mix[l][None, :], wq, wkv, wgdn, wz, wsm)
        q, kv, gdn, z, sm = (a.reshape(b, s, a.shape[-1]) for a in (q, kv, gdn, z, sm))

        groups = kv[:, :, :2 * NSA_KV_WIDTH].reshape(b, s, 2 * NSA_KV_HEADS, NSA_DH)
        groups = groups.transpose(0, 2, 1, 3).reshape(b, 2 * NSA_KV_HEADS, s // group, group * NSA_DH)
        kc, vc = _compress(
            groups, cmp_pos_k[l].reshape(1, -1), cmp_pos_v[l].reshape(1, -1),
            cmp_k_w1[l].astype(BF16), cmp_k_b1[l][None, :], _head_padded_w2(cmp_k_w2[l]),
            cmp_v_w1[l].astype(BF16), cmp_v_b1[l][None, :], _head_padded_w2(cmp_v_w2[l]))
        o_nsa = _nsa_attention(q, sm, kc, vc, kv)

        qkv_act = _gdn_conv(gdn, gdn_conv[l])
        o_gdn = _gdn_scan(qkv_act, z, sm, _lane_vec(gdn_a_log[l], SM_A0), _lane_vec(gdn_dt_bias[l], SM_A0),
                          gdn_norm[l][None, :])

        w_o = w_out[l].astype(BF16)
        x2 = _outproj(o_nsa.reshape(b * s, -1), o_gdn.reshape(b * s, -1), x2, nsa_norm[l][None, :],
                      w_o[:NSA_WIDTH], w_o[NSA_WIDTH:])
        x2 = _ffn(x2.reshape(b, s, d), norm_ffn[l][None, :], ffn_up[l].astype(BF16), ffn_conv[l],
                  ffn_down[l].astype(BF16)).reshape(b * s, d)
    return _rmsnorm(x2, norm_final[None, :]).reshape(b, s, d)
```

```python
import jax
import jax.numpy as jnp
from jax import lax
from jax.experimental import pallas as pl
from jax.experimental.pallas import tpu as pltpu

F32 = jnp.float32
BF16 = jnp.bfloat16
EPS = 1e-6
NEG_INF = float("-inf")
MASK_BIAS = -1e30

NSA_HEADS = 8
NSA_KV_HEADS = 2
NSA_GROUP = NSA_HEADS // NSA_KV_HEADS
NSA_DH = 64
NSA_WIDTH = NSA_HEADS * NSA_DH
NSA_KV_WIDTH = NSA_KV_HEADS * NSA_DH
CMP_BLOCK = 32
CMP_STRIDE = 16
SEL_BLOCK = 64
N_SELECT = 8
WINDOW = 512
GDN_HEADS = 4
GDN_DH = 128
GDN_WIDTH = GDN_HEADS * GDN_DH
GDN_CHUNK = 64
GDN_CHUNK_SHIFT = GDN_CHUNK.bit_length() - 1

LANES = 128
VMEM_LIMIT_BYTES = 56 * 1024 * 1024

ROW_TILE = 512
FFN_HALO = 8
FFN_COL_CHUNK = 256
SEL_KEY_CHUNK = 512
WIN_KEYS = WINDOW + 2 * SEL_BLOCK
GDN_SEQ_TILE = 512
GDN_PREP_TILE = 256
GDN_HALO = 8

SM_GATE0 = 0
SM_BETA0 = 3 * NSA_HEADS
SM_A0 = SM_BETA0 + GDN_HEADS


def _dot(a, b, precision=None):
    return jnp.dot(a, b, preferred_element_type=F32, precision=precision)


def _dot_nt(a, b, precision=None):
    return lax.dot_general(a, b, (((1,), (1,)), ((), ())), preferred_element_type=F32, precision=precision)


def _dot_tn(a, b, precision=None):
    return lax.dot_general(a, b, (((0,), (0,)), ((), ())), preferred_element_type=F32, precision=precision)


def _lane_col(x, idx):
    lane = lax.broadcasted_iota(jnp.int32, x.shape, 1)
    return jnp.sum(jnp.where(lane == idx, x, 0.0), axis=-1, keepdims=True)


def _const_spec(shape):
    zeros = (0,) * len(shape)
    return pl.BlockSpec(shape, lambda *_: zeros, pipeline_mode=pl.Buffered(1))


def _params(*semantics):
    return pltpu.CompilerParams(dimension_semantics=semantics, vmem_limit_bytes=VMEM_LIMIT_BYTES)


def _inproj_kernel(x_ref, g_ref, wq_ref, wkv_ref, wgdn_ref, wz_ref, wsm_ref,
                   q_ref, kv_ref, gdn_ref, z_ref, sm_ref):
    x = x_ref[...]
    ms = jnp.mean(x * x, axis=-1, keepdims=True)
    h = (x * lax.rsqrt(ms + EPS) * g_ref[...]).astype(BF16)
    q_ref[...] = _dot(h, wq_ref[...]).astype(BF16)
    kv_ref[...] = _dot(h, wkv_ref[...]).astype(BF16)
    gdn_ref[...] = _dot(h, wgdn_ref[...])
    z_ref[...] = _dot(h, wz_ref[...])
    sm_ref[...] = _dot(h, wsm_ref[...])


def _inproj(x2, g, wq, wkv, wgdn, wz, wsm):
    n, d = x2.shape
    tm = ROW_TILE
    row = lambda w: pl.BlockSpec((tm, w), lambda i: (i, 0))
    widths = (wq.shape[1], wkv.shape[1], wgdn.shape[1], wz.shape[1], wsm.shape[1])
    dtypes = (BF16, BF16, F32, F32, F32)
    return pl.pallas_call(
        _inproj_kernel,
        grid=(n // tm,),
        in_specs=[row(d), _const_spec((1, d))] + [_const_spec(w.shape) for w in (wq, wkv, wgdn, wz, wsm)],
        out_specs=[row(w) for w in widths],
        out_shape=[jax.ShapeDtypeStruct((n, w), dt) for w, dt in zip(widths, dtypes)],
        compiler_params=_params("parallel"),
        name="inproj",
    )(x2, g, wq, wkv, wgdn, wz, wsm)


def _compress_kernel(g_ref, posk_ref, posv_ref, w1k_ref, b1k_ref, w2k_ref, w1v_ref, b1v_ref, w2v_ref,
                     kc_ref, vc_ref):
    ng = g_ref.shape[2]
    half = w1k_ref.shape[0] // 2
    row = lax.broadcasted_iota(jnp.int32, (ng, 1), 0)
    for sel, (pos_ref, w1_ref, b1_ref, w2_ref, out_ref) in enumerate(
            ((posk_ref, w1k_ref, b1k_ref, w2k_ref, kc_ref), (posv_ref, w1v_ref, b1v_ref, w2v_ref, vc_ref))):
        w1 = w1_ref[...]
        pos8 = jnp.broadcast_to(pos_ref[...], (8, pos_ref.shape[1])).astype(BF16)
        bias = _dot(pos8, w1)[0:1, :] + b1_ref[...]
        acc = jnp.zeros((ng, LANES), F32)
        for h in range(NSA_KV_HEADS):
            grp = g_ref[0, sel * NSA_KV_HEADS + h]
            top = _dot(grp, w1[:half])
            bot = _dot(grp, w1[half:])
            hid = top + pltpu.roll(bot, ng - 1, axis=0) + bias
            act = jax.nn.gelu(hid, approximate=True).astype(BF16)
            acc = acc + _dot(act, w2_ref[h])
        out_ref[0] = jnp.where(row < ng - 1, acc, 0.0).astype(BF16)


def _compress(groups, posk, posv, w1k, b1k, w2k, w1v, b1v, w2v):
    b, _, ng, gw = groups.shape
    out = jax.ShapeDtypeStruct((b, ng, LANES), BF16)
    return pl.pallas_call(
        _compress_kernel,
        grid=(b,),
        in_specs=[pl.BlockSpec((1, 2 * NSA_KV_HEADS, ng, gw), lambda i: (i, 0, 0, 0))]
        + [_const_spec(a.shape) for a in (posk, posv, w1k, b1k, w2k, w1v, b1v, w2v)],
        out_specs=[pl.BlockSpec((1, ng, LANES), lambda i: (i, 0, 0))] * 2,
        out_shape=[out, out],
        compiler_params=_params("parallel"),
        name="nsa_compress",
    )(groups, posk, posv, w1k, b1k, w2k, w1v, b1v, w2v)


def _softmax_parts(s, mask):
    sm = jnp.where(mask, s, NEG_INF)
    m = jnp.max(sm, axis=-1, keepdims=True)
    m = jnp.where(jnp.isfinite(m), m, 0.0)
    e = jnp.where(mask, jnp.exp(sm - m), 0.0)
    return e, jnp.sum(e, axis=-1, keepdims=True)


def _nsa_kernel(q_ref, sm_ref, kc_ref, vc_ref, ks_ref, vs_ref, kw_ref, vw_ref, oh_ref, o_ref):
    qb = q_ref.shape[1]
    seq = ks_ref.shape[1]
    ncmp = kc_ref.shape[1]
    nsb = seq // SEL_BLOCK
    rows = NSA_GROUP * qb
    both = NSA_KV_HEADS * qb
    blk = pl.program_id(1)
    s0 = blk * qb

    gates = jax.nn.sigmoid(sm_ref[0])
    q_all = q_ref[0]
    t_rows = s0 + (lax.broadcasted_iota(jnp.int32, (rows, 1), 0) & (qb - 1))
    lane = lax.broadcasted_iota(jnp.int32, (qb, LANES), 1)

    jn = lax.broadcasted_iota(jnp.int32, (nsb, ncmp), 0)
    nn = lax.broadcasted_iota(jnp.int32, (nsb, ncmp), 1)
    overlap_t = jnp.where((nn * CMP_STRIDE < jn * SEL_BLOCK + SEL_BLOCK)
                          & (jn * SEL_BLOCK < nn * CMP_STRIDE + CMP_BLOCK) & (nn < ncmp - 1), 1.0, 0.0)
    n_cmp_idx = lax.broadcasted_iota(jnp.int32, (1, ncmp), 1)
    cmp_mask = (n_cmp_idx * CMP_STRIDE + (CMP_BLOCK - 1) <= t_rows) & (n_cmp_idx < ncmp - 1)
    qs, o_cmp, imp_t = [], [], []
    for h in range(NSA_KV_HEADS):
        qs.append(jnp.concatenate([q_all[:, (h * NSA_GROUP + g) * LANES:(h * NSA_GROUP + g + 1) * LANES]
                                   for g in range(NSA_GROUP)], axis=0))
        e, l = _softmax_parts(_dot_nt(qs[h], kc_ref[0]), cmp_mask)
        p_cmp = e / jnp.maximum(l, 1e-30)
        o_cmp.append(_dot(p_cmp.astype(BF16), vc_ref[0]))
        p_sum = p_cmp[0:qb]
        for g in range(1, NSA_GROUP):
            p_sum = p_sum + p_cmp[g * qb:(g + 1) * qb]
        imp_t.append(_dot_nt(overlap_t, p_sum, precision=lax.Precision.HIGHEST))

    jj = lax.broadcasted_iota(jnp.int32, (nsb, both), 0)
    forced = (jj == 0) | (jj == blk) | (jj == blk - 1)
    score = jnp.where(forced, jnp.inf, jnp.where(jj <= blk, jnp.concatenate(imp_t, axis=1), NEG_INF))
    rank = jnp.zeros((nsb, both), F32)
    for j2 in range(nsb):
        other = score[j2:j2 + 1, :]
        ahead = (other > score) | ((other == score) & (jj > j2))
        rank = rank + jnp.where(ahead, 1.0, 0.0)
    chosen = (rank < N_SELECT) & (score > NEG_INF) & (jj < blk - 1)
    bias_t = jnp.concatenate([jnp.where(chosen, 0.0, MASK_BIAS), jnp.zeros((LANES - nsb, both), F32)], axis=0)
    eye = jnp.where(lax.broadcasted_iota(jnp.int32, (both, both), 0)
                    == lax.broadcasted_iota(jnp.int32, (both, both), 1), 1.0, 0.0).astype(BF16)
    bias_q = _dot_nt(eye, bias_t.astype(BF16)).astype(BF16)

    d_lo = pl.multiple_of(jnp.maximum(s0 - SEL_BLOCK, 0), SEL_BLOCK)
    diag_mask = d_lo + lax.broadcasted_iota(jnp.int32, (1, 2 * SEL_BLOCK), 1) <= t_rows
    w_lo = jnp.minimum(jnp.maximum(s0 - WINDOW, 0), seq - WIN_KEYS)
    w_lo = pl.multiple_of(w_lo, SEL_BLOCK)
    kpos_w = w_lo + lax.broadcasted_iota(jnp.int32, (1, WIN_KEYS), 1)
    win_bias = jnp.where((kpos_w <= t_rows) & (kpos_w > t_rows - WINDOW), 0.0, MASK_BIAS)
    n_chunks = (blk + SEL_KEY_CHUNK // SEL_BLOCK - 2) // (SEL_KEY_CHUNK // SEL_BLOCK)

    for h in range(NSA_KV_HEADS):
        s_d = jnp.where(diag_mask, _dot_nt(qs[h], ks_ref[0, pl.ds(d_lo, 2 * SEL_BLOCK), :]), NEG_INF)
        m0 = jnp.max(s_d, axis=-1, keepdims=True)
        p0 = jnp.exp(s_d - m0)
        l0 = jnp.sum(p0, axis=-1, keepdims=True)
        acc0 = _dot(p0.astype(BF16), vs_ref[0, pl.ds(d_lo, 2 * SEL_BLOCK), :])
        q_aug = jnp.concatenate([qs[h], jnp.concatenate([bias_q[h * qb:(h + 1) * qb]] * NSA_GROUP, axis=0)], axis=1)

        def sel_step(c, carry):
            m_prev, l_prev, acc = carry
            k0 = pl.multiple_of(c * SEL_KEY_CHUNK, SEL_KEY_CHUNK)
            k_aug = jnp.concatenate([ks_ref[0, pl.ds(k0, SEL_KEY_CHUNK), :], oh_ref[pl.ds(k0, SEL_KEY_CHUNK), :]],
                                    axis=1)
            s = _dot_nt(q_aug, k_aug)
            m_new = jnp.maximum(m_prev, jnp.max(s, axis=-1, keepdims=True))
            p = jnp.exp(s - m_new)
            alpha = jnp.exp(m_prev - m_new)
            l_new = alpha * l_prev + jnp.sum(p, axis=-1, keepdims=True)
            acc_new = alpha * acc + _dot(p.astype(BF16), vs_ref[0, pl.ds(k0, SEL_KEY_CHUNK), :])
            return m_new, l_new, acc_new

        _, l_sel, acc_sel = lax.fori_loop(0, n_chunks, sel_step, (m0, l0, acc0))
        o_sel = acc_sel / jnp.maximum(l_sel, 1e-30)

        s_w = _dot_nt(qs[h], kw_ref[0, pl.ds(w_lo, WIN_KEYS), :]) + win_bias
        e = jnp.exp(s_w - jnp.max(s_w, axis=-1, keepdims=True))
        o_win = (_dot(e.astype(BF16), vw_ref[0, pl.ds(w_lo, WIN_KEYS), :])
                 / jnp.maximum(jnp.sum(e, axis=-1, keepdims=True), 1e-30))

        mixed = []
        for g in range(NSA_GROUP):
            hq = h * NSA_GROUP + g
            r = slice(g * qb, (g + 1) * qb)
            mixed.append(_lane_col(gates, SM_GATE0 + 3 * hq) * o_cmp[h][r]
                         + _lane_col(gates, SM_GATE0 + 3 * hq + 1) * o_sel[r]
                         + _lane_col(gates, SM_GATE0 + 3 * hq + 2) * o_win[r])
        for pair in range(NSA_GROUP // 2):
            a, b = mixed[2 * pair], mixed[2 * pair + 1]
            if h == 0:
                b = pltpu.roll(b, NSA_DH, axis=1)
            else:
                a = pltpu.roll(a, NSA_DH, axis=1)
            c0 = (h * NSA_GROUP + 2 * pair) * NSA_DH
            o_ref[0, :, c0:c0 + LANES] = jnp.where(lane < NSA_DH, a, b)


def _nsa_attention(q, sm, kc, vc, kv):
    b, s, qw = q.shape
    qb = SEL_BLOCK
    ncmp = kc.shape[1]
    onehot = (jnp.arange(s)[:, None] // SEL_BLOCK == jnp.arange(LANES)[None, :]).astype(BF16)
    kv_spec = lambda col: pl.BlockSpec((1, s, LANES), lambda i, j: (i, 0, col))
    return pl.pallas_call(
        _nsa_kernel,
        grid=(b, s // qb),
        in_specs=[pl.BlockSpec((1, qb, qw), lambda i, j: (i, j, 0)),
                  pl.BlockSpec((1, qb, LANES), lambda i, j: (i, j, 0)),
                  pl.BlockSpec((1, ncmp, LANES), lambda i, j: (i, 0, 0)),
                  pl.BlockSpec((1, ncmp, LANES), lambda i, j: (i, 0, 0)),
                  kv_spec(2), kv_spec(3), kv_spec(4), kv_spec(5), _const_spec((s, LANES))],
        out_specs=pl.BlockSpec((1, qb, NSA_WIDTH), lambda i, j: (i, j, 0)),
        out_shape=jax.ShapeDtypeStruct((b, s, NSA_WIDTH), F32),
        compiler_params=_params("parallel", "arbitrary"),
        name="nsa_attention",
    )(q, sm, kc, vc, kv, kv, kv, kv, onehot)


def _unit_lower_inverse(a, r, c):
    n = a.shape[0]
    eye = jnp.where(r == c, 1.0, 0.0)
    blk16 = (r >> 4) == (c >> 4)
    mm = lambda x, y: _dot(x.astype(BF16), y.astype(BF16))
    n1 = jnp.where(blk16, -a, 0.0)
    n2 = mm(n1, n1)
    t = eye + n1
    x = mm(jnp.concatenate([t, n2], axis=0), n2)
    t, n4 = t + x[:n], x[n:]
    x = mm(jnp.concatenate([t, n4], axis=0), n4)
    t, n8 = t + x[:n], x[n:]
    t = t + mm(t, n8)
    v = mm(jnp.where(blk16, 0.0, a), t)
    x = mm(jnp.concatenate([t, v], axis=0), v)
    p, v2 = t - x[:n], x[n:]
    return p + mm(p, v2)


def _log_decay(sm, alog_ref, dtb_ref):
    return -jnp.exp(alog_ref[...]) * jax.nn.softplus(sm + dtb_ref[...])


def _gdn_prep_kernel(x_ref, halo_ref, cw_ref, sm_ref, alog_ref, dtb_ref, q_ref, k_ref, u_ref, w_ref):
    n = x_ref.shape[1]
    width = cw_ref.shape[0]
    row = lax.broadcasted_iota(jnp.int32, (GDN_HALO + n, 1), 0)
    keep = row >= jnp.where(pl.program_id(1) > 0, 0, GDN_HALO)

    def conv_silu(c0):
        cols = slice(c0, c0 + GDN_DH)
        xe = jnp.where(keep, jnp.concatenate([halo_ref[0, :, cols], x_ref[0, :, cols]], axis=0), 0.0)
        y = xe[GDN_HALO:] * cw_ref[width - 1:width, cols]
        for k in range(width - 1):
            y = y + pltpu.roll(xe, width - 1 - k, axis=0)[GDN_HALO:] * cw_ref[k:k + 1, cols]
        return jax.nn.silu(y)

    def l2norm(y):
        return y * lax.rsqrt(jnp.sum(y * y, axis=-1, keepdims=True) + EPS)

    r = lax.broadcasted_iota(jnp.int32, (n, n), 0)
    c = lax.broadcasted_iota(jnp.int32, (n, n), 1)
    in_chunk = (r >> GDN_CHUNK_SHIFT) == (c >> GDN_CHUNK_SHIFT)
    causal = in_chunk & (r >= c)
    hi = lax.Precision.HIGHEST
    sm = sm_ref[0]
    beta = jax.nn.sigmoid(sm)
    gcum = _dot(jnp.where(causal, 1.0, 0.0), _log_decay(sm, alog_ref, dtb_ref), precision=hi)
    pick = jnp.where(lax.broadcasted_iota(jnp.int32, (8, LANES), 1)
                     == lax.broadcasted_iota(jnp.int32, (8, LANES), 0) + SM_A0, 1.0, 0.0)
    gcum_rows = _dot_nt(pick, gcum, precision=hi)

    for h in range(GDN_HEADS):
        cols = slice(h * GDN_DH, (h + 1) * GDN_DH)
        q = l2norm(conv_silu(h * GDN_DH)) * (GDN_DH ** -0.5)
        k = l2norm(conv_silu(GDN_WIDTH + h * GDN_DH))
        v = conv_silu(2 * GDN_WIDTH + h * GDN_DH)
        q_ref[0, :, cols] = q
        k_ref[0, :, cols] = k
        gc = _lane_col(gcum, SM_A0 + h)
        bcol = _lane_col(beta, SM_BETA0 + h)
        decay = jnp.exp(jnp.where(causal, gc - gcum_rows[h:h + 1, :], NEG_INF))
        kb = k * bcol
        a_mat = jnp.where(r > c, _dot_nt(kb.astype(BF16), k.astype(BF16)) * decay, 0.0)
        t_inv = _unit_lower_inverse(a_mat, r, c).astype(BF16)
        uw = _dot(t_inv, jnp.concatenate([v * bcol, kb * jnp.exp(gc)], axis=1).astype(BF16))
        u_ref[0, :, cols] = uw[:, :GDN_DH]
        w_ref[0, :, cols] = uw[:, GDN_DH:].astype(BF16)


def _gdn_prep(qkv, cw, sm, alog_vec, dtb_vec):
    b, s, width = qkv.shape
    n = GDN_PREP_TILE
    per = n // GDN_HALO
    tok = lambda w: pl.BlockSpec((1, n, w), lambda i, j: (i, j, 0))
    out = lambda dt: jax.ShapeDtypeStruct((b, s, GDN_WIDTH), dt)
    return pl.pallas_call(
        _gdn_prep_kernel,
        grid=(b, s // n),
        in_specs=[tok(width),
                  pl.BlockSpec((1, GDN_HALO, width), lambda i, j: (i, jnp.maximum(j * per - 1, 0), 0)),
                  _const_spec(cw.shape), tok(LANES), _const_spec((1, LANES)), _const_spec((1, LANES))],
        out_specs=[tok(GDN_WIDTH)] * 4,
        out_shape=[out(F32), out(F32), out(F32), out(BF16)],
        compiler_params=_params("parallel", "parallel"),
        name="gdn_prep",
    )(qkv, qkv, cw, sm, alog_vec, dtb_vec)


def _gdn_scan_kernel(q_ref, k_ref, u_ref, w_ref, z_ref, sm_ref, alog_ref, dtb_ref, ng_ref, o_ref, state_ref):
    c = GDN_CHUNK
    n_chunks = q_ref.shape[1] // c

    @pl.when(pl.program_id(1) == 0)
    def _():
        state_ref[...] = jnp.zeros_like(state_ref)

    r = lax.broadcasted_iota(jnp.int32, (c, c), 0)
    cc = lax.broadcasted_iota(jnp.int32, (c, c), 1)
    tril = r >= cc
    cum_mat = jnp.where(tril, 1.0, 0.0)
    pick = jnp.where(lax.broadcasted_iota(jnp.int32, (8, LANES), 1)
                     == lax.broadcasted_iota(jnp.int32, (8, LANES), 0) + SM_A0, 1.0, 0.0)
    hi = lax.Precision.HIGHEST

    def chunk_step(ci, _):
        t0 = pl.multiple_of(ci * c, c)
        gcum = _dot(cum_mat, _log_decay(sm_ref[0, pl.ds(t0, c), :], alog_ref, dtb_ref), precision=hi)
        gcum_rows = _dot_nt(pick, gcum, precision=hi)
        for h in range(GDN_HEADS):
            cols = slice(h * GDN_DH, (h + 1) * GDN_DH)
            q = q_ref[0, pl.ds(t0, c), cols]
            k = k_ref[0, pl.ds(t0, c), cols]
            gc = _lane_col(gcum, SM_A0 + h)
            decay = jnp.exp(jnp.where(tril, gc - gcum_rows[h:h + 1, :], NEG_INF))
            attn = (_dot_nt(q.astype(BF16), k.astype(BF16)) * decay).astype(BF16)
            g_last = gc[c - 1:c, :]
            k_dec = (k * jnp.exp(g_last - gc)).astype(BF16)
            q_dec = (q * jnp.exp(gc)).astype(BF16)
            state = state_ref[h]
            sb = state.astype(BF16)
            v_new = (u_ref[0, pl.ds(t0, c), cols] - _dot(w_ref[0, pl.ds(t0, c), cols], sb)).astype(BF16)
            o = _dot(q_dec, sb) + _dot(attn, v_new)
            state_ref[h] = state * jnp.exp(g_last) + _dot_tn(k_dec, v_new)
            y = o * lax.rsqrt(jnp.mean(o * o, axis=-1, keepdims=True) + EPS) * ng_ref[...]
            o_ref[0, pl.ds(t0, c), cols] = y * jax.nn.silu(z_ref[0, pl.ds(t0, c), cols])
        return 0

    lax.fori_loop(0, n_chunks, chunk_step, 0)


def _gdn_scan(q, k, u, w, z, sm, alog_vec, dtb_vec, norm_g):
    b, s, _ = q.shape
    ts = GDN_SEQ_TILE
    tok = lambda wd: pl.BlockSpec((1, ts, wd), lambda i, j: (i, j, 0))
    return pl.pallas_call(
        _gdn_scan_kernel,
        grid=(b, s // ts),
        in_specs=[tok(GDN_WIDTH)] * 5 + [tok(LANES), _const_spec((1, LANES)), _const_spec((1, LANES)),
                                         _const_spec((1, GDN_DH))],
        out_specs=tok(GDN_WIDTH),
        out_shape=jax.ShapeDtypeStruct((b, s, GDN_WIDTH), F32),
        scratch_shapes=[pltpu.VMEM((GDN_HEADS, GDN_DH, GDN_DH), F32)],
        compiler_params=_params("parallel", "arbitrary"),
        name="gdn_scan",
    )(q, k, u, w, z, sm, alog_vec, dtb_vec, norm_g)


def _outproj_kernel(on_ref, og_ref, x_ref, g_ref, wa_ref, wb_ref, o_ref):
    on = on_ref[...]
    ms = jnp.mean(on * on, axis=-1, keepdims=True)
    hn = (on * lax.rsqrt(ms + EPS) * g_ref[...]).astype(BF16)
    o_ref[...] = x_ref[...] + _dot(hn, wa_ref[...]) + _dot(og_ref[...].astype(BF16), wb_ref[...])


def _outproj(o_nsa, o_gdn, x2, g, wa, wb):
    n, d = x2.shape
    tm = ROW_TILE
    row = lambda w: pl.BlockSpec((tm, w), lambda i: (i, 0))
    return pl.pallas_call(
        _outproj_kernel,
        grid=(n // tm,),
        in_specs=[row(o_nsa.shape[1]), row(o_gdn.shape[1]), row(d),
                  _const_spec(g.shape), _const_spec(wa.shape), _const_spec(wb.shape)],
        out_specs=row(d),
        out_shape=jax.ShapeDtypeStruct((n, d), F32),
        compiler_params=_params("parallel"),
        name="outproj",
    )(o_nsa, o_gdn, x2, g, wa, wb)


def _ffn_kernel(x_ref, halo_ref, g_ref, wup_ref, cw_ref, wdn_ref, o_ref):
    tm = x_ref.shape[1]
    d_ff = wdn_ref.shape[0]
    width = cw_ref.shape[0]
    x = x_ref[0]
    xin = jnp.concatenate([halo_ref[0], x], axis=0)
    ms = jnp.mean(xin * xin, axis=-1, keepdims=True)
    h = (xin * lax.rsqrt(ms + EPS) * g_ref[...]).astype(BF16)
    row = lax.broadcasted_iota(jnp.int32, (FFN_HALO + tm, 1), 0)
    keep = row >= jnp.where(pl.program_id(1) > 0, 0, FFN_HALO)

    def conv(u, c0):
        u = jnp.where(keep, u, 0.0)
        y = u[FFN_HALO:] * cw_ref[width - 1:width, c0:c0 + FFN_COL_CHUNK]
        for k in range(width - 1):
            shifted = pltpu.roll(u, width - 1 - k, axis=0)[FFN_HALO:]
            y = y + shifted * cw_ref[k:k + 1, c0:c0 + FFN_COL_CHUNK]
        return y

    acc = jnp.zeros((tm, x.shape[1]), F32)
    for j in range(d_ff // FFN_COL_CHUNK):
        c0 = j * FFN_COL_CHUNK
        gate = conv(_dot(h, wup_ref[:, c0:c0 + FFN_COL_CHUNK]), c0)
        up = conv(_dot(h, wup_ref[:, d_ff + c0:d_ff + c0 + FFN_COL_CHUNK]), d_ff + c0)
        act = (jax.nn.silu(gate) * up).astype(BF16)
        acc = acc + _dot(act, wdn_ref[c0:c0 + FFN_COL_CHUNK, :])
    o_ref[0] = x + acc


def _ffn(x3, g, wup, cw, wdn):
    b, s, d = x3.shape
    tm = ROW_TILE
    per = tm // FFN_HALO
    return pl.pallas_call(
        _ffn_kernel,
        grid=(b, s // tm),
        in_specs=[pl.BlockSpec((1, tm, d), lambda i, j: (i, j, 0)),
                  pl.BlockSpec((1, FFN_HALO, d), lambda i, j: (i, jnp.maximum(j * per - 1, 0), 0)),
                  _const_spec(g.shape), _const_spec(wup.shape), _const_spec(cw.shape), _const_spec(wdn.shape)],
        out_specs=pl.BlockSpec((1, tm, d), lambda i, j: (i, j, 0)),
        out_shape=jax.ShapeDtypeStruct((b, s, d), F32),
        compiler_params=_params("parallel", "parallel"),
        name="convffn",
    )(x3, x3, g, wup, cw, wdn)


def _rmsnorm_kernel(x_ref, g_ref, o_ref):
    x = x_ref[...]
    o_ref[...] = x * lax.rsqrt(jnp.mean(x * x, axis=-1, keepdims=True) + EPS) * g_ref[...]


def _rmsnorm(x2, g):
    n, d = x2.shape
    tm = ROW_TILE
    return pl.pallas_call(
        _rmsnorm_kernel,
        grid=(n // tm,),
        in_specs=[pl.BlockSpec((tm, d), lambda i: (i, 0)), _const_spec(g.shape)],
        out_specs=pl.BlockSpec((tm, d), lambda i: (i, 0)),
        out_shape=jax.ShapeDtypeStruct((n, d), F32),
        compiler_params=_params("parallel"),
        name="final_rmsnorm",
    )(x2, g)


def _split_in_weights(w_in):
    d = w_in.shape[0]
    o = 0
    wq = w_in[:, o:o + NSA_WIDTH]; o += NSA_WIDTH
    wkv = w_in[:, o:o + 6 * NSA_KV_WIDTH]; o += 6 * NSA_KV_WIDTH
    wgate = w_in[:, o:o + 3 * NSA_HEADS]; o += 3 * NSA_HEADS
    wgdn = w_in[:, o:o + 3 * GDN_WIDTH]; o += 3 * GDN_WIDTH
    wz = w_in[:, o:o + GDN_WIDTH]; o += GDN_WIDTH
    wb = w_in[:, o:o + GDN_HEADS]; o += GDN_HEADS
    wa = w_in[:, o:o + GDN_HEADS]
    wq = wq.reshape(d, NSA_HEADS, NSA_DH) * (NSA_DH ** -0.5)
    zero = jnp.zeros_like(wq)
    kv_head = (jnp.arange(NSA_HEADS) // NSA_GROUP)[None, :, None]
    wq_pad = jnp.concatenate([jnp.where(kv_head == 0, wq, zero), jnp.where(kv_head == 1, wq, zero)], axis=-1)
    wq_pad = wq_pad.reshape(d, NSA_HEADS * LANES)
    wsm = jnp.concatenate([wgate, wb, wa, jnp.zeros((d, LANES - SM_A0 - GDN_HEADS), w_in.dtype)], axis=-1)
    return tuple(w.astype(BF16) for w in (wq_pad, wkv, wgdn, wz, wsm))


def _lane_vec(v, offset):
    return jnp.zeros((1, LANES), F32).at[0, offset:offset + v.shape[0]].set(v.astype(F32))


def _head_padded_w2(w2):
    z = jnp.zeros_like(w2)
    return jnp.stack([jnp.concatenate([w2, z], axis=-1), jnp.concatenate([z, w2], axis=-1)]).astype(BF16)


def kernel(x, norm_mix, w_in, cmp_pos_k, cmp_pos_v, cmp_k_w1, cmp_k_b1, cmp_k_w2, cmp_v_w1, cmp_v_b1, cmp_v_w2,
           nsa_norm, gdn_conv, gdn_a_log, gdn_dt_bias, gdn_norm, w_out, norm_ffn, ffn_up, ffn_conv, ffn_down,
           norm_final):
    b, s, d = x.shape
    depth = w_in.shape[0]
    group = CMP_BLOCK // 2
    assert CMP_STRIDE == group and s % GDN_SEQ_TILE == 0 and s >= WIN_KEYS and (b * s) % ROW_TILE == 0
    x2 = x.reshape(b * s, d)
    for l in range(depth):
        wq, wkv, wgdn, wz, wsm = _split_in_weights(w_in[l])
        q, kv, gdn, z, sm = _inproj(x2, norm_mix[l][None, :], wq, wkv, wgdn, wz, wsm)
        q, kv, gdn, z, sm = (a.reshape(b, s, a.shape[-1]) for a in (q, kv, gdn, z, sm))

        groups = kv[:, :, :2 * NSA_KV_WIDTH].reshape(b, s, 2 * NSA_KV_HEADS, NSA_DH)
        groups = groups.transpose(0, 2, 1, 3).reshape(b, 2 * NSA_KV_HEADS, s // group, group * NSA_DH)
        kc, vc = _compress(
            groups, cmp_pos_k[l].reshape(1, -1), cmp_pos_v[l].reshape(1, -1),
            cmp_k_w1[l].astype(BF16), cmp_k_b1[l][None, :], _head_padded_w2(cmp_k_w2[l]),
            cmp_v_w1[l].astype(BF16), cmp_v_b1[l][None, :], _head_padded_w2(cmp_v_w2[l]))
        o_nsa = _nsa_attention(q, sm, kc, vc, kv)

        alog_vec, dtb_vec = _lane_vec(gdn_a_log[l], SM_A0), _lane_vec(gdn_dt_bias[l], SM_A0)
        gq, gk, gu, gw = _gdn_prep(gdn, gdn_conv[l], sm, alog_vec, dtb_vec)
        o_gdn = _gdn_scan(gq, gk, gu, gw, z, sm, alog_vec, dtb_vec, gdn_norm[l][None, :])

        w_o = w_out[l].astype(BF16)
        x2 = _outproj(o_nsa.reshape(b * s, -1), o_gdn.reshape(b * s, -1), x2, nsa_norm[l][None, :],
                      w_o[:NSA_WIDTH], w_o[NSA_WIDTH:])
        x2 = _ffn(x2.reshape(b, s, d), norm_ffn[l][None, :], ffn_up[l].astype(BF16), ffn_conv[l],
                  ffn_down[l].astype(BF16)).reshape(b * s, d)
    return _rmsnorm(x2, norm_final[None, :]).reshape(b, s, d)
```

```python
import functools

import jax
import jax.numpy as jnp
from jax import lax
from jax.experimental import pallas as pl
from jax.experimental.pallas import tpu as pltpu

F32 = jnp.float32
BF16 = jnp.bfloat16
EPS = 1e-6
NEG_INF = float("-inf")
MASK_BIAS = -1e30

NSA_HEADS = 8
NSA_KV_HEADS = 2
NSA_GROUP = NSA_HEADS // NSA_KV_HEADS
NSA_DH = 64
NSA_WIDTH = NSA_HEADS * NSA_DH
NSA_KV_WIDTH = NSA_KV_HEADS * NSA_DH
CMP_BLOCK = 32
CMP_STRIDE = 16
SEL_BLOCK = 64
N_SELECT = 8
WINDOW = 512
GDN_HEADS = 4
GDN_DH = 128
GDN_WIDTH = GDN_HEADS * GDN_DH
GDN_CHUNK = 64
GDN_CHUNK_SHIFT = GDN_CHUNK.bit_length() - 1

LANES = 128
VMEM_LIMIT_BYTES = 56 * 1024 * 1024

ROW_TILE = 512
FFN_HALO = 8
FFN_COL_CHUNK = 256
SEL_KEY_CHUNK = 512
WIN_KEYS = WINDOW + 2 * SEL_BLOCK
GDN_SEQ_TILE = 256
GDN_SCAN_BATCH = 4
GDN_PREP_TILE = 256
GDN_HALO = 8

SM_GATE0 = 0
SM_BETA0 = 3 * NSA_HEADS
SM_A0 = SM_BETA0 + GDN_HEADS


def _dot(a, b, precision=None):
    return jnp.dot(a, b, preferred_element_type=F32, precision=precision)


def _dot_nt(a, b, precision=None):
    return lax.dot_general(a, b, (((1,), (1,)), ((), ())), preferred_element_type=F32, precision=precision)


def _dot_tn(a, b, precision=None):
    return lax.dot_general(a, b, (((0,), (0,)), ((), ())), preferred_element_type=F32, precision=precision)


def _lane_col(x, idx):
    lane = lax.broadcasted_iota(jnp.int32, x.shape, 1)
    return jnp.sum(jnp.where(lane == idx, x, 0.0), axis=-1, keepdims=True)


def _const_spec(shape):
    zeros = (0,) * len(shape)
    return pl.BlockSpec(shape, lambda *_: zeros, pipeline_mode=pl.Buffered(1))


def _params(*semantics):
    return pltpu.CompilerParams(dimension_semantics=semantics, vmem_limit_bytes=VMEM_LIMIT_BYTES)


def _inproj_kernel(x_ref, g_ref, wq_ref, wkv_ref, wgdn_ref, wz_ref, wsm_ref,
                   q_ref, kv_ref, gdn_ref, z_ref, sm_ref):
    x = x_ref[...]
    ms = jnp.mean(x * x, axis=-1, keepdims=True)
    h = (x * lax.rsqrt(ms + EPS) * g_ref[...]).astype(BF16)
    q_ref[...] = _dot(h, wq_ref[...]).astype(BF16)
    kv_ref[...] = _dot(h, wkv_ref[...]).astype(BF16)
    gdn_ref[...] = _dot(h, wgdn_ref[...])
    z_ref[...] = _dot(h, wz_ref[...])
    sm_ref[...] = _dot(h, wsm_ref[...])


def _inproj(x2, g, wq, wkv, wgdn, wz, wsm):
    n, d = x2.shape
    tm = ROW_TILE
    row = lambda w: pl.BlockSpec((tm, w), lambda i: (i, 0))
    widths = (wq.shape[1], wkv.shape[1], wgdn.shape[1], wz.shape[1], wsm.shape[1])
    dtypes = (BF16, BF16, F32, F32, F32)
    return pl.pallas_call(
        _inproj_kernel,
        grid=(n // tm,),
        in_specs=[row(d), _const_spec((1, d))] + [_const_spec(w.shape) for w in (wq, wkv, wgdn, wz, wsm)],
        out_specs=[row(w) for w in widths],
        out_shape=[jax.ShapeDtypeStruct((n, w), dt) for w, dt in zip(widths, dtypes)],
        compiler_params=_params("parallel"),
        name="inproj",
    )(x2, g, wq, wkv, wgdn, wz, wsm)


def _compress_kernel(g_ref, posk_ref, posv_ref, w1k_ref, b1k_ref, w2k_ref, w1v_ref, b1v_ref, w2v_ref,
                     kc_ref, vc_ref):
    ng = g_ref.shape[2]
    half = w1k_ref.shape[0] // 2
    row = lax.broadcasted_iota(jnp.int32, (ng, 1), 0)
    for sel, (pos_ref, w1_ref, b1_ref, w2_ref, out_ref) in enumerate(
            ((posk_ref, w1k_ref, b1k_ref, w2k_ref, kc_ref), (posv_ref, w1v_ref, b1v_ref, w2v_ref, vc_ref))):
        w1 = w1_ref[...]
        pos8 = jnp.broadcast_to(pos_ref[...], (8, pos_ref.shape[1])).astype(BF16)
        bias = _dot(pos8, w1)[0:1, :] + b1_ref[...]
        acc = jnp.zeros((ng, LANES), F32)
        for h in range(NSA_KV_HEADS):
            grp = g_ref[0, sel * NSA_KV_HEADS + h]
            top = _dot(grp, w1[:half])
            bot = _dot(grp, w1[half:])
            hid = top + pltpu.roll(bot, ng - 1, axis=0) + bias
            act = jax.nn.gelu(hid, approximate=True).astype(BF16)
            acc = acc + _dot(act, w2_ref[h])
        out_ref[0] = jnp.where(row < ng - 1, acc, 0.0).astype(BF16)


def _compress(groups, posk, posv, w1k, b1k, w2k, w1v, b1v, w2v):
    b, _, ng, gw = groups.shape
    out = jax.ShapeDtypeStruct((b, ng, LANES), BF16)
    return pl.pallas_call(
        _compress_kernel,
        grid=(b,),
        in_specs=[pl.BlockSpec((1, 2 * NSA_KV_HEADS, ng, gw), lambda i: (i, 0, 0, 0))]
        + [_const_spec(a.shape) for a in (posk, posv, w1k, b1k, w2k, w1v, b1v, w2v)],
        out_specs=[pl.BlockSpec((1, ng, LANES), lambda i: (i, 0, 0))] * 2,
        out_shape=[out, out],
        compiler_params=_params("parallel"),
        name="nsa_compress",
    )(groups, posk, posv, w1k, b1k, w2k, w1v, b1v, w2v)


def _softmax_parts(s, mask):
    sm = jnp.where(mask, s, NEG_INF)
    m = jnp.max(sm, axis=-1, keepdims=True)
    m = jnp.where(jnp.isfinite(m), m, 0.0)
    e = jnp.where(mask, jnp.exp(sm - m), 0.0)
    return e, jnp.sum(e, axis=-1, keepdims=True)


def _nsa_kernel(q_ref, sm_ref, kc_ref, vc_ref, ks_ref, vs_ref, kw_ref, vw_ref, oh_ref, o_ref):
    qb = q_ref.shape[1]
    seq = ks_ref.shape[1]
    ncmp = kc_ref.shape[1]
    nsb = seq // SEL_BLOCK
    rows = NSA_HEADS * qb
    both = NSA_KV_HEADS * qb
    blk = pl.program_id(1)
    s0 = blk * qb

    gates = jax.nn.sigmoid(sm_ref[0])
    q_all = q_ref[0]
    t_rows = s0 + (lax.broadcasted_iota(jnp.int32, (rows, 1), 0) & (qb - 1))
    lane = lax.broadcasted_iota(jnp.int32, (qb, LANES), 1)

    jn = lax.broadcasted_iota(jnp.int32, (nsb, ncmp), 0)
    nn = lax.broadcasted_iota(jnp.int32, (nsb, ncmp), 1)
    overlap_t = jnp.where((nn * CMP_STRIDE < jn * SEL_BLOCK + SEL_BLOCK)
                          & (jn * SEL_BLOCK < nn * CMP_STRIDE + CMP_BLOCK) & (nn < ncmp - 1), 1.0, 0.0)
    n_cmp_idx = lax.broadcasted_iota(jnp.int32, (1, ncmp), 1)
    cmp_mask = (n_cmp_idx * CMP_STRIDE + (CMP_BLOCK - 1) <= t_rows) & (n_cmp_idx < ncmp - 1)
    qs = jnp.concatenate([q_all[:, hq * LANES:(hq + 1) * LANES] for hq in range(NSA_HEADS)], axis=0)
    e, l = _softmax_parts(_dot_nt(qs, kc_ref[0]), cmp_mask)
    p_cmp = e / jnp.maximum(l, 1e-30)
    o_cmp = _dot(p_cmp.astype(BF16), vc_ref[0])
    p_sum = []
    for h in range(NSA_KV_HEADS):
        acc = p_cmp[h * NSA_GROUP * qb:(h * NSA_GROUP + 1) * qb]
        for g in range(1, NSA_GROUP):
            acc = acc + p_cmp[(h * NSA_GROUP + g) * qb:(h * NSA_GROUP + g + 1) * qb]
        p_sum.append(acc)
    imp_t = _dot_nt(overlap_t, jnp.concatenate(p_sum, axis=0), precision=lax.Precision.HIGHEST)

    jj = lax.broadcasted_iota(jnp.int32, (nsb, both), 0)
    forced = (jj == 0) | (jj == blk) | (jj == blk - 1)
    score = jnp.where(forced, jnp.inf, jnp.where(jj <= blk, imp_t, NEG_INF))
    rank = jnp.zeros((nsb, both), F32)
    for j2 in range(nsb):
        other = score[j2:j2 + 1, :]
        ahead = (other > score) | ((other == score) & (jj > j2))
        rank = rank + jnp.where(ahead, 1.0, 0.0)
    chosen = (rank < N_SELECT) & (score > NEG_INF) & (jj < blk - 1)
    bias_t = jnp.concatenate([jnp.where(chosen, 0.0, MASK_BIAS), jnp.zeros((LANES - nsb, both), F32)], axis=0)
    eye = jnp.where(lax.broadcasted_iota(jnp.int32, (both, both), 0)
                    == lax.broadcasted_iota(jnp.int32, (both, both), 1), 1.0, 0.0).astype(BF16)
    bias_q = _dot_nt(eye, bias_t.astype(BF16)).astype(BF16)

    d_lo = pl.multiple_of(jnp.maximum(s0 - SEL_BLOCK, 0), SEL_BLOCK)
    diag_mask = d_lo + lax.broadcasted_iota(jnp.int32, (1, 2 * SEL_BLOCK), 1) <= t_rows
    w_lo = jnp.minimum(jnp.maximum(s0 - WINDOW, 0), seq - WIN_KEYS)
    w_lo = pl.multiple_of(w_lo, SEL_BLOCK)
    kpos_w = w_lo + lax.broadcasted_iota(jnp.int32, (1, WIN_KEYS), 1)
    win_bias = jnp.where((kpos_w <= t_rows) & (kpos_w > t_rows - WINDOW), 0.0, MASK_BIAS)
    n_chunks = (blk + SEL_KEY_CHUNK // SEL_BLOCK - 2) // (SEL_KEY_CHUNK // SEL_BLOCK)

    s_w = _dot_nt(qs, kw_ref[0, pl.ds(w_lo, WIN_KEYS), :]) + win_bias
    e = jnp.exp(s_w - jnp.max(s_w, axis=-1, keepdims=True))
    o_win = (_dot(e.astype(BF16), vw_ref[0, pl.ds(w_lo, WIN_KEYS), :])
             / jnp.maximum(jnp.sum(e, axis=-1, keepdims=True), 1e-30))

    s_d = jnp.where(diag_mask, _dot_nt(qs, ks_ref[0, pl.ds(d_lo, 2 * SEL_BLOCK), :]), NEG_INF)
    m0 = jnp.max(s_d, axis=-1, keepdims=True)
    p0 = jnp.exp(s_d - m0)
    l0 = jnp.sum(p0, axis=-1, keepdims=True)
    acc0 = _dot(p0.astype(BF16), vs_ref[0, pl.ds(d_lo, 2 * SEL_BLOCK), :])
    bias_rows = jnp.concatenate([bias_q[h * qb:(h + 1) * qb] for h in range(NSA_KV_HEADS)
                                 for _ in range(NSA_GROUP)], axis=0)
    q_aug = jnp.concatenate([qs, bias_rows], axis=1)

    def sel_step(c, carry):
        m_prev, l_prev, acc = carry
        k0 = pl.multiple_of(c * SEL_KEY_CHUNK, SEL_KEY_CHUNK)
        k_aug = jnp.concatenate([ks_ref[0, pl.ds(k0, SEL_KEY_CHUNK), :], oh_ref[pl.ds(k0, SEL_KEY_CHUNK), :]],
                                axis=1)
        s = _dot_nt(q_aug, k_aug)
        m_new = jnp.maximum(m_prev, jnp.max(s, axis=-1, keepdims=True))
        p = jnp.exp(s - m_new)
        alpha = jnp.exp(m_prev - m_new)
        l_new = alpha * l_prev + jnp.sum(p, axis=-1, keepdims=True)
        acc_new = alpha * acc + _dot(p.astype(BF16), vs_ref[0, pl.ds(k0, SEL_KEY_CHUNK), :])
        return m_new, l_new, acc_new

    _, l_sel, acc_sel = lax.fori_loop(0, n_chunks, sel_step, (m0, l0, acc0))
    o_sel = acc_sel / jnp.maximum(l_sel, 1e-30)

    mixed = []
    for hq in range(NSA_HEADS):
        r = slice(hq * qb, (hq + 1) * qb)
        mixed.append(_lane_col(gates, SM_GATE0 + 3 * hq) * o_cmp[r]
                     + _lane_col(gates, SM_GATE0 + 3 * hq + 1) * o_sel[r]
                     + _lane_col(gates, SM_GATE0 + 3 * hq + 2) * o_win[r])
    for pair in range(NSA_HEADS // 2):
        a, b = mixed[2 * pair], mixed[2 * pair + 1]
        if 2 * pair // NSA_GROUP == 0:
            b = pltpu.roll(b, NSA_DH, axis=1)
        else:
            a = pltpu.roll(a, NSA_DH, axis=1)
        o_ref[0, :, pair * LANES:(pair + 1) * LANES] = jnp.where(lane < NSA_DH, a, b)


def _nsa_attention(q, sm, kc, vc, kv):
    b, s, qw = q.shape
    qb = SEL_BLOCK
    ncmp = kc.shape[1]
    onehot = (jnp.arange(s)[:, None] // SEL_BLOCK == jnp.arange(LANES)[None, :]).astype(BF16)
    kv_spec = lambda col: pl.BlockSpec((1, s, LANES), lambda i, j: (i, 0, col))
    return pl.pallas_call(
        _nsa_kernel,
        grid=(b, s // qb),
        in_specs=[pl.BlockSpec((1, qb, qw), lambda i, j: (i, j, 0)),
                  pl.BlockSpec((1, qb, LANES), lambda i, j: (i, j, 0)),
                  pl.BlockSpec((1, ncmp, LANES), lambda i, j: (i, 0, 0)),
                  pl.BlockSpec((1, ncmp, LANES), lambda i, j: (i, 0, 0)),
                  kv_spec(2), kv_spec(3), kv_spec(4), kv_spec(5), _const_spec((s, LANES))],
        out_specs=pl.BlockSpec((1, qb, NSA_WIDTH), lambda i, j: (i, j, 0)),
        out_shape=jax.ShapeDtypeStruct((b, s, NSA_WIDTH), F32),
        compiler_params=_params("parallel", "arbitrary"),
        name="nsa_attention",
    )(q, sm, kc, vc, kv, kv, kv, kv, onehot)


def _unit_lower_inverse(a, r, c):
    n = a.shape[0]
    eye = jnp.where(r == c, 1.0, 0.0)
    blk16 = (r >> 4) == (c >> 4)
    mm = lambda x, y: _dot(x.astype(BF16), y.astype(BF16))
    n1 = jnp.where(blk16, -a, 0.0)
    n2 = mm(n1, n1)
    t = eye + n1
    x = mm(jnp.concatenate([t, n2], axis=0), n2)
    t, n4 = t + x[:n], x[n:]
    x = mm(jnp.concatenate([t, n4], axis=0), n4)
    t, n8 = t + x[:n], x[n:]
    t = t + mm(t, n8)
    v = mm(jnp.where(blk16, 0.0, a), t)
    x = mm(jnp.concatenate([t, v], axis=0), v)
    p, v2 = t - x[:n], x[n:]
    return p + mm(p, v2)


def _log_decay(sm, alog_ref, dtb_ref):
    return -jnp.exp(alog_ref[...]) * jax.nn.softplus(sm + dtb_ref[...])


def _gdn_prep_kernel(x_ref, halo_ref, cw_ref, sm_ref, alog_ref, dtb_ref,
                     qd_ref, kdt_ref, at_ref, u_ref, w_ref, gcum_ref):
    n = x_ref.shape[1]
    width = cw_ref.shape[0]
    row = lax.broadcasted_iota(jnp.int32, (GDN_HALO + n, 1), 0)
    keep = row >= jnp.where(pl.program_id(1) > 0, 0, GDN_HALO)

    def conv_silu(c0):
        cols = slice(c0, c0 + GDN_DH)
        xe = jnp.where(keep, jnp.concatenate([halo_ref[0, :, cols], x_ref[0, :, cols]], axis=0), 0.0)
        y = xe[GDN_HALO:] * cw_ref[width - 1:width, cols]
        for k in range(width - 1):
            y = y + pltpu.roll(xe, width - 1 - k, axis=0)[GDN_HALO:] * cw_ref[k:k + 1, cols]
        return jax.nn.silu(y)

    def l2norm(y):
        return y * lax.rsqrt(jnp.sum(y * y, axis=-1, keepdims=True) + EPS)

    r = lax.broadcasted_iota(jnp.int32, (n, n), 0)
    c = lax.broadcasted_iota(jnp.int32, (n, n), 1)
    in_chunk = (r >> GDN_CHUNK_SHIFT) == (c >> GDN_CHUNK_SHIFT)
    causal = in_chunk & (r >= c)
    hi = lax.Precision.HIGHEST
    sm = sm_ref[0]
    beta = jax.nn.sigmoid(sm)
    g = _log_decay(sm, alog_ref, dtb_ref)
    gcum = _dot(jnp.where(causal, 1.0, 0.0), g, precision=hi)
    gtot = _dot(jnp.where(in_chunk, 1.0, 0.0), g, precision=hi)
    gcum_ref[0] = gcum
    pick = jnp.where(lax.broadcasted_iota(jnp.int32, (8, LANES), 1)
                     == lax.broadcasted_iota(jnp.int32, (8, LANES), 0) + SM_A0, 1.0, 0.0)
    gcum_rows = _dot_nt(pick, gcum, precision=hi)

    for h in range(GDN_HEADS):
        cols = slice(h * GDN_DH, (h + 1) * GDN_DH)
        q = l2norm(conv_silu(h * GDN_DH)) * (GDN_DH ** -0.5)
        k = l2norm(conv_silu(GDN_WIDTH + h * GDN_DH))
        v = conv_silu(2 * GDN_WIDTH + h * GDN_DH)
        gc = _lane_col(gcum, SM_A0 + h)
        bcol = _lane_col(beta, SM_BETA0 + h)
        decay = jnp.exp(jnp.where(causal, gc - gcum_rows[h:h + 1, :], NEG_INF))
        kb = k * bcol
        egc = jnp.exp(gc)
        qd_ref[0, :, cols] = (q * egc).astype(BF16)
        kdt_ref[0, cols, :] = (k * jnp.exp(_lane_col(gtot, SM_A0 + h) - gc)).T.astype(BF16)
        grams = _dot_nt(jnp.concatenate([kb, q], axis=0).astype(BF16), k.astype(BF16))
        attn = (grams[n:] * decay).astype(BF16)
        for ch in range(n // GDN_CHUNK):
            lt = ch * GDN_CHUNK // LANES
            at_ref[0, ch * GDN_CHUNK:(ch + 1) * GDN_CHUNK, cols] = (
                attn[ch * GDN_CHUNK:(ch + 1) * GDN_CHUNK, lt * LANES:(lt + 1) * LANES])
        a_mat = jnp.where(r > c, grams[:n] * decay, 0.0)
        t_inv = _unit_lower_inverse(a_mat, r, c).astype(BF16)
        uw = _dot(t_inv, jnp.concatenate([v * bcol, kb * egc], axis=1).astype(BF16))
        u_ref[0, :, cols] = uw[:, :GDN_DH]
        w_ref[0, :, cols] = uw[:, GDN_DH:].astype(BF16)


def _gdn_prep(qkv, cw, sm, alog_vec, dtb_vec):
    b, s, width = qkv.shape
    n = GDN_PREP_TILE
    per = n // GDN_HALO
    tok = lambda w: pl.BlockSpec((1, n, w), lambda i, j: (i, j, 0))
    out = lambda dt: jax.ShapeDtypeStruct((b, s, GDN_WIDTH), dt)
    return pl.pallas_call(
        _gdn_prep_kernel,
        grid=(b, s // n),
        in_specs=[tok(width),
                  pl.BlockSpec((1, GDN_HALO, width), lambda i, j: (i, jnp.maximum(j * per - 1, 0), 0)),
                  _const_spec(cw.shape), tok(LANES), _const_spec((1, LANES)), _const_spec((1, LANES))],
        out_specs=[tok(GDN_WIDTH), pl.BlockSpec((1, GDN_WIDTH, n), lambda i, j: (i, 0, j))]
        + [tok(GDN_WIDTH)] * 3 + [tok(LANES)],
        out_shape=[out(BF16), jax.ShapeDtypeStruct((b, GDN_WIDTH, s), BF16), out(BF16), out(F32), out(BF16),
                   jax.ShapeDtypeStruct((b, s, LANES), F32)],
        compiler_params=_params("parallel", "parallel"),
        name="gdn_prep",
    )(qkv, qkv, cw, sm, alog_vec, dtb_vec)


def _gdn_scan_kernel(qd_ref, kdt_ref, at_ref, u_ref, w_ref, z_ref, gcum_ref, ng_ref, o_ref, state_ref):
    c = GDN_CHUNK
    n_chunks = qd_ref.shape[1] // c

    @pl.when(pl.program_id(1) == 0)
    def _():
        state_ref[...] = jnp.zeros_like(state_ref)

    def chunk_step(ci, _):
        t0 = pl.multiple_of(ci * c, c)
        rows = pl.ds(t0, c)
        odd = (ci & 1) == 1
        pair = pl.ds(pl.multiple_of((ci >> 1) * (2 * c), 2 * c), 2 * c)
        for bi in range(qd_ref.shape[0]):
            state_decay = jnp.exp(gcum_ref[bi, pl.ds(t0 + c - 1, 1), :])
            for h in range(GDN_HEADS):
                cols = slice(h * GDN_DH, (h + 1) * GDN_DH)
                state = state_ref[bi * GDN_HEADS + h]
                ws_qs = _dot(jnp.concatenate([w_ref[bi, rows, cols], qd_ref[bi, rows, cols]], axis=0),
                             state.astype(BF16))
                v_new = (u_ref[bi, rows, cols] - ws_qs[:c]).astype(BF16)
                zero = jnp.zeros_like(v_new)
                v_pad = jnp.concatenate([jnp.where(odd, zero, v_new), jnp.where(odd, v_new, zero)], axis=0)
                av_kv = _dot(jnp.concatenate([at_ref[bi, rows, cols], kdt_ref[bi, cols, pair]], axis=0), v_pad)
                state_ref[bi * GDN_HEADS + h] = state * _lane_col(state_decay, SM_A0 + h) + av_kv[c:]
                o = ws_qs[c:] + av_kv[:c]
                y = o * lax.rsqrt(jnp.mean(o * o, axis=-1, keepdims=True) + EPS) * ng_ref[...]
                o_ref[bi, rows, cols] = y * jax.nn.silu(z_ref[bi, rows, cols])
        return 0

    lax.fori_loop(0, n_chunks, chunk_step, 0)


def _gdn_scan(qd, kdt, at, u, w, z, gcum, norm_g):
    b, s, _ = qd.shape
    ts = GDN_SEQ_TILE
    nb = GDN_SCAN_BATCH if b % GDN_SCAN_BATCH == 0 else 1
    tok = lambda wd: pl.BlockSpec((nb, ts, wd), lambda i, j: (i, j, 0))
    return pl.pallas_call(
        _gdn_scan_kernel,
        grid=(b // nb, s // ts),
        in_specs=[tok(GDN_WIDTH), pl.BlockSpec((nb, GDN_WIDTH, ts), lambda i, j: (i, 0, j))]
        + [tok(GDN_WIDTH)] * 4 + [tok(LANES), _const_spec((1, GDN_DH))],
        out_specs=tok(GDN_WIDTH),
        out_shape=jax.ShapeDtypeStruct((b, s, GDN_WIDTH), F32),
        scratch_shapes=[pltpu.VMEM((nb * GDN_HEADS, GDN_DH, GDN_DH), F32)],
        compiler_params=_params("parallel", "arbitrary"),
        name="gdn_scan",
    )(qd, kdt, at, u, w, z, gcum, norm_g)


def _outproj_kernel(on_ref, og_ref, x_ref, g_ref, wa_ref, wb_ref, o_ref):
    on = on_ref[...]
    ms = jnp.mean(on * on, axis=-1, keepdims=True)
    hn = (on * lax.rsqrt(ms + EPS) * g_ref[...]).astype(BF16)
    o_ref[...] = x_ref[...] + _dot(hn, wa_ref[...]) + _dot(og_ref[...].astype(BF16), wb_ref[...])


def _outproj(o_nsa, o_gdn, x2, g, wa, wb):
    n, d = x2.shape
    tm = ROW_TILE
    row = lambda w: pl.BlockSpec((tm, w), lambda i: (i, 0))
    return pl.pallas_call(
        _outproj_kernel,
        grid=(n // tm,),
        in_specs=[row(o_nsa.shape[1]), row(o_gdn.shape[1]), row(d),
                  _const_spec(g.shape), _const_spec(wa.shape), _const_spec(wb.shape)],
        out_specs=row(d),
        out_shape=jax.ShapeDtypeStruct((n, d), F32),
        compiler_params=_params("parallel"),
        name="outproj",
    )(o_nsa, o_gdn, x2, g, wa, wb)


def _ffn_kernel(x_ref, halo_ref, g_ref, wup_ref, cw_ref, wdn_ref, gf_ref, o_ref, *, final_norm):
    tm = x_ref.shape[1]
    d_ff = wdn_ref.shape[0]
    width = cw_ref.shape[0]
    x = x_ref[0]
    xin = jnp.concatenate([halo_ref[0], x], axis=0)
    ms = jnp.mean(xin * xin, axis=-1, keepdims=True)
    h = (xin * lax.rsqrt(ms + EPS) * g_ref[...]).astype(BF16)
    row = lax.broadcasted_iota(jnp.int32, (FFN_HALO + tm, 1), 0)
    keep = row >= jnp.where(pl.program_id(1) > 0, 0, FFN_HALO)

    def conv(u, c0):
        u = jnp.where(keep, u, 0.0)
        y = u[FFN_HALO:] * cw_ref[width - 1:width, c0:c0 + FFN_COL_CHUNK]
        for k in range(width - 1):
            shifted = pltpu.roll(u, width - 1 - k, axis=0)[FFN_HALO:]
            y = y + shifted * cw_ref[k:k + 1, c0:c0 + FFN_COL_CHUNK]
        return y

    acc = jnp.zeros((tm, x.shape[1]), F32)
    for j in range(d_ff // FFN_COL_CHUNK):
        c0 = j * FFN_COL_CHUNK
        gate = conv(_dot(h, wup_ref[:, c0:c0 + FFN_COL_CHUNK]), c0)
        up = conv(_dot(h, wup_ref[:, d_ff + c0:d_ff + c0 + FFN_COL_CHUNK]), d_ff + c0)
        act = (jax.nn.silu(gate) * up).astype(BF16)
        acc = acc + _dot(act, wdn_ref[c0:c0 + FFN_COL_CHUNK, :])
    y = x + acc
    if final_norm:
        y = y * lax.rsqrt(jnp.mean(y * y, axis=-1, keepdims=True) + EPS) * gf_ref[...]
    o_ref[0] = y


def _ffn(x3, g, wup, cw, wdn, g_final, final_norm):
    b, s, d = x3.shape
    tm = ROW_TILE
    per = tm // FFN_HALO
    return pl.pallas_call(
        functools.partial(_ffn_kernel, final_norm=final_norm),
        grid=(b, s // tm),
        in_specs=[pl.BlockSpec((1, tm, d), lambda i, j: (i, j, 0)),
                  pl.BlockSpec((1, FFN_HALO, d), lambda i, j: (i, jnp.maximum(j * per - 1, 0), 0)),
                  _const_spec(g.shape), _const_spec(wup.shape), _const_spec(cw.shape), _const_spec(wdn.shape),
                  _const_spec(g_final.shape)],
        out_specs=pl.BlockSpec((1, tm, d), lambda i, j: (i, j, 0)),
        out_shape=jax.ShapeDtypeStruct((b, s, d), F32),
        compiler_params=_params("parallel", "parallel"),
        name="convffn",
    )(x3, x3, g, wup, cw, wdn, g_final)


def _split_in_weights(w_in):
    d = w_in.shape[0]
    o = 0
    wq = w_in[:, o:o + NSA_WIDTH]; o += NSA_WIDTH
    wkv = w_in[:, o:o + 6 * NSA_KV_WIDTH]; o += 6 * NSA_KV_WIDTH
    wgate = w_in[:, o:o + 3 * NSA_HEADS]; o += 3 * NSA_HEADS
    wgdn = w_in[:, o:o + 3 * GDN_WIDTH]; o += 3 * GDN_WIDTH
    wz = w_in[:, o:o + GDN_WIDTH]; o += GDN_WIDTH
    wb = w_in[:, o:o + GDN_HEADS]; o += GDN_HEADS
    wa = w_in[:, o:o + GDN_HEADS]
    wq = wq.reshape(d, NSA_HEADS, NSA_DH) * (NSA_DH ** -0.5)
    zero = jnp.zeros_like(wq)
    kv_head = (jnp.arange(NSA_HEADS) // NSA_GROUP)[None, :, None]
    wq_pad = jnp.concatenate([jnp.where(kv_head == 0, wq, zero), jnp.where(kv_head == 1, wq, zero)], axis=-1)
    wq_pad = wq_pad.reshape(d, NSA_HEADS * LANES)
    wsm = jnp.concatenate([wgate, wb, wa, jnp.zeros((d, LANES - SM_A0 - GDN_HEADS), w_in.dtype)], axis=-1)
    return tuple(w.astype(BF16) for w in (wq_pad, wkv, wgdn, wz, wsm))


def _lane_vec(v, offset):
    return jnp.zeros((1, LANES), F32).at[0, offset:offset + v.shape[0]].set(v.astype(F32))


def _head_padded_w2(w2):
    z = jnp.zeros_like(w2)
    return jnp.stack([jnp.concatenate([w2, z], axis=-1), jnp.concatenate([z, w2], axis=-1)]).astype(BF16)


def kernel(x, norm_mix, w_in, cmp_pos_k, cmp_pos_v, cmp_k_w1, cmp_k_b1, cmp_k_w2, cmp_v_w1, cmp_v_b1, cmp_v_w2,
           nsa_norm, gdn_conv, gdn_a_log, gdn_dt_bias, gdn_norm, w_out, norm_ffn, ffn_up, ffn_conv, ffn_down,
           norm_final):
    b, s, d = x.shape
    depth = w_in.shape[0]
    group = CMP_BLOCK // 2
    assert CMP_STRIDE == group and s % GDN_SEQ_TILE == 0 and s >= WIN_KEYS and (b * s) % ROW_TILE == 0
    x2 = x.reshape(b * s, d)
    for l in range(depth):
        wq, wkv, wgdn, wz, wsm = _split_in_weights(w_in[l])
        q, kv, gdn, z, sm = _inproj(x2, norm_mix[l][None, :], wq, wkv, wgdn, wz, wsm)
        q, kv, gdn, z, sm = (a.reshape(b, s, a.shape[-1]) for a in (q, kv, gdn, z, sm))

        groups = kv[:, :, :2 * NSA_KV_WIDTH].reshape(b, s, 2 * NSA_KV_HEADS, NSA_DH)
        groups = groups.transpose(0, 2, 1, 3).reshape(b, 2 * NSA_KV_HEADS, s // group, group * NSA_DH)
        kc, vc = _compress(
            groups, cmp_pos_k[l].reshape(1, -1), cmp_pos_v[l].reshape(1, -1),
            cmp_k_w1[l].astype(BF16), cmp_k_b1[l][None, :], _head_padded_w2(cmp_k_w2[l]),
            cmp_v_w1[l].astype(BF16), cmp_v_b1[l][None, :], _head_padded_w2(cmp_v_w2[l]))
        o_nsa = _nsa_attention(q, sm, kc, vc, kv)

        alog_vec, dtb_vec = _lane_vec(gdn_a_log[l], SM_A0), _lane_vec(gdn_dt_bias[l], SM_A0)
        qd, kdt, at, gu, gw, gcum = _gdn_prep(gdn, gdn_conv[l], sm, alog_vec, dtb_vec)
        o_gdn = _gdn_scan(qd, kdt, at, gu, gw, z, gcum, gdn_norm[l][None, :])

        w_o = w_out[l].astype(BF16)
        x2 = _outproj(o_nsa.reshape(b * s, -1), o_gdn.reshape(b * s, -1), x2, nsa_norm[l][None, :],
                      w_o[:NSA_WIDTH], w_o[NSA_WIDTH:])
        x2 = _ffn(x2.reshape(b, s, d), norm_ffn[l][None, :], ffn_up[l].astype(BF16), ffn_conv[l],
                  ffn_down[l].astype(BF16), norm_final[None, :], final_norm=(l == depth - 1)).reshape(b * s, d)
    return x2.reshape(b, s, d)
```

```python
import functools

import jax
import jax.numpy as jnp
from jax import lax
from jax.experimental import pallas as pl
from jax.experimental.pallas import tpu as pltpu

F32 = jnp.float32
BF16 = jnp.bfloat16
EPS = 1e-6
NEG_INF = float("-inf")
MASK_BIAS = -1e30

NSA_HEADS = 8
NSA_KV_HEADS = 2
NSA_GROUP = NSA_HEADS // NSA_KV_HEADS
NSA_DH = 64
NSA_WIDTH = NSA_HEADS * NSA_DH
NSA_KV_WIDTH = NSA_KV_HEADS * NSA_DH
CMP_BLOCK = 32
CMP_STRIDE = 16
SEL_BLOCK = 64
N_SELECT = 8
WINDOW = 512
GDN_HEADS = 4
GDN_DH = 128
GDN_WIDTH = GDN_HEADS * GDN_DH
GDN_CHUNK = 64
GDN_CHUNK_SHIFT = GDN_CHUNK.bit_length() - 1

LANES = 128
VMEM_LIMIT_BYTES = 56 * 1024 * 1024

ROW_TILE = 512
FFN_HALO = 8
FFN_COL_CHUNK = 256
SEL_KEY_CHUNK = 512
WIN_KEYS = WINDOW + 2 * SEL_BLOCK
GDN_SEQ_TILE = 256
GDN_SCAN_BATCH = 4
GDN_PREP_TILE = 256
GDN_HALO = 8

SM_GATE0 = 0
SM_BETA0 = 3 * NSA_HEADS
SM_A0 = SM_BETA0 + GDN_HEADS


def _dot(a, b, precision=None):
    return jnp.dot(a, b, preferred_element_type=F32, precision=precision)


def _dot_nt(a, b, precision=None):
    return lax.dot_general(a, b, (((1,), (1,)), ((), ())), preferred_element_type=F32, precision=precision)


def _dot_tn(a, b, precision=None):
    return lax.dot_general(a, b, (((0,), (0,)), ((), ())), preferred_element_type=F32, precision=precision)


def _lane_col(x, idx):
    lane = lax.broadcasted_iota(jnp.int32, x.shape, 1)
    return jnp.sum(jnp.where(lane == idx, x, 0.0), axis=-1, keepdims=True)


def _const_spec(shape):
    zeros = (0,) * len(shape)
    return pl.BlockSpec(shape, lambda *_: zeros, pipeline_mode=pl.Buffered(1))


def _params(*semantics):
    return pltpu.CompilerParams(dimension_semantics=semantics, vmem_limit_bytes=VMEM_LIMIT_BYTES)


def _inproj_kernel(x_ref, g_ref, wq_ref, wkv_ref, wgdn_ref, wz_ref, wsm_ref,
                   q_ref, kv_ref, gdn_ref, z_ref, sm_ref):
    x = x_ref[...]
    ms = jnp.mean(x * x, axis=-1, keepdims=True)
    h = (x * lax.rsqrt(ms + EPS) * g_ref[...]).astype(BF16)
    q_ref[...] = _dot(h, wq_ref[...]).astype(BF16)
    kv_ref[...] = _dot(h, wkv_ref[...]).astype(BF16)
    gdn_ref[...] = _dot(h, wgdn_ref[...])
    z_ref[...] = _dot(h, wz_ref[...])
    sm_ref[...] = _dot(h, wsm_ref[...])


def _inproj(x2, g, wq, wkv, wgdn, wz, wsm):
    n, d = x2.shape
    tm = ROW_TILE
    row = lambda w: pl.BlockSpec((tm, w), lambda i: (i, 0))
    widths = (wq.shape[1], wkv.shape[1], wgdn.shape[1], wz.shape[1], wsm.shape[1])
    dtypes = (BF16, BF16, F32, F32, F32)
    return pl.pallas_call(
        _inproj_kernel,
        grid=(n // tm,),
        in_specs=[row(d), _const_spec((1, d))] + [_const_spec(w.shape) for w in (wq, wkv, wgdn, wz, wsm)],
        out_specs=[row(w) for w in widths],
        out_shape=[jax.ShapeDtypeStruct((n, w), dt) for w, dt in zip(widths, dtypes)],
        compiler_params=_params("parallel"),
        name="inproj",
    )(x2, g, wq, wkv, wgdn, wz, wsm)


def _compress_kernel(g_ref, posk_ref, posv_ref, w1k_ref, b1k_ref, w2k_ref, w1v_ref, b1v_ref, w2v_ref,
                     kc_ref, vc_ref):
    ng = g_ref.shape[2]
    half = w1k_ref.shape[0] // 2
    row = lax.broadcasted_iota(jnp.int32, (ng, 1), 0)
    for sel, (pos_ref, w1_ref, b1_ref, w2_ref, out_ref) in enumerate(
            ((posk_ref, w1k_ref, b1k_ref, w2k_ref, kc_ref), (posv_ref, w1v_ref, b1v_ref, w2v_ref, vc_ref))):
        w1 = w1_ref[...]
        pos8 = jnp.broadcast_to(pos_ref[...], (8, pos_ref.shape[1])).astype(BF16)
        bias = _dot(pos8, w1)[0:1, :] + b1_ref[...]
        acc = jnp.zeros((ng, LANES), F32)
        for h in range(NSA_KV_HEADS):
            grp = g_ref[0, sel * NSA_KV_HEADS + h]
            top = _dot(grp, w1[:half])
            bot = _dot(grp, w1[half:])
            hid = top + pltpu.roll(bot, ng - 1, axis=0) + bias
            act = jax.nn.gelu(hid, approximate=True).astype(BF16)
            acc = acc + _dot(act, w2_ref[h])
        out_ref[0] = jnp.where(row < ng - 1, acc, 0.0).astype(BF16)


def _compress(groups, posk, posv, w1k, b1k, w2k, w1v, b1v, w2v):
    b, _, ng, gw = groups.shape
    out = jax.ShapeDtypeStruct((b, ng, LANES), BF16)
    return pl.pallas_call(
        _compress_kernel,
        grid=(b,),
        in_specs=[pl.BlockSpec((1, 2 * NSA_KV_HEADS, ng, gw), lambda i: (i, 0, 0, 0))]
        + [_const_spec(a.shape) for a in (posk, posv, w1k, b1k, w2k, w1v, b1v, w2v)],
        out_specs=[pl.BlockSpec((1, ng, LANES), lambda i: (i, 0, 0))] * 2,
        out_shape=[out, out],
        compiler_params=_params("parallel"),
        name="nsa_compress",
    )(groups, posk, posv, w1k, b1k, w2k, w1v, b1v, w2v)


def _softmax_parts(s, mask):
    sm = jnp.where(mask, s, NEG_INF)
    m = jnp.max(sm, axis=-1, keepdims=True)
    m = jnp.where(jnp.isfinite(m), m, 0.0)
    e = jnp.where(mask, jnp.exp(sm - m), 0.0)
    return e, jnp.sum(e, axis=-1, keepdims=True)


def _nsa_kernel(q_ref, sm_ref, kc_ref, vc_ref, ks_ref, vs_ref, kw_ref, vw_ref, oh_ref, o_ref,
                m_ref, l_ref, acc_ref):
    qb = q_ref.shape[1]
    seq = ks_ref.shape[1]
    ncmp = kc_ref.shape[1]
    nsb = seq // SEL_BLOCK
    rows = NSA_HEADS * qb
    both = NSA_KV_HEADS * qb
    blk = pl.program_id(1)
    s0 = blk * qb

    gates = jax.nn.sigmoid(sm_ref[0])
    q_all = q_ref[0]
    t_rows = s0 + (lax.broadcasted_iota(jnp.int32, (rows, 1), 0) & (qb - 1))
    lane = lax.broadcasted_iota(jnp.int32, (qb, LANES), 1)

    jn = lax.broadcasted_iota(jnp.int32, (nsb, ncmp), 0)
    nn = lax.broadcasted_iota(jnp.int32, (nsb, ncmp), 1)
    overlap_t = jnp.where((nn * CMP_STRIDE < jn * SEL_BLOCK + SEL_BLOCK)
                          & (jn * SEL_BLOCK < nn * CMP_STRIDE + CMP_BLOCK) & (nn < ncmp - 1), 1.0, 0.0)
    n_cmp_idx = lax.broadcasted_iota(jnp.int32, (1, ncmp), 1)
    cmp_mask = (n_cmp_idx * CMP_STRIDE + (CMP_BLOCK - 1) <= t_rows) & (n_cmp_idx < ncmp - 1)
    qs = jnp.concatenate([q_all[:, hq * LANES:(hq + 1) * LANES] for hq in range(NSA_HEADS)], axis=0)

    d_lo = pl.multiple_of(jnp.maximum(s0 - SEL_BLOCK, 0), SEL_BLOCK)
    diag_mask = d_lo + lax.broadcasted_iota(jnp.int32, (1, 2 * SEL_BLOCK), 1) <= t_rows
    w_lo = jnp.minimum(jnp.maximum(s0 - WINDOW, 0), seq - WIN_KEYS)
    w_lo = pl.multiple_of(w_lo, SEL_BLOCK)
    kpos_w = w_lo + lax.broadcasted_iota(jnp.int32, (1, WIN_KEYS), 1)
    win_bias = jnp.where((kpos_w <= t_rows) & (kpos_w > t_rows - WINDOW), 0.0, MASK_BIAS)

    s_c = _dot_nt(qs, kc_ref[0])
    s_w = _dot_nt(qs, kw_ref[0, pl.ds(w_lo, WIN_KEYS), :]) + win_bias
    s_d = jnp.where(diag_mask, _dot_nt(qs, ks_ref[0, pl.ds(d_lo, 2 * SEL_BLOCK), :]), NEG_INF)
    e, l = _softmax_parts(s_c, cmp_mask)
    p_cmp = e / jnp.maximum(l, 1e-30)
    e_w = jnp.exp(s_w - jnp.max(s_w, axis=-1, keepdims=True))
    m0 = jnp.max(s_d, axis=-1, keepdims=True)
    p0 = jnp.exp(s_d - m0)
    l0 = jnp.sum(p0, axis=-1, keepdims=True)
    o_cmp = _dot(p_cmp.astype(BF16), vc_ref[0])
    o_win = (_dot(e_w.astype(BF16), vw_ref[0, pl.ds(w_lo, WIN_KEYS), :])
             / jnp.maximum(jnp.sum(e_w, axis=-1, keepdims=True), 1e-30))
    acc0 = _dot(p0.astype(BF16), vs_ref[0, pl.ds(d_lo, 2 * SEL_BLOCK), :])
    p_sum = []
    for h in range(NSA_KV_HEADS):
        acc = p_cmp[h * NSA_GROUP * qb:(h * NSA_GROUP + 1) * qb]
        for g in range(1, NSA_GROUP):
            acc = acc + p_cmp[(h * NSA_GROUP + g) * qb:(h * NSA_GROUP + g + 1) * qb]
        p_sum.append(acc)
    imp_t = _dot_nt(overlap_t, jnp.concatenate(p_sum, axis=0), precision=lax.Precision.HIGHEST)

    jj = lax.broadcasted_iota(jnp.int32, (nsb, both), 0)
    forced = (jj == 0) | (jj == blk) | (jj == blk - 1)
    score = jnp.where(forced, jnp.inf, jnp.where(jj <= blk, imp_t, NEG_INF))
    rank = jnp.zeros((nsb, both), F32)
    for j2 in range(nsb):
        other = score[j2:j2 + 1, :]
        ahead = (other > score) | ((other == score) & (jj > j2))
        rank = rank + jnp.where(ahead, 1.0, 0.0)
    chosen = (rank < N_SELECT) & (score > NEG_INF) & (jj < blk - 1)
    bias_t = jnp.concatenate([jnp.where(chosen, 0.0, MASK_BIAS), jnp.zeros((LANES - nsb, both), F32)], axis=0)
    eye = jnp.where(lax.broadcasted_iota(jnp.int32, (both, both), 0)
                    == lax.broadcasted_iota(jnp.int32, (both, both), 1), 1.0, 0.0).astype(BF16)
    bias_q = _dot_nt(eye, bias_t.astype(BF16)).astype(BF16)

    n_chunks = (blk + SEL_KEY_CHUNK // SEL_BLOCK - 2) // (SEL_KEY_CHUNK // SEL_BLOCK)
    bias_rows = jnp.concatenate([bias_q[h * qb:(h + 1) * qb] for h in range(NSA_KV_HEADS)
                                 for _ in range(NSA_GROUP)], axis=0)
    q_aug = jnp.concatenate([qs, bias_rows], axis=1)
    m_ref[...] = jnp.broadcast_to(m0, m_ref.shape)
    l_ref[...] = jnp.broadcast_to(l0, l_ref.shape)
    acc_ref[...] = acc0

    def sel_step(c, _):
        k0 = pl.multiple_of(c * SEL_KEY_CHUNK, SEL_KEY_CHUNK)
        k_aug = jnp.concatenate([ks_ref[0, pl.ds(k0, SEL_KEY_CHUNK), :], oh_ref[pl.ds(k0, SEL_KEY_CHUNK), :]],
                                axis=1)
        v_blk = vs_ref[0, pl.ds(k0, SEL_KEY_CHUNK), :]
        half = rows // 2
        s = [_dot_nt(q_aug[i * half:(i + 1) * half], k_aug) for i in range(2)]
        for i in range(2):
            rs = slice(i * half, (i + 1) * half)
            m_prev = m_ref[rs]
            m_new = jnp.maximum(m_prev, jnp.max(s[i], axis=-1, keepdims=True))
            p = jnp.exp(s[i] - jnp.concatenate([m_new] * (SEL_KEY_CHUNK // LANES), axis=1))
            alpha = jnp.exp(m_prev - m_new)
            m_ref[rs] = m_new
            l_ref[rs] = alpha * l_ref[rs] + jnp.sum(p, axis=-1, keepdims=True)
            acc_ref[rs] = alpha * acc_ref[rs] + _dot(p.astype(BF16), v_blk)
        return 0

    lax.fori_loop(0, n_chunks, sel_step, 0)
    o_sel = acc_ref[...] / jnp.maximum(l_ref[...], 1e-30)

    mixed = []
    for hq in range(NSA_HEADS):
        r = slice(hq * qb, (hq + 1) * qb)
        mixed.append(_lane_col(gates, SM_GATE0 + 3 * hq) * o_cmp[r]
                     + _lane_col(gates, SM_GATE0 + 3 * hq + 1) * o_sel[r]
                     + _lane_col(gates, SM_GATE0 + 3 * hq + 2) * o_win[r])
    for pair in range(NSA_HEADS // 2):
        a, b = mixed[2 * pair], mixed[2 * pair + 1]
        if 2 * pair // NSA_GROUP == 0:
            b = pltpu.roll(b, NSA_DH, axis=1)
        else:
            a = pltpu.roll(a, NSA_DH, axis=1)
        o_ref[0, :, pair * LANES:(pair + 1) * LANES] = jnp.where(lane < NSA_DH, a, b)


def _nsa_attention(q, sm, kc, vc, kv):
    b, s, qw = q.shape
    qb = SEL_BLOCK
    ncmp = kc.shape[1]
    onehot = (jnp.arange(s)[:, None] // SEL_BLOCK == jnp.arange(LANES)[None, :]).astype(BF16)
    kv_spec = lambda col: pl.BlockSpec((1, s, LANES), lambda i, j: (i, 0, col))
    return pl.pallas_call(
        _nsa_kernel,
        grid=(b, s // qb),
        in_specs=[pl.BlockSpec((1, qb, qw), lambda i, j: (i, j, 0)),
                  pl.BlockSpec((1, qb, LANES), lambda i, j: (i, j, 0)),
                  pl.BlockSpec((1, ncmp, LANES), lambda i, j: (i, 0, 0)),
                  pl.BlockSpec((1, ncmp, LANES), lambda i, j: (i, 0, 0)),
                  kv_spec(2), kv_spec(3), kv_spec(4), kv_spec(5), _const_spec((s, LANES))],
        out_specs=pl.BlockSpec((1, qb, NSA_WIDTH), lambda i, j: (i, j, 0)),
        out_shape=jax.ShapeDtypeStruct((b, s, NSA_WIDTH), F32),
        scratch_shapes=[pltpu.VMEM((NSA_HEADS * qb, LANES), F32)] * 3,
        compiler_params=_params("parallel", "arbitrary"),
        name="nsa_attention",
    )(q, sm, kc, vc, kv, kv, kv, kv, onehot)


def _unit_lower_inverses(mats, r, c):
    n = mats[0].shape[0]
    eye = jnp.where(r == c, 1.0, 0.0)
    blk16 = (r >> 4) == (c >> 4)
    each = lambda f, *lists: [f(*args) for args in zip(*lists)]
    mm = lambda x, y: _dot(x.astype(BF16), y.astype(BF16))
    stacked = lambda top, rhs: mm(jnp.concatenate([top, rhs], axis=0), rhs)
    n1 = each(lambda a: jnp.where(blk16, -a, 0.0), mats)
    n2 = each(mm, n1, n1)
    t = each(lambda m: eye + m, n1)
    x = each(stacked, t, n2)
    t, n4 = each(lambda ti, xi: ti + xi[:n], t, x), each(lambda xi: xi[n:], x)
    x = each(stacked, t, n4)
    t, n8 = each(lambda ti, xi: ti + xi[:n], t, x), each(lambda xi: xi[n:], x)
    t = each(lambda ti, m: ti + mm(ti, m), t, n8)
    v = each(lambda a, ti: mm(jnp.where(blk16, 0.0, a), ti), mats, t)
    x = each(stacked, t, v)
    p, v2 = each(lambda ti, xi: ti - xi[:n], t, x), each(lambda xi: xi[n:], x)
    return each(lambda pi, m: pi + mm(pi, m), p, v2)


def _log_decay(sm, alog_ref, dtb_ref):
    return -jnp.exp(alog_ref[...]) * jax.nn.softplus(sm + dtb_ref[...])


def _gdn_prep_kernel(x_ref, halo_ref, cw_ref, sm_ref, alog_ref, dtb_ref,
                     qd_ref, kdt_ref, at_ref, u_ref, w_ref, gcum_ref):
    n = x_ref.shape[1]
    width = cw_ref.shape[0]
    has_history = jnp.where(pl.program_id(1) > 0, 1.0, 0.0)

    def conv_silu(c0):
        cols = slice(c0, c0 + GDN_DH)
        xe = jnp.concatenate([halo_ref[0, :, cols] * has_history, x_ref[0, :, cols]], axis=0)
        y = xe[GDN_HALO:] * cw_ref[width - 1:width, cols]
        for k in range(width - 1):
            y = y + pltpu.roll(xe, width - 1 - k, axis=0)[GDN_HALO:] * cw_ref[k:k + 1, cols]
        return jax.nn.silu(y)

    def l2norm(y):
        return y * lax.rsqrt(jnp.sum(y * y, axis=-1, keepdims=True) + EPS)

    r = lax.broadcasted_iota(jnp.int32, (n, n), 0)
    c = lax.broadcasted_iota(jnp.int32, (n, n), 1)
    in_chunk = (r >> GDN_CHUNK_SHIFT) == (c >> GDN_CHUNK_SHIFT)
    causal = in_chunk & (r >= c)
    hi = lax.Precision.HIGHEST
    sm = sm_ref[0]
    beta = jax.nn.sigmoid(sm)
    g = _log_decay(sm, alog_ref, dtb_ref)
    gcum = _dot(jnp.where(causal, 1.0, 0.0), g, precision=hi)
    gtot = _dot(jnp.where(in_chunk, 1.0, 0.0), g, precision=hi)
    gcum_ref[0] = gcum
    pick = jnp.where(lax.broadcasted_iota(jnp.int32, (8, LANES), 1)
                     == lax.broadcasted_iota(jnp.int32, (8, LANES), 0) + SM_A0, 1.0, 0.0)
    gcum_rows = _dot_nt(pick, gcum, precision=hi)

    a_mats, rhs = [], []
    for h in range(GDN_HEADS):
        cols = slice(h * GDN_DH, (h + 1) * GDN_DH)
        q = l2norm(conv_silu(h * GDN_DH)) * (GDN_DH ** -0.5)
        k = l2norm(conv_silu(GDN_WIDTH + h * GDN_DH))
        v = conv_silu(2 * GDN_WIDTH + h * GDN_DH)
        gc = _lane_col(gcum, SM_A0 + h)
        bcol = _lane_col(beta, SM_BETA0 + h)
        decay = jnp.exp(jnp.where(causal, gc - gcum_rows[h:h + 1, :], NEG_INF))
        kb = k * bcol
        egc = jnp.exp(gc)
        qd_ref[0, :, cols] = (q * egc).astype(BF16)
        kdt_ref[0, cols, :] = (k * jnp.exp(_lane_col(gtot, SM_A0 + h) - gc)).T.astype(BF16)
        grams = _dot_nt(jnp.concatenate([kb, q], axis=0).astype(BF16), k.astype(BF16))
        attn = (grams[n:] * decay).astype(BF16)
        for ch in range(n // GDN_CHUNK):
            lt = ch * GDN_CHUNK // LANES
            at_ref[0, ch * GDN_CHUNK:(ch + 1) * GDN_CHUNK, cols] = (
                attn[ch * GDN_CHUNK:(ch + 1) * GDN_CHUNK, lt * LANES:(lt + 1) * LANES])
        a_mats.append(jnp.where(r > c, grams[:n] * decay, 0.0))
        rhs.append(jnp.concatenate([v * bcol, kb * egc], axis=1).astype(BF16))
    for h, t_inv in enumerate(_unit_lower_inverses(a_mats, r, c)):
        cols = slice(h * GDN_DH, (h + 1) * GDN_DH)
        uw = _dot(t_inv.astype(BF16), rhs[h])
        u_ref[0, :, cols] = uw[:, :GDN_DH]
        w_ref[0, :, cols] = uw[:, GDN_DH:].astype(BF16)


def _gdn_prep(qkv, cw, sm, alog_vec, dtb_vec):
    b, s, width = qkv.shape
    n = GDN_PREP_TILE
    per = n // GDN_HALO
    tok = lambda w: pl.BlockSpec((1, n, w), lambda i, j: (i, j, 0))
    out = lambda dt: jax.ShapeDtypeStruct((b, s, GDN_WIDTH), dt)
    return pl.pallas_call(
        _gdn_prep_kernel,
        grid=(b, s // n),
        in_specs=[tok(width),
                  pl.BlockSpec((1, GDN_HALO, width), lambda i, j: (i, jnp.maximum(j * per - 1, 0), 0)),
                  _const_spec(cw.shape), tok(LANES), _const_spec((1, LANES)), _const_spec((1, LANES))],
        out_specs=[tok(GDN_WIDTH), pl.BlockSpec((1, GDN_WIDTH, n), lambda i, j: (i, 0, j))]
        + [tok(GDN_WIDTH)] * 3 + [tok(LANES)],
        out_shape=[out(BF16), jax.ShapeDtypeStruct((b, GDN_WIDTH, s), BF16), out(BF16), out(F32), out(BF16),
                   jax.ShapeDtypeStruct((b, s, LANES), F32)],
        compiler_params=_params("parallel", "parallel"),
        name="gdn_prep",
    )(qkv, qkv, cw, sm, alog_vec, dtb_vec)


def _gdn_scan_kernel(qd_ref, kdt_ref, at_ref, u_ref, w_ref, z_ref, gcum_ref, ng_ref, o_ref, state_ref):
    c = GDN_CHUNK
    n_chunks = qd_ref.shape[1] // c

    @pl.when(pl.program_id(1) == 0)
    def _():
        state_ref[...] = jnp.zeros_like(state_ref)

    def chunk_step(ci, _):
        t0 = pl.multiple_of(ci * c, c)
        rows = pl.ds(t0, c)
        odd = (ci & 1) == 1
        pair = pl.ds(pl.multiple_of((ci >> 1) * (2 * c), 2 * c), 2 * c)
        chains = [(bi, h) for bi in range(qd_ref.shape[0]) for h in range(GDN_HEADS)]
        cols = lambda h: slice(h * GDN_DH, (h + 1) * GDN_DH)
        states = [state_ref[bi * GDN_HEADS + h] for bi, h in chains]
        ws_qs = [_dot(jnp.concatenate([w_ref[bi, rows, cols(h)], qd_ref[bi, rows, cols(h)]], axis=0),
                      s.astype(BF16)) for (bi, h), s in zip(chains, states)]
        av_kv = []
        for (bi, h), x in zip(chains, ws_qs):
            v_new = (u_ref[bi, rows, cols(h)] - x[:c]).astype(BF16)
            zero = jnp.zeros_like(v_new)
            v_pad = jnp.concatenate([jnp.where(odd, zero, v_new), jnp.where(odd, v_new, zero)], axis=0)
            av_kv.append(_dot(jnp.concatenate([at_ref[bi, rows, cols(h)], kdt_ref[bi, cols(h), pair]], axis=0),
                              v_pad))
        for (bi, h), s, x, y in zip(chains, states, ws_qs, av_kv):
            state_decay = jnp.exp(gcum_ref[bi, pl.ds(t0 + c - 1, 1), :])
            state_ref[bi * GDN_HEADS + h] = s * _lane_col(state_decay, SM_A0 + h) + y[c:]
            o = x[c:] + y[:c]
            o = o * lax.rsqrt(jnp.mean(o * o, axis=-1, keepdims=True) + EPS) * ng_ref[...]
            o_ref[bi, rows, cols(h)] = o * jax.nn.silu(z_ref[bi, rows, cols(h)])
        return 0

    lax.fori_loop(0, n_chunks, chunk_step, 0)


def _gdn_scan(qd, kdt, at, u, w, z, gcum, norm_g):
    b, s, _ = qd.shape
    ts = GDN_SEQ_TILE
    nb = GDN_SCAN_BATCH if b % GDN_SCAN_BATCH == 0 else 1
    tok = lambda wd: pl.BlockSpec((nb, ts, wd), lambda i, j: (i, j, 0))
    return pl.pallas_call(
        _gdn_scan_kernel,
        grid=(b // nb, s // ts),
        in_specs=[tok(GDN_WIDTH), pl.BlockSpec((nb, GDN_WIDTH, ts), lambda i, j: (i, 0, j))]
        + [tok(GDN_WIDTH)] * 4 + [tok(LANES), _const_spec((1, GDN_DH))],
        out_specs=tok(GDN_WIDTH),
        out_shape=jax.ShapeDtypeStruct((b, s, GDN_WIDTH), F32),
        scratch_shapes=[pltpu.VMEM((nb * GDN_HEADS, GDN_DH, GDN_DH), F32)],
        compiler_params=_params("parallel", "arbitrary"),
        name="gdn_scan",
    )(qd, kdt, at, u, w, z, gcum, norm_g)


def _outproj_kernel(on_ref, og_ref, x_ref, g_ref, wa_ref, wb_ref, o_ref):
    on = on_ref[...]
    ms = jnp.mean(on * on, axis=-1, keepdims=True)
    hn = (on * lax.rsqrt(ms + EPS) * g_ref[...]).astype(BF16)
    o_ref[...] = x_ref[...] + _dot(hn, wa_ref[...]) + _dot(og_ref[...].astype(BF16), wb_ref[...])


def _outproj(o_nsa, o_gdn, x2, g, wa, wb):
    n, d = x2.shape
    tm = ROW_TILE
    row = lambda w: pl.BlockSpec((tm, w), lambda i: (i, 0))
    return pl.pallas_call(
        _outproj_kernel,
        grid=(n // tm,),
        in_specs=[row(o_nsa.shape[1]), row(o_gdn.shape[1]), row(d),
                  _const_spec(g.shape), _const_spec(wa.shape), _const_spec(wb.shape)],
        out_specs=row(d),
        out_shape=jax.ShapeDtypeStruct((n, d), F32),
        compiler_params=_params("parallel"),
        name="outproj",
    )(o_nsa, o_gdn, x2, g, wa, wb)


def _ffn_kernel(x_ref, halo_ref, g_ref, wup_ref, cw_ref, wdn_ref, gf_ref, o_ref, *, final_norm):
    tm = x_ref.shape[1]
    d_ff = wdn_ref.shape[0]
    width = cw_ref.shape[0]
    x = x_ref[0]
    xin = jnp.concatenate([halo_ref[0], x], axis=0)
    ms = jnp.mean(xin * xin, axis=-1, keepdims=True)
    h = (xin * lax.rsqrt(ms + EPS) * g_ref[...]).astype(BF16)
    row = lax.broadcasted_iota(jnp.int32, (FFN_HALO + tm, 1), 0)
    keep = row >= jnp.where(pl.program_id(1) > 0, 0, FFN_HALO)

    def conv(u, c0):
        u = jnp.where(keep, u, 0.0)
        y = u[FFN_HALO:] * cw_ref[width - 1:width, c0:c0 + FFN_COL_CHUNK]
        for k in range(width - 1):
            shifted = pltpu.roll(u, width - 1 - k, axis=0)[FFN_HALO:]
            y = y + shifted * cw_ref[k:k + 1, c0:c0 + FFN_COL_CHUNK]
        return y

    acc = jnp.zeros((tm, x.shape[1]), F32)
    for j in range(d_ff // FFN_COL_CHUNK):
        c0 = j * FFN_COL_CHUNK
        gate = conv(_dot(h, wup_ref[:, c0:c0 + FFN_COL_CHUNK]), c0)
        up = conv(_dot(h, wup_ref[:, d_ff + c0:d_ff + c0 + FFN_COL_CHUNK]), d_ff + c0)
        act = (jax.nn.silu(gate) * up).astype(BF16)
        acc = acc + _dot(act, wdn_ref[c0:c0 + FFN_COL_CHUNK, :])
    y = x + acc
    if final_norm:
        y = y * lax.rsqrt(jnp.mean(y * y, axis=-1, keepdims=True) + EPS) * gf_ref[...]
    o_ref[0] = y


def _ffn(x3, g, wup, cw, wdn, g_final, final_norm):
    b, s, d = x3.shape
    tm = ROW_TILE
    per = tm // FFN_HALO
    return pl.pallas_call(
        functools.partial(_ffn_kernel, final_norm=final_norm),
        grid=(b, s // tm),
        in_specs=[pl.BlockSpec((1, tm, d), lambda i, j: (i, j, 0)),
                  pl.BlockSpec((1, FFN_HALO, d), lambda i, j: (i, jnp.maximum(j * per - 1, 0), 0)),
                  _const_spec(g.shape), _const_spec(wup.shape), _const_spec(cw.shape), _const_spec(wdn.shape),
                  _const_spec(g_final.shape)],
        out_specs=pl.BlockSpec((1, tm, d), lambda i, j: (i, j, 0)),
        out_shape=jax.ShapeDtypeStruct((b, s, d), F32),
        compiler_params=_params("parallel", "parallel"),
        name="convffn",
    )(x3, x3, g, wup, cw, wdn, g_final)


def _split_in_weights(w_in):
    d = w_in.shape[0]
    o = 0
    wq = w_in[:, o:o + NSA_WIDTH]; o += NSA_WIDTH
    wkv = w_in[:, o:o + 6 * NSA_KV_WIDTH]; o += 6 * NSA_KV_WIDTH
    wgate = w_in[:, o:o + 3 * NSA_HEADS]; o += 3 * NSA_HEADS
    wgdn = w_in[:, o:o + 3 * GDN_WIDTH]; o += 3 * GDN_WIDTH
    wz = w_in[:, o:o + GDN_WIDTH]; o += GDN_WIDTH
    wb = w_in[:, o:o + GDN_HEADS]; o += GDN_HEADS
    wa = w_in[:, o:o + GDN_HEADS]
    wq = wq.reshape(d, NSA_HEADS, NSA_DH) * (NSA_DH ** -0.5)
    zero = jnp.zeros_like(wq)
    kv_head = (jnp.arange(NSA_HEADS) // NSA_GROUP)[None, :, None]
    wq_pad = jnp.concatenate([jnp.where(kv_head == 0, wq, zero), jnp.where(kv_head == 1, wq, zero)], axis=-1)
    wq_pad = wq_pad.reshape(d, NSA_HEADS * LANES)
    wsm = jnp.concatenate([wgate, wb, wa, jnp.zeros((d, LANES - SM_A0 - GDN_HEADS), w_in.dtype)], axis=-1)
    return tuple(w.astype(BF16) for w in (wq_pad, wkv, wgdn, wz, wsm))


def _lane_vec(v, offset):
    return jnp.zeros((1, LANES), F32).at[0, offset:offset + v.shape[0]].set(v.astype(F32))


def _head_padded_w2(w2):
    z = jnp.zeros_like(w2)
    return jnp.stack([jnp.concatenate([w2, z], axis=-1), jnp.concatenate([z, w2], axis=-1)]).astype(BF16)


def kernel(x, norm_mix, w_in, cmp_pos_k, cmp_pos_v, cmp_k_w1, cmp_k_b1, cmp_k_w2, cmp_v_w1, cmp_v_b1, cmp_v_w2,
           nsa_norm, gdn_conv, gdn_a_log, gdn_dt_bias, gdn_norm, w_out, norm_ffn, ffn_up, ffn_conv, ffn_down,
           norm_final):
    b, s, d = x.shape
    depth = w_in.shape[0]
    group = CMP_BLOCK // 2
    assert CMP_STRIDE == group and s % GDN_SEQ_TILE == 0 and s >= WIN_KEYS and (b * s) % ROW_TILE == 0
    x2 = x.reshape(b * s, d)
    for l in range(depth):
        wq, wkv, wgdn, wz, wsm = _split_in_weights(w_in[l])
        q, kv, gdn, z, sm = _inproj(x2, norm_mix[l][None, :], wq, wkv, wgdn, wz, wsm)
        q, kv, gdn, z, sm = (a.reshape(b, s, a.shape[-1]) for a in (q, kv, gdn, z, sm))

        groups = kv[:, :, :2 * NSA_KV_WIDTH].reshape(b, s, 2 * NSA_KV_HEADS, NSA_DH)
        groups = groups.transpose(0, 2, 1, 3).reshape(b, 2 * NSA_KV_HEADS, s // group, group * NSA_DH)
        kc, vc = _compress(
            groups, cmp_pos_k[l].reshape(1, -1), cmp_pos_v[l].reshape(1, -1),
            cmp_k_w1[l].astype(BF16), cmp_k_b1[l][None, :], _head_padded_w2(cmp_k_w2[l]),
            cmp_v_w1[l].astype(BF16), cmp_v_b1[l][None, :], _head_padded_w2(cmp_v_w2[l]))
        o_nsa = _nsa_attention(q, sm, kc, vc, kv)

        alog_vec, dtb_vec = _lane_vec(gdn_a_log[l], SM_A0), _lane_vec(gdn_dt_bias[l], SM_A0)
        qd, kdt, at, gu, gw, gcum = _gdn_prep(gdn, gdn_conv[l], sm, alog_vec, dtb_vec)
        o_gdn = _gdn_scan(qd, kdt, at, gu, gw, z, gcum, gdn_norm[l][None, :])

        w_o = w_out[l].astype(BF16)
        x2 = _outproj(o_nsa.reshape(b * s, -1), o_gdn.reshape(b * s, -1), x2, nsa_norm[l][None, :],
                      w_o[:NSA_WIDTH], w_o[NSA_WIDTH:])
        x2 = _ffn(x2.reshape(b, s, d), norm_ffn[l][None, :], ffn_up[l].astype(BF16), ffn_conv[l],
                  ffn_down[l].astype(BF16), norm_final[None, :], final_norm=(l == depth - 1)).reshape(b * s, d)
    return x2.reshape(b, s, d)
```

```python
import functools

import jax
import jax.numpy as jnp
from jax import lax
from jax.experimental import pallas as pl
from jax.experimental.pallas import tpu as pltpu

F32 = jnp.float32
BF16 = jnp.bfloat16
EPS = 1e-6
NEG_INF = float("-inf")
MASK_BIAS = -1e30
LOG2_E = 1.4426950408889634

NSA_HEADS = 8
NSA_KV_HEADS = 2
NSA_GROUP = NSA_HEADS // NSA_KV_HEADS
NSA_DH = 64
NSA_WIDTH = NSA_HEADS * NSA_DH
NSA_KV_WIDTH = NSA_KV_HEADS * NSA_DH
CMP_BLOCK = 32
CMP_STRIDE = 16
SEL_BLOCK = 64
N_SELECT = 8
WINDOW = 512
GDN_HEADS = 4
GDN_DH = 128
GDN_WIDTH = GDN_HEADS * GDN_DH
GDN_CHUNK = 64
GDN_CHUNK_SHIFT = GDN_CHUNK.bit_length() - 1

LANES = 128
VMEM_LIMIT_BYTES = 56 * 1024 * 1024

ROW_TILE = 512
FFN_HALO = 8
FFN_COL_CHUNK = 256
FFN_DOWN_GROUP = 4
SEL_KEY_CHUNK = 512
WIN_KEYS = WINDOW + 2 * SEL_BLOCK
GDN_SEQ_TILE = 256
GDN_SCAN_BATCH = 4
GDN_PREP_TILE = 256
GDN_HALO = 8

SM_GATE0 = 0
SM_BETA0 = 3 * NSA_HEADS
SM_A0 = SM_BETA0 + GDN_HEADS


def _dot(a, b, precision=None):
    return jnp.dot(a, b, preferred_element_type=F32, precision=precision)


def _dot_nt(a, b, precision=None):
    return lax.dot_general(a, b, (((1,), (1,)), ((), ())), preferred_element_type=F32, precision=precision)


def _dot_tn(a, b, precision=None):
    return lax.dot_general(a, b, (((0,), (0,)), ((), ())), preferred_element_type=F32, precision=precision)


def _lane_col(x, idx):
    lane = lax.broadcasted_iota(jnp.int32, x.shape, 1)
    return jnp.sum(jnp.where(lane == idx, x, 0.0), axis=-1, keepdims=True)


def _const_spec(shape):
    zeros = (0,) * len(shape)
    return pl.BlockSpec(shape, lambda *_: zeros, pipeline_mode=pl.Buffered(1))


def _params(*semantics):
    return pltpu.CompilerParams(dimension_semantics=semantics, vmem_limit_bytes=VMEM_LIMIT_BYTES)


def _inproj_kernel(x_ref, g_ref, wq_ref, wkv_ref, wgdn_ref, wz_ref, wsm_ref,
                   q_ref, kv_ref, gdn_ref, z_ref, sm_ref):
    x = x_ref[...]
    ms = jnp.mean(x * x, axis=-1, keepdims=True)
    h = (x * lax.rsqrt(ms + EPS) * g_ref[...]).astype(BF16)
    q_ref[...] = _dot(h, wq_ref[...]).astype(BF16)
    kv_ref[...] = _dot(h, wkv_ref[...]).astype(BF16)
    gdn_ref[...] = _dot(h, wgdn_ref[...])
    z_ref[...] = _dot(h, wz_ref[...])
    sm_ref[...] = _dot(h, wsm_ref[...])


def _inproj(x2, g, wq, wkv, wgdn, wz, wsm):
    n, d = x2.shape
    tm = ROW_TILE
    row = lambda w: pl.BlockSpec((tm, w), lambda i: (i, 0))
    widths = (wq.shape[1], wkv.shape[1], wgdn.shape[1], wz.shape[1], wsm.shape[1])
    dtypes = (BF16, BF16, F32, F32, F32)
    return pl.pallas_call(
        _inproj_kernel,
        grid=(n // tm,),
        in_specs=[row(d), _const_spec((1, d))] + [_const_spec(w.shape) for w in (wq, wkv, wgdn, wz, wsm)],
        out_specs=[row(w) for w in widths],
        out_shape=[jax.ShapeDtypeStruct((n, w), dt) for w, dt in zip(widths, dtypes)],
        compiler_params=_params("parallel"),
        name="inproj",
    )(x2, g, wq, wkv, wgdn, wz, wsm)


def _compress_kernel(g_ref, posk_ref, posv_ref, w1k_ref, b1k_ref, w2k_ref, w1v_ref, b1v_ref, w2v_ref,
                     kc_ref, vc_ref):
    ng = g_ref.shape[2]
    half = w1k_ref.shape[0] // 2
    row = lax.broadcasted_iota(jnp.int32, (ng, 1), 0)
    for sel, (pos_ref, w1_ref, b1_ref, w2_ref, out_ref) in enumerate(
            ((posk_ref, w1k_ref, b1k_ref, w2k_ref, kc_ref), (posv_ref, w1v_ref, b1v_ref, w2v_ref, vc_ref))):
        w1 = w1_ref[...]
        pos8 = jnp.broadcast_to(pos_ref[...], (8, pos_ref.shape[1])).astype(BF16)
        bias = _dot(pos8, w1)[0:1, :] + b1_ref[...]
        acc = jnp.zeros((ng, LANES), F32)
        for h in range(NSA_KV_HEADS):
            grp = g_ref[0, sel * NSA_KV_HEADS + h]
            top = _dot(grp, w1[:half])
            bot = _dot(grp, w1[half:])
            hid = top + pltpu.roll(bot, ng - 1, axis=0) + bias
            act = jax.nn.gelu(hid, approximate=True).astype(BF16)
            acc = acc + _dot(act, w2_ref[h])
        out_ref[0] = jnp.where(row < ng - 1, acc, 0.0).astype(BF16)


def _compress(groups, posk, posv, w1k, b1k, w2k, w1v, b1v, w2v):
    b, _, ng, gw = groups.shape
    out = jax.ShapeDtypeStruct((b, ng, LANES), BF16)
    return pl.pallas_call(
        _compress_kernel,
        grid=(b,),
        in_specs=[pl.BlockSpec((1, 2 * NSA_KV_HEADS, ng, gw), lambda i: (i, 0, 0, 0))]
        + [_const_spec(a.shape) for a in (posk, posv, w1k, b1k, w2k, w1v, b1v, w2v)],
        out_specs=[pl.BlockSpec((1, ng, LANES), lambda i: (i, 0, 0))] * 2,
        out_shape=[out, out],
        compiler_params=_params("parallel"),
        name="nsa_compress",
    )(groups, posk, posv, w1k, b1k, w2k, w1v, b1v, w2v)


def _softmax_parts(s, mask):
    sm = jnp.where(mask, s, NEG_INF)
    m = jnp.max(sm, axis=-1, keepdims=True)
    m = jnp.where(jnp.isfinite(m), m, 0.0)
    e = jnp.where(mask, jnp.exp2(sm - m), 0.0)
    return e, jnp.sum(e, axis=-1, keepdims=True)


def _nsa_kernel(q_ref, sm_ref, kc_ref, vc_ref, ks_ref, vs_ref, kw_ref, vw_ref, oh_ref, o_ref,
                m_ref, l_ref, acc_ref):
    qb = q_ref.shape[1]
    seq = ks_ref.shape[1]
    ncmp = kc_ref.shape[1]
    nsb = seq // SEL_BLOCK
    rows = NSA_HEADS * qb
    both = NSA_KV_HEADS * qb
    blk = pl.program_id(1)
    s0 = blk * qb

    gates = jax.nn.sigmoid(sm_ref[0])
    q_all = q_ref[0]
    t_rows = s0 + (lax.broadcasted_iota(jnp.int32, (rows, 1), 0) & (qb - 1))
    lane = lax.broadcasted_iota(jnp.int32, (qb, LANES), 1)

    jn = lax.broadcasted_iota(jnp.int32, (nsb, ncmp), 0)
    nn = lax.broadcasted_iota(jnp.int32, (nsb, ncmp), 1)
    overlap_t = jnp.where((nn * CMP_STRIDE < jn * SEL_BLOCK + SEL_BLOCK)
                          & (jn * SEL_BLOCK < nn * CMP_STRIDE + CMP_BLOCK) & (nn < ncmp - 1), 1.0, 0.0)
    n_cmp_idx = lax.broadcasted_iota(jnp.int32, (1, ncmp), 1)
    cmp_mask = (n_cmp_idx * CMP_STRIDE + (CMP_BLOCK - 1) <= t_rows) & (n_cmp_idx < ncmp - 1)
    qs = jnp.concatenate([q_all[:, hq * LANES:(hq + 1) * LANES] for hq in range(NSA_HEADS)], axis=0)

    d_lo = pl.multiple_of(jnp.maximum(s0 - SEL_BLOCK, 0), SEL_BLOCK)
    diag_mask = d_lo + lax.broadcasted_iota(jnp.int32, (1, 2 * SEL_BLOCK), 1) <= t_rows
    w_lo = jnp.minimum(jnp.maximum(s0 - WINDOW, 0), seq - WIN_KEYS)
    w_lo = pl.multiple_of(w_lo, SEL_BLOCK)
    kpos_w = w_lo + lax.broadcasted_iota(jnp.int32, (1, WIN_KEYS), 1)
    win_bias = jnp.where((kpos_w <= t_rows) & (kpos_w > t_rows - WINDOW), 0.0, MASK_BIAS)

    def value_and_rowsum(p, v):
        return _dot(p.astype(BF16), v), jnp.sum(p, axis=-1, keepdims=True)

    s_c = _dot_nt(qs, kc_ref[0])
    s_w = _dot_nt(qs, kw_ref[0, pl.ds(w_lo, WIN_KEYS), :]) + win_bias
    s_d = jnp.where(diag_mask, _dot_nt(qs, ks_ref[0, pl.ds(d_lo, 2 * SEL_BLOCK), :]), NEG_INF)
    e, l = _softmax_parts(s_c, cmp_mask)
    p_cmp = e / jnp.maximum(l, 1e-30)
    o_cmp = _dot(p_cmp.astype(BF16), vc_ref[0])
    p_sum = []
    for h in range(NSA_KV_HEADS):
        acc = p_cmp[h * NSA_GROUP * qb:(h * NSA_GROUP + 1) * qb]
        for g in range(1, NSA_GROUP):
            acc = acc + p_cmp[(h * NSA_GROUP + g) * qb:(h * NSA_GROUP + g + 1) * qb]
        p_sum.append(acc)
    imp_t = _dot_nt(overlap_t, jnp.concatenate(p_sum, axis=0), precision=lax.Precision.HIGHEST)

    jj = lax.broadcasted_iota(jnp.int32, (nsb, both), 0)
    forced = (jj == 0) | (jj == blk) | (jj == blk - 1)
    score = jnp.where(forced, jnp.inf, jnp.where(jj <= blk, imp_t, NEG_INF))
    rank = jnp.zeros((nsb, both), F32)
    for j2 in range(nsb):
        other = score[j2:j2 + 1, :]
        ahead = (other > score) | ((other == score) & (jj > j2))
        rank = rank + jnp.where(ahead, 1.0, 0.0)
    chosen = (rank < N_SELECT) & (score > NEG_INF) & (jj < blk - 1)
    bias_t = jnp.concatenate([jnp.where(chosen, 0.0, MASK_BIAS), jnp.zeros((LANES - nsb, both), F32)], axis=0)
    eye = jnp.where(lax.broadcasted_iota(jnp.int32, (both, both), 0)
                    == lax.broadcasted_iota(jnp.int32, (both, both), 1), 1.0, 0.0).astype(BF16)
    bias_q = _dot_nt(eye, bias_t.astype(BF16)).astype(BF16)

    e_w = jnp.exp2(s_w - jnp.max(s_w, axis=-1, keepdims=True))
    o_win, l_win = value_and_rowsum(e_w, vw_ref[0, pl.ds(w_lo, WIN_KEYS), :])
    o_win = o_win / jnp.maximum(l_win, 1e-30)

    m0 = jnp.max(s_d, axis=-1, keepdims=True)
    acc0, l0 = value_and_rowsum(jnp.exp2(s_d - m0), vs_ref[0, pl.ds(d_lo, 2 * SEL_BLOCK), :])
    m_ref[...] = jnp.broadcast_to(m0, m_ref.shape)
    l_ref[...] = jnp.broadcast_to(l0, l_ref.shape)
    acc_ref[...] = acc0
    n_chunks = (blk + SEL_KEY_CHUNK // SEL_BLOCK - 2) // (SEL_KEY_CHUNK // SEL_BLOCK)
    bias_rows = jnp.concatenate([bias_q[h * qb:(h + 1) * qb] for h in range(NSA_KV_HEADS)
                                 for _ in range(NSA_GROUP)], axis=0)
    q_aug = jnp.concatenate([qs, bias_rows], axis=1)

    def sel_step(c, _):
        k0 = pl.multiple_of(c * SEL_KEY_CHUNK, SEL_KEY_CHUNK)
        k_aug = jnp.concatenate([ks_ref[0, pl.ds(k0, SEL_KEY_CHUNK), :], oh_ref[pl.ds(k0, SEL_KEY_CHUNK), :]],
                                axis=1)
        s = _dot_nt(q_aug, k_aug)
        m_prev = m_ref[...]
        m_new = jnp.maximum(m_prev, jnp.max(s, axis=-1, keepdims=True))
        p = jnp.exp2(s - jnp.concatenate([m_new] * (SEL_KEY_CHUNK // LANES), axis=1))
        alpha = jnp.exp2(m_prev - m_new)
        pv, psum = value_and_rowsum(p, vs_ref[0, pl.ds(k0, SEL_KEY_CHUNK), :])
        m_ref[...] = m_new
        l_ref[...] = alpha * l_ref[...] + psum
        acc_ref[...] = alpha * acc_ref[...] + pv
        return 0

    lax.fori_loop(0, n_chunks, sel_step, 0)
    o_sel = acc_ref[...] / jnp.maximum(l_ref[...], 1e-30)

    mixed = []
    for hq in range(NSA_HEADS):
        r = slice(hq * qb, (hq + 1) * qb)
        mixed.append(_lane_col(gates, SM_GATE0 + 3 * hq) * o_cmp[r]
                     + _lane_col(gates, SM_GATE0 + 3 * hq + 1) * o_sel[r]
                     + _lane_col(gates, SM_GATE0 + 3 * hq + 2) * o_win[r])
    for pair in range(NSA_HEADS // 2):
        a, b = mixed[2 * pair], mixed[2 * pair + 1]
        if 2 * pair // NSA_GROUP == 0:
            b = pltpu.roll(b, NSA_DH, axis=1)
        else:
            a = pltpu.roll(a, NSA_DH, axis=1)
        o_ref[0, :, pair * LANES:(pair + 1) * LANES] = jnp.where(lane < NSA_DH, a, b)


def _nsa_attention(q, sm, kc, vc, kv):
    b, s, qw = q.shape
    qb = SEL_BLOCK
    ncmp = kc.shape[1]
    onehot = (jnp.arange(s)[:, None] // SEL_BLOCK == jnp.arange(LANES)[None, :]).astype(BF16)
    kv_spec = lambda col: pl.BlockSpec((1, s, LANES), lambda i, j: (i, 0, col))
    return pl.pallas_call(
        _nsa_kernel,
        grid=(b, s // qb),
        in_specs=[pl.BlockSpec((1, qb, qw), lambda i, j: (i, j, 0)),
                  pl.BlockSpec((1, qb, LANES), lambda i, j: (i, j, 0)),
                  pl.BlockSpec((1, ncmp, LANES), lambda i, j: (i, 0, 0)),
                  pl.BlockSpec((1, ncmp, LANES), lambda i, j: (i, 0, 0)),
                  kv_spec(2), kv_spec(3), kv_spec(4), kv_spec(5), _const_spec((s, LANES))],
        out_specs=pl.BlockSpec((1, qb, NSA_WIDTH), lambda i, j: (i, j, 0)),
        out_shape=jax.ShapeDtypeStruct((b, s, NSA_WIDTH), F32),
        scratch_shapes=[pltpu.VMEM((NSA_HEADS * qb, LANES), F32)] * 3,
        compiler_params=_params("parallel", "arbitrary"),
        name="nsa_attention",
    )(q, sm, kc, vc, kv, kv, kv, kv, onehot)


def _unit_lower_inverses(mats, r, c):
    n = mats[0].shape[0]
    eye = jnp.where(r == c, 1.0, 0.0)
    blk16 = (r >> 4) == (c >> 4)
    each = lambda f, *lists: [f(*args) for args in zip(*lists)]
    mm = lambda x, y: _dot(x.astype(BF16), y.astype(BF16))
    stacked = lambda top, rhs: mm(jnp.concatenate([top, rhs], axis=0), rhs)
    n1 = each(lambda a: jnp.where(blk16, -a, 0.0), mats)
    n2 = each(mm, n1, n1)
    t = each(lambda m: eye + m, n1)
    x = each(stacked, t, n2)
    t, n4 = each(lambda ti, xi: ti + xi[:n], t, x), each(lambda xi: xi[n:], x)
    x = each(stacked, t, n4)
    t, n8 = each(lambda ti, xi: ti + xi[:n], t, x), each(lambda xi: xi[n:], x)
    t = each(lambda ti, m: ti + mm(ti, m), t, n8)
    v = each(lambda a, ti: mm(jnp.where(blk16, 0.0, a), ti), mats, t)
    x = each(stacked, t, v)
    p, v2 = each(lambda ti, xi: ti - xi[:n], t, x), each(lambda xi: xi[n:], x)
    return each(lambda pi, m: pi + mm(pi, m), p, v2)


def _log_decay(sm, alog_ref, dtb_ref):
    return -jnp.exp(alog_ref[...]) * jax.nn.softplus(sm + dtb_ref[...])


def _gdn_prep_kernel(x_ref, halo_ref, cw_ref, sm_ref, alog_ref, dtb_ref,
                     qd_ref, kdt_ref, at_ref, u_ref, w_ref, gcum_ref):
    n = x_ref.shape[1]
    width = cw_ref.shape[0]
    has_history = jnp.where(pl.program_id(1) > 0, 1.0, 0.0)

    def conv_silu(c0):
        cols = slice(c0, c0 + GDN_DH)
        xe = jnp.concatenate([halo_ref[0, :, cols] * has_history, x_ref[0, :, cols]], axis=0)
        y = xe[GDN_HALO:] * cw_ref[width - 1:width, cols]
        for k in range(width - 1):
            y = y + pltpu.roll(xe, width - 1 - k, axis=0)[GDN_HALO:] * cw_ref[k:k + 1, cols]
        return jax.nn.silu(y)

    def l2norm(y):
        return y * lax.rsqrt(jnp.sum(y * y, axis=-1, keepdims=True) + EPS)

    r = lax.broadcasted_iota(jnp.int32, (n, n), 0)
    c = lax.broadcasted_iota(jnp.int32, (n, n), 1)
    in_chunk = (r >> GDN_CHUNK_SHIFT) == (c >> GDN_CHUNK_SHIFT)
    causal = in_chunk & (r >= c)
    hi = lax.Precision.HIGHEST
    sm = sm_ref[0]
    beta = jax.nn.sigmoid(sm)
    g = _log_decay(sm, alog_ref, dtb_ref)
    gcum = _dot(jnp.where(causal, 1.0, 0.0), g, precision=hi)
    gtot = _dot(jnp.where(in_chunk, 1.0, 0.0), g, precision=hi)
    gcum_ref[0] = gcum
    pick = jnp.where(lax.broadcasted_iota(jnp.int32, (8, LANES), 1)
                     == lax.broadcasted_iota(jnp.int32, (8, LANES), 0) + SM_A0, 1.0, 0.0)
    gcum_rows = _dot_nt(pick, gcum, precision=hi)

    a_mats, rhs = [], []
    for h in range(GDN_HEADS):
        cols = slice(h * GDN_DH, (h + 1) * GDN_DH)
        q = l2norm(conv_silu(h * GDN_DH)) * (GDN_DH ** -0.5)
        k = l2norm(conv_silu(GDN_WIDTH + h * GDN_DH))
        v = conv_silu(2 * GDN_WIDTH + h * GDN_DH)
        gc = _lane_col(gcum, SM_A0 + h)
        bcol = _lane_col(beta, SM_BETA0 + h)
        decay = jnp.exp(jnp.where(causal, gc - gcum_rows[h:h + 1, :], NEG_INF))
        kb = k * bcol
        egc = jnp.exp(gc)
        qd_ref[0, :, cols] = (q * egc).astype(BF16)
        kdt_ref[0, cols, :] = (k * jnp.exp(_lane_col(gtot, SM_A0 + h) - gc)).T.astype(BF16)
        grams = _dot_nt(jnp.concatenate([kb, q], axis=0).astype(BF16), k.astype(BF16))
        attn = (grams[n:] * decay).astype(BF16)
        for ch in range(n // GDN_CHUNK):
            lt = ch * GDN_CHUNK // LANES
            at_ref[0, ch * GDN_CHUNK:(ch + 1) * GDN_CHUNK, cols] = (
                attn[ch * GDN_CHUNK:(ch + 1) * GDN_CHUNK, lt * LANES:(lt + 1) * LANES])
        a_mats.append(jnp.where(r > c, grams[:n] * decay, 0.0))
        rhs.append(jnp.concatenate([v * bcol, kb * egc], axis=1).astype(BF16))
    for h, t_inv in enumerate(_unit_lower_inverses(a_mats, r, c)):
        cols = slice(h * GDN_DH, (h + 1) * GDN_DH)
        uw = _dot(t_inv.astype(BF16), rhs[h])
        u_ref[0, :, cols] = uw[:, :GDN_DH]
        w_ref[0, :, cols] = uw[:, GDN_DH:].astype(BF16)


def _gdn_prep(qkv, cw, sm, alog_vec, dtb_vec):
    b, s, width = qkv.shape
    n = GDN_PREP_TILE
    per = n // GDN_HALO
    tok = lambda w: pl.BlockSpec((1, n, w), lambda i, j: (i, j, 0))
    out = lambda dt: jax.ShapeDtypeStruct((b, s, GDN_WIDTH), dt)
    return pl.pallas_call(
        _gdn_prep_kernel,
        grid=(b, s // n),
        in_specs=[tok(width),
                  pl.BlockSpec((1, GDN_HALO, width), lambda i, j: (i, jnp.maximum(j * per - 1, 0), 0)),
                  _const_spec(cw.shape), tok(LANES), _const_spec((1, LANES)), _const_spec((1, LANES))],
        out_specs=[tok(GDN_WIDTH), pl.BlockSpec((1, GDN_WIDTH, n), lambda i, j: (i, 0, j))]
        + [tok(GDN_WIDTH)] * 3 + [tok(LANES)],
        out_shape=[out(BF16), jax.ShapeDtypeStruct((b, GDN_WIDTH, s), BF16), out(BF16), out(F32), out(BF16),
                   jax.ShapeDtypeStruct((b, s, LANES), F32)],
        compiler_params=_params("parallel", "parallel"),
        name="gdn_prep",
    )(qkv, qkv, cw, sm, alog_vec, dtb_vec)


def _gdn_scan_kernel(qd_ref, kdt_ref, at_ref, u_ref, w_ref, z_ref, gcum_ref, ng_ref, o_ref, state_ref):
    c = GDN_CHUNK
    n_chunks = qd_ref.shape[1] // c

    @pl.when(pl.program_id(1) == 0)
    def _():
        state_ref[...] = jnp.zeros_like(state_ref)

    def chunk_step(ci, _):
        t0 = pl.multiple_of(ci * c, c)
        rows = pl.ds(t0, c)
        odd = (ci & 1) == 1
        pair = pl.ds(pl.multiple_of((ci >> 1) * (2 * c), 2 * c), 2 * c)
        chains = [(bi, h) for bi in range(qd_ref.shape[0]) for h in range(GDN_HEADS)]
        cols = lambda h: slice(h * GDN_DH, (h + 1) * GDN_DH)
        states = [state_ref[bi * GDN_HEADS + h] for bi, h in chains]
        ws_qs = [_dot(jnp.concatenate([w_ref[bi, rows, cols(h)], qd_ref[bi, rows, cols(h)]], axis=0),
                      s.astype(BF16)) for (bi, h), s in zip(chains, states)]
        av_kv = []
        for (bi, h), x in zip(chains, ws_qs):
            v_new = (u_ref[bi, rows, cols(h)] - x[:c]).astype(BF16)
            zero = jnp.zeros_like(v_new)
            v_pad = jnp.concatenate([jnp.where(odd, zero, v_new), jnp.where(odd, v_new, zero)], axis=0)
            av_kv.append(_dot(jnp.concatenate([at_ref[bi, rows, cols(h)], kdt_ref[bi, cols(h), pair]], axis=0),
                              v_pad))
        for (bi, h), s, x, y in zip(chains, states, ws_qs, av_kv):
            state_decay = jnp.exp(gcum_ref[bi, pl.ds(t0 + c - 1, 1), :])
            state_ref[bi * GDN_HEADS + h] = s * _lane_col(state_decay, SM_A0 + h) + y[c:]
            o = x[c:] + y[:c]
            o = o * lax.rsqrt(jnp.mean(o * o, axis=-1, keepdims=True) + EPS) * ng_ref[...]
            o_ref[bi, rows, cols(h)] = o * jax.nn.silu(z_ref[bi, rows, cols(h)])
        return 0

    lax.fori_loop(0, n_chunks, chunk_step, 0)


def _gdn_scan(qd, kdt, at, u, w, z, gcum, norm_g):
    b, s, _ = qd.shape
    ts = GDN_SEQ_TILE
    nb = GDN_SCAN_BATCH if b % GDN_SCAN_BATCH == 0 else 1
    tok = lambda wd: pl.BlockSpec((nb, ts, wd), lambda i, j: (i, j, 0))
    return pl.pallas_call(
        _gdn_scan_kernel,
        grid=(b // nb, s // ts),
        in_specs=[tok(GDN_WIDTH), pl.BlockSpec((nb, GDN_WIDTH, ts), lambda i, j: (i, 0, j))]
        + [tok(GDN_WIDTH)] * 4 + [tok(LANES), _const_spec((1, GDN_DH))],
        out_specs=tok(GDN_WIDTH),
        out_shape=jax.ShapeDtypeStruct((b, s, GDN_WIDTH), F32),
        scratch_shapes=[pltpu.VMEM((nb * GDN_HEADS, GDN_DH, GDN_DH), F32)],
        compiler_params=_params("parallel", "arbitrary"),
        name="gdn_scan",
    )(qd, kdt, at, u, w, z, gcum, norm_g)


def _outproj_kernel(on_ref, og_ref, x_ref, g_ref, wa_ref, wb_ref, o_ref):
    on = on_ref[...]
    ms = jnp.mean(on * on, axis=-1, keepdims=True)
    hn = (on * lax.rsqrt(ms + EPS) * g_ref[...]).astype(BF16)
    o_ref[...] = x_ref[...] + _dot(hn, wa_ref[...]) + _dot(og_ref[...].astype(BF16), wb_ref[...])


def _outproj(o_nsa, o_gdn, x2, g, wa, wb):
    n, d = x2.shape
    tm = ROW_TILE
    row = lambda w: pl.BlockSpec((tm, w), lambda i: (i, 0))
    return pl.pallas_call(
        _outproj_kernel,
        grid=(n // tm,),
        in_specs=[row(o_nsa.shape[1]), row(o_gdn.shape[1]), row(d),
                  _const_spec(g.shape), _const_spec(wa.shape), _const_spec(wb.shape)],
        out_specs=row(d),
        out_shape=jax.ShapeDtypeStruct((n, d), F32),
        compiler_params=_params("parallel"),
        name="outproj",
    )(o_nsa, o_gdn, x2, g, wa, wb)


def _ffn_kernel(x_ref, halo_ref, g_ref, wup_ref, cw_ref, wdn_ref, gf_ref, o_ref, *, final_norm):
    tm = x_ref.shape[1]
    d_ff = wdn_ref.shape[0]
    width = cw_ref.shape[0]
    x = x_ref[0]
    has_history = jnp.where(pl.program_id(1) > 0, 1.0, 0.0)
    xin = jnp.concatenate([halo_ref[0] * has_history, x], axis=0)
    ms = jnp.mean(xin * xin, axis=-1, keepdims=True)
    h = (xin * lax.rsqrt(ms + EPS) * g_ref[...]).astype(BF16)

    def conv(u, c0):
        y = u[FFN_HALO:] * cw_ref[width - 1:width, c0:c0 + FFN_COL_CHUNK]
        for k in range(width - 1):
            shifted = pltpu.roll(u, width - 1 - k, axis=0)[FFN_HALO:]
            y = y + shifted * cw_ref[k:k + 1, c0:c0 + FFN_COL_CHUNK]
        return y

    def up_proj(j):
        c0 = j * FFN_COL_CHUNK
        return (_dot(h, wup_ref[:, c0:c0 + FFN_COL_CHUNK]),
                _dot(h, wup_ref[:, d_ff + c0:d_ff + c0 + FFN_COL_CHUNK]))

    n_chunks = d_ff // FFN_COL_CHUNK
    y = x
    acts = []
    nxt = up_proj(0)
    for j in range(n_chunks):
        c0 = j * FFN_COL_CHUNK
        u_gate, u_up = nxt
        if j + 1 < n_chunks:
            nxt = up_proj(j + 1)
        acts.append((jax.nn.silu(conv(u_gate, c0)) * conv(u_up, d_ff + c0)).astype(BF16))
        if len(acts) == FFN_DOWN_GROUP or j + 1 == n_chunks:
            k0 = c0 + FFN_COL_CHUNK - len(acts) * FFN_COL_CHUNK
            y = y + _dot(jnp.concatenate(acts, axis=1), wdn_ref[k0:c0 + FFN_COL_CHUNK, :])
            acts = []
    if final_norm:
        y = y * lax.rsqrt(jnp.mean(y * y, axis=-1, keepdims=True) + EPS) * gf_ref[...]
    o_ref[0] = y


def _ffn(x3, g, wup, cw, wdn, g_final, final_norm):
    b, s, d = x3.shape
    tm = ROW_TILE
    per = tm // FFN_HALO
    return pl.pallas_call(
        functools.partial(_ffn_kernel, final_norm=final_norm),
        grid=(b, s // tm),
        in_specs=[pl.BlockSpec((1, tm, d), lambda i, j: (i, j, 0)),
                  pl.BlockSpec((1, FFN_HALO, d), lambda i, j: (i, jnp.maximum(j * per - 1, 0), 0)),
                  _const_spec(g.shape), _const_spec(wup.shape), _const_spec(cw.shape), _const_spec(wdn.shape),
                  _const_spec(g_final.shape)],
        out_specs=pl.BlockSpec((1, tm, d), lambda i, j: (i, j, 0)),
        out_shape=jax.ShapeDtypeStruct((b, s, d), F32),
        compiler_params=_params("parallel", "parallel"),
        name="convffn",
    )(x3, x3, g, wup, cw, wdn, g_final)


def _split_in_weights(w_in):
    d = w_in.shape[0]
    o = 0
    wq = w_in[:, o:o + NSA_WIDTH]; o += NSA_WIDTH
    wkv = w_in[:, o:o + 6 * NSA_KV_WIDTH]; o += 6 * NSA_KV_WIDTH
    wgate = w_in[:, o:o + 3 * NSA_HEADS]; o += 3 * NSA_HEADS
    wgdn = w_in[:, o:o + 3 * GDN_WIDTH]; o += 3 * GDN_WIDTH
    wz = w_in[:, o:o + GDN_WIDTH]; o += GDN_WIDTH
    wb = w_in[:, o:o + GDN_HEADS]; o += GDN_HEADS
    wa = w_in[:, o:o + GDN_HEADS]
    wq = wq.reshape(d, NSA_HEADS, NSA_DH) * (NSA_DH ** -0.5 * LOG2_E)
    zero = jnp.zeros_like(wq)
    kv_head = (jnp.arange(NSA_HEADS) // NSA_GROUP)[None, :, None]
    wq_pad = jnp.concatenate([jnp.where(kv_head == 0, wq, zero), jnp.where(kv_head == 1, wq, zero)], axis=-1)
    wq_pad = wq_pad.reshape(d, NSA_HEADS * LANES)
    wsm = jnp.concatenate([wgate, wb, wa, jnp.zeros((d, LANES - SM_A0 - GDN_HEADS), w_in.dtype)], axis=-1)
    return tuple(w.astype(BF16) for w in (wq_pad, wkv, wgdn, wz, wsm))


def _lane_vec(v, offset):
    return jnp.zeros((1, LANES), F32).at[0, offset:offset + v.shape[0]].set(v.astype(F32))


def _head_padded_w2(w2):
    z = jnp.zeros_like(w2)
    return jnp.stack([jnp.concatenate([w2, z], axis=-1), jnp.concatenate([z, w2], axis=-1)]).astype(BF16)


def kernel(x, norm_mix, w_in, cmp_pos_k, cmp_pos_v, cmp_k_w1, cmp_k_b1, cmp_k_w2, cmp_v_w1, cmp_v_b1, cmp_v_w2,
           nsa_norm, gdn_conv, gdn_a_log, gdn_dt_bias, gdn_norm, w_out, norm_ffn, ffn_up, ffn_conv, ffn_down,
           norm_final):
    b, s, d = x.shape
    depth = w_in.shape[0]
    group = CMP_BLOCK // 2
    assert CMP_STRIDE == group and s % GDN_SEQ_TILE == 0 and s >= WIN_KEYS and (b * s) % ROW_TILE == 0
    x2 = x.reshape(b * s, d)
    for l in range(depth):
        wq, wkv, wgdn, wz, wsm = _split_in_weights(w_in[l])
        q, kv, gdn, z, sm = _inproj(x2, norm_mix[l][None, :], wq, wkv, wgdn, wz, wsm)
        q, kv, gdn, z, sm = (a.reshape(b, s, a.shape[-1]) for a in (q, kv, gdn, z, sm))

        groups = kv[:, :, :2 * NSA_KV_WIDTH].reshape(b, s, 2 * NSA_KV_HEADS, NSA_DH)
        groups = groups.transpose(0, 2, 1, 3).reshape(b, 2 * NSA_KV_HEADS, s // group, group * NSA_DH)
        kc, vc = _compress(
            groups, cmp_pos_k[l].reshape(1, -1), cmp_pos_v[l].reshape(1, -1),
            cmp_k_w1[l].astype(BF16), cmp_k_b1[l][None, :], _head_padded_w2(cmp_k_w2[l]),
            cmp_v_w1[l].astype(BF16), cmp_v_b1[l][None, :], _head_padded_w2(cmp_v_w2[l]))
        o_nsa = _nsa_attention(q, sm, kc, vc, kv)

        alog_vec, dtb_vec = _lane_vec(gdn_a_log[l], SM_A0), _lane_vec(gdn_dt_bias[l], SM_A0)
        qd, kdt, at, gu, gw, gcum = _gdn_prep(gdn, gdn_conv[l], sm, alog_vec, dtb_vec)
        o_gdn = _gdn_scan(qd, kdt, at, gu, gw, z, gcum, gdn_norm[l][None, :])

        w_o = w_out[l].astype(BF16)
        x2 = _outproj(o_nsa.reshape(b * s, -1), o_gdn.reshape(b * s, -1), x2, nsa_norm[l][None, :],
                      w_o[:NSA_WIDTH], w_o[NSA_WIDTH:])
        x2 = _ffn(x2.reshape(b, s, d), norm_ffn[l][None, :], ffn_up[l].astype(BF16), ffn_conv[l],
                  ffn_down[l].astype(BF16), norm_final[None, :], final_norm=(l == depth - 1)).reshape(b * s, d)
    return x2.reshape(b, s, d)
```

```python
import functools

import jax
import jax.numpy as jnp
from jax import lax
from jax.experimental import pallas as pl
from jax.experimental.pallas import tpu as pltpu

F32 = jnp.float32
BF16 = jnp.bfloat16
EPS = 1e-6
NEG_INF = float("-inf")
MASK_BIAS = -1e30
LOG2_E = 1.4426950408889634

NSA_HEADS = 8
NSA_KV_HEADS = 2
NSA_GROUP = NSA_HEADS // NSA_KV_HEADS
NSA_DH = 64
NSA_WIDTH = NSA_HEADS * NSA_DH
NSA_KV_WIDTH = NSA_KV_HEADS * NSA_DH
CMP_BLOCK = 32
CMP_STRIDE = 16
SEL_BLOCK = 64
N_SELECT = 8
WINDOW = 512
GDN_HEADS = 4
GDN_DH = 128
GDN_WIDTH = GDN_HEADS * GDN_DH
GDN_CHUNK = 64
GDN_CHUNK_SHIFT = GDN_CHUNK.bit_length() - 1

LANES = 128
VMEM_LIMIT_BYTES = 56 * 1024 * 1024

ROW_TILE = 512
FFN_HALO = 8
FFN_COL_CHUNK = 256
FFN_DOWN_GROUP = 4
SEL_KEY_CHUNK = 512
NSA_STEP_BLOCKS = 2
WIN_KEYS = WINDOW + 2 * SEL_BLOCK
GDN_SEQ_TILE = 256
GDN_SCAN_BATCH = 4
GDN_PREP_TILE = 256
GDN_HALO = 8

SM_GATE0 = 0
SM_BETA0 = 3 * NSA_HEADS
SM_A0 = SM_BETA0 + GDN_HEADS


def _dot(a, b, precision=None):
    return jnp.dot(a, b, preferred_element_type=F32, precision=precision)


def _dot_nt(a, b, precision=None):
    return lax.dot_general(a, b, (((1,), (1,)), ((), ())), preferred_element_type=F32, precision=precision)


def _dot_tn(a, b, precision=None):
    return lax.dot_general(a, b, (((0,), (0,)), ((), ())), preferred_element_type=F32, precision=precision)


def _lane_col(x, idx):
    lane = lax.broadcasted_iota(jnp.int32, x.shape, 1)
    return jnp.sum(jnp.where(lane == idx, x, 0.0), axis=-1, keepdims=True)


def _const_spec(shape):
    zeros = (0,) * len(shape)
    return pl.BlockSpec(shape, lambda *_: zeros, pipeline_mode=pl.Buffered(1))


def _params(*semantics):
    return pltpu.CompilerParams(dimension_semantics=semantics, vmem_limit_bytes=VMEM_LIMIT_BYTES)


def _inproj_kernel(x_ref, g_ref, wq_ref, wkv_ref, wgdn_ref, wz_ref, wsm_ref,
                   q_ref, kv_ref, gdn_ref, z_ref, sm_ref):
    x = x_ref[...]
    ms = jnp.mean(x * x, axis=-1, keepdims=True)
    h = (x * lax.rsqrt(ms + EPS) * g_ref[...]).astype(BF16)
    q_ref[...] = _dot(h, wq_ref[...]).astype(BF16)
    kv_ref[...] = _dot(h, wkv_ref[...]).astype(BF16)
    gdn_ref[...] = _dot(h, wgdn_ref[...])
    z_ref[...] = _dot(h, wz_ref[...])
    sm_ref[...] = _dot(h, wsm_ref[...])


def _inproj(x2, g, wq, wkv, wgdn, wz, wsm):
    n, d = x2.shape
    tm = ROW_TILE
    row = lambda w: pl.BlockSpec((tm, w), lambda i: (i, 0))
    widths = (wq.shape[1], wkv.shape[1], wgdn.shape[1], wz.shape[1], wsm.shape[1])
    dtypes = (BF16, BF16, F32, F32, F32)
    return pl.pallas_call(
        _inproj_kernel,
        grid=(n // tm,),
        in_specs=[row(d), _const_spec((1, d))] + [_const_spec(w.shape) for w in (wq, wkv, wgdn, wz, wsm)],
        out_specs=[row(w) for w in widths],
        out_shape=[jax.ShapeDtypeStruct((n, w), dt) for w, dt in zip(widths, dtypes)],
        compiler_params=_params("parallel"),
        name="inproj",
    )(x2, g, wq, wkv, wgdn, wz, wsm)


def _compress_kernel(g_ref, posk_ref, posv_ref, w1k_ref, b1k_ref, w2k_ref, w1v_ref, b1v_ref, w2v_ref,
                     kc_ref, vc_ref):
    ng = g_ref.shape[2]
    half = w1k_ref.shape[0] // 2
    row = lax.broadcasted_iota(jnp.int32, (ng, 1), 0)
    for sel, (pos_ref, w1_ref, b1_ref, w2_ref, out_ref) in enumerate(
            ((posk_ref, w1k_ref, b1k_ref, w2k_ref, kc_ref), (posv_ref, w1v_ref, b1v_ref, w2v_ref, vc_ref))):
        w1 = w1_ref[...]
        pos8 = jnp.broadcast_to(pos_ref[...], (8, pos_ref.shape[1])).astype(BF16)
        bias = _dot(pos8, w1)[0:1, :] + b1_ref[...]
        acc = jnp.zeros((ng, LANES), F32)
        for h in range(NSA_KV_HEADS):
            grp = g_ref[0, sel * NSA_KV_HEADS + h]
            top = _dot(grp, w1[:half])
            bot = _dot(grp, w1[half:])
            hid = top + pltpu.roll(bot, ng - 1, axis=0) + bias
            act = jax.nn.gelu(hid, approximate=True).astype(BF16)
            acc = acc + _dot(act, w2_ref[h])
        out_ref[0] = jnp.where(row < ng - 1, acc, 0.0).astype(BF16)


def _compress(groups, posk, posv, w1k, b1k, w2k, w1v, b1v, w2v):
    b, _, ng, gw = groups.shape
    out = jax.ShapeDtypeStruct((b, ng, LANES), BF16)
    return pl.pallas_call(
        _compress_kernel,
        grid=(b,),
        in_specs=[pl.BlockSpec((1, 2 * NSA_KV_HEADS, ng, gw), lambda i: (i, 0, 0, 0))]
        + [_const_spec(a.shape) for a in (posk, posv, w1k, b1k, w2k, w1v, b1v, w2v)],
        out_specs=[pl.BlockSpec((1, ng, LANES), lambda i: (i, 0, 0))] * 2,
        out_shape=[out, out],
        compiler_params=_params("parallel"),
        name="nsa_compress",
    )(groups, posk, posv, w1k, b1k, w2k, w1v, b1v, w2v)


def _softmax_parts(s, mask):
    sm = jnp.where(mask, s, NEG_INF)
    m = jnp.max(sm, axis=-1, keepdims=True)
    m = jnp.where(jnp.isfinite(m), m, 0.0)
    e = jnp.where(mask, jnp.exp2(sm - m), 0.0)
    return e, jnp.sum(e, axis=-1, keepdims=True)


def _nsa_kernel(q_ref, sm_ref, kc_ref, vc_ref, ks_ref, vs_ref, kw_ref, vw_ref, oh_ref, o_ref,
                m_ref, l_ref, acc_ref):
    qb = SEL_BLOCK
    nq = q_ref.shape[1] // qb
    seq = ks_ref.shape[1]
    ncmp = kc_ref.shape[1]
    nsb = seq // SEL_BLOCK
    rows = NSA_HEADS * qb
    both = NSA_KV_HEADS * qb
    blocks = range(nq)
    each = lambda f, *lists: [f(*args) for args in zip(*lists)]
    blk = [pl.program_id(1) * nq + i for i in blocks]
    s0 = [b * qb for b in blk]

    lane = lax.broadcasted_iota(jnp.int32, (qb, LANES), 1)
    row_q = lax.broadcasted_iota(jnp.int32, (rows, 1), 0) & (qb - 1)
    t_rows = [s + row_q for s in s0]
    jn = lax.broadcasted_iota(jnp.int32, (nsb, ncmp), 0)
    nn = lax.broadcasted_iota(jnp.int32, (nsb, ncmp), 1)
    overlap_t = jnp.where((nn * CMP_STRIDE < jn * SEL_BLOCK + SEL_BLOCK)
                          & (jn * SEL_BLOCK < nn * CMP_STRIDE + CMP_BLOCK) & (nn < ncmp - 1), 1.0, 0.0)
    n_cmp_idx = lax.broadcasted_iota(jnp.int32, (1, ncmp), 1)
    jj = lax.broadcasted_iota(jnp.int32, (nsb, both), 0)
    eye = jnp.where(lax.broadcasted_iota(jnp.int32, (both, both), 0)
                    == lax.broadcasted_iota(jnp.int32, (both, both), 1), 1.0, 0.0).astype(BF16)

    gates = [jax.nn.sigmoid(sm_ref[0, i * qb:(i + 1) * qb]) for i in blocks]
    gcol = [[_lane_col(g, SM_GATE0 + j) for j in range(3 * NSA_HEADS)] for g in gates]

    qs = [jnp.concatenate([q_ref[0, i * qb:(i + 1) * qb, hq * LANES:(hq + 1) * LANES]
                           for hq in range(NSA_HEADS)], axis=0) for i in blocks]
    cmp_mask = [(n_cmp_idx * CMP_STRIDE + (CMP_BLOCK - 1) <= t) & (n_cmp_idx < ncmp - 1) for t in t_rows]
    d_lo = [pl.multiple_of(jnp.maximum(s - SEL_BLOCK, 0), SEL_BLOCK) for s in s0]
    diag_mask = [d + lax.broadcasted_iota(jnp.int32, (1, 2 * SEL_BLOCK), 1) <= t for d, t in zip(d_lo, t_rows)]
    w_lo = [pl.multiple_of(jnp.minimum(jnp.maximum(s - WINDOW, 0), seq - WIN_KEYS), SEL_BLOCK) for s in s0]
    kpos_w = [w + lax.broadcasted_iota(jnp.int32, (1, WIN_KEYS), 1) for w in w_lo]
    win_bias = [jnp.where((k <= t) & (k > t - WINDOW), 0.0, MASK_BIAS) for k, t in zip(kpos_w, t_rows)]

    s_c = [_dot_nt(q, kc_ref[0]) for q in qs]
    s_w = [_dot_nt(q, kw_ref[0, pl.ds(w, WIN_KEYS), :]) + b for q, w, b in zip(qs, w_lo, win_bias)]
    s_d = [jnp.where(m, _dot_nt(q, ks_ref[0, pl.ds(d, 2 * SEL_BLOCK), :]), NEG_INF)
           for q, d, m in zip(qs, d_lo, diag_mask)]

    parts = each(_softmax_parts, s_c, cmp_mask)
    p_cmp = [e / jnp.maximum(l, 1e-30) for e, l in parts]
    o_cmp = [_dot(p.astype(BF16), vc_ref[0]) for p in p_cmp]

    def group_sums(p):
        out = []
        for h in range(NSA_KV_HEADS):
            acc = p[h * NSA_GROUP * qb:(h * NSA_GROUP + 1) * qb]
            for g in range(1, NSA_GROUP):
                acc = acc + p[(h * NSA_GROUP + g) * qb:(h * NSA_GROUP + g + 1) * qb]
            out.append(acc)
        return jnp.concatenate(out, axis=0)

    imp_t = [_dot_nt(overlap_t, group_sums(p), precision=lax.Precision.HIGHEST) for p in p_cmp]

    def selection_bias(imp, b):
        forced = (jj == 0) | (jj == b) | (jj == b - 1)
        score = jnp.where(forced, jnp.inf, jnp.where(jj <= b, imp, NEG_INF))
        rank = jnp.zeros((nsb, both), F32)
        for j2 in range(nsb):
            other = score[j2:j2 + 1, :]
            ahead = (other > score) | ((other == score) & (jj > j2))
            rank = rank + jnp.where(ahead, 1.0, 0.0)
        chosen = (rank < N_SELECT) & (score > NEG_INF) & (jj < b - 1)
        bias_t = jnp.concatenate([jnp.where(chosen, 0.0, MASK_BIAS), jnp.zeros((LANES - nsb, both), F32)], axis=0)
        return bias_t.astype(BF16)

    bias_t = each(selection_bias, imp_t, blk)
    bias_q = [_dot_nt(eye, b).astype(BF16) for b in bias_t]

    e_w = [jnp.exp2(s - jnp.max(s, axis=-1, keepdims=True)) for s in s_w]
    o_win = [_dot(e.astype(BF16), vw_ref[0, pl.ds(w, WIN_KEYS), :]) / jnp.maximum(
        jnp.sum(e, axis=-1, keepdims=True), 1e-30) for e, w in zip(e_w, w_lo)]

    m0 = [jnp.max(s, axis=-1, keepdims=True) for s in s_d]
    p0 = [jnp.exp2(s - m) for s, m in zip(s_d, m0)]
    for i in blocks:
        m_ref[i] = jnp.broadcast_to(m0[i], m_ref.shape[1:])
        l_ref[i] = jnp.broadcast_to(jnp.sum(p0[i], axis=-1, keepdims=True), l_ref.shape[1:])
        acc_ref[i] = _dot(p0[i].astype(BF16), vs_ref[0, pl.ds(d_lo[i], 2 * SEL_BLOCK), :])
    q_aug = [jnp.concatenate([q, jnp.concatenate([b[h * qb:(h + 1) * qb] for h in range(NSA_KV_HEADS)
                                                  for _ in range(NSA_GROUP)], axis=0)], axis=1)
             for q, b in zip(qs, bias_q)]
    n_chunks = (blk[-1] + SEL_KEY_CHUNK // SEL_BLOCK - 2) // (SEL_KEY_CHUNK // SEL_BLOCK)

    def sel_step(c, _):
        k0 = pl.multiple_of(c * SEL_KEY_CHUNK, SEL_KEY_CHUNK)
        k_aug = jnp.concatenate([ks_ref[0, pl.ds(k0, SEL_KEY_CHUNK), :], oh_ref[pl.ds(k0, SEL_KEY_CHUNK), :]],
                                axis=1)
        v_blk = vs_ref[0, pl.ds(k0, SEL_KEY_CHUNK), :]
        s = [_dot_nt(q, k_aug) for q in q_aug]
        for i in blocks:
            m_prev = m_ref[i]
            m_new = jnp.maximum(m_prev, jnp.max(s[i], axis=-1, keepdims=True))
            p = jnp.exp2(s[i] - jnp.concatenate([m_new] * (SEL_KEY_CHUNK // LANES), axis=1))
            alpha = jnp.exp2(m_prev - m_new)
            m_ref[i] = m_new
            l_ref[i] = alpha * l_ref[i] + jnp.sum(p, axis=-1, keepdims=True)
            acc_ref[i] = alpha * acc_ref[i] + _dot(p.astype(BF16), v_blk)
        return 0

    head_rows = lambda x, hq: x[hq * qb:(hq + 1) * qb]
    partial = [[gcol[i][3 * hq] * head_rows(o_cmp[i], hq) + gcol[i][3 * hq + 2] * head_rows(o_win[i], hq)
                for hq in range(NSA_HEADS)] for i in blocks]
    lax.fori_loop(0, n_chunks, sel_step, 0)

    for i in blocks:
        o_sel = acc_ref[i] / jnp.maximum(l_ref[i], 1e-30)
        mixed = [partial[i][hq] + gcol[i][3 * hq + 1] * head_rows(o_sel, hq) for hq in range(NSA_HEADS)]
        for pair in range(NSA_HEADS // 2):
            a, b = mixed[2 * pair], mixed[2 * pair + 1]
            if 2 * pair // NSA_GROUP == 0:
                b = pltpu.roll(b, NSA_DH, axis=1)
            else:
                a = pltpu.roll(a, NSA_DH, axis=1)
            o_ref[0, i * qb:(i + 1) * qb, pair * LANES:(pair + 1) * LANES] = jnp.where(lane < NSA_DH, a, b)


def _nsa_attention(q, sm, kc, vc, kv):
    b, s, qw = q.shape
    tq = NSA_STEP_BLOCKS * SEL_BLOCK
    ncmp = kc.shape[1]
    onehot = (jnp.arange(s)[:, None] // SEL_BLOCK == jnp.arange(LANES)[None, :]).astype(BF16)
    kv_spec = lambda col: pl.BlockSpec((1, s, LANES), lambda i, j: (i, 0, col))
    return pl.pallas_call(
        _nsa_kernel,
        grid=(b, s // tq),
        in_specs=[pl.BlockSpec((1, tq, qw), lambda i, j: (i, j, 0)),
                  pl.BlockSpec((1, tq, LANES), lambda i, j: (i, j, 0)),
                  pl.BlockSpec((1, ncmp, LANES), lambda i, j: (i, 0, 0)),
                  pl.BlockSpec((1, ncmp, LANES), lambda i, j: (i, 0, 0)),
                  kv_spec(2), kv_spec(3), kv_spec(4), kv_spec(5), _const_spec((s, LANES))],
        out_specs=pl.BlockSpec((1, tq, NSA_WIDTH), lambda i, j: (i, j, 0)),
        out_shape=jax.ShapeDtypeStruct((b, s, NSA_WIDTH), F32),
        scratch_shapes=[pltpu.VMEM((NSA_STEP_BLOCKS, NSA_HEADS * SEL_BLOCK, LANES), F32)] * 3,
        compiler_params=_params("parallel", "arbitrary"),
        name="nsa_attention",
    )(q, sm, kc, vc, kv, kv, kv, kv, onehot)


def _unit_lower_inverses(mats, r, c):
    n = mats[0].shape[0]
    eye = jnp.where(r == c, 1.0, 0.0)
    blk16 = (r >> 4) == (c >> 4)
    each = lambda f, *lists: [f(*args) for args in zip(*lists)]
    mm = lambda x, y: _dot(x.astype(BF16), y.astype(BF16))
    stacked = lambda top, rhs: mm(jnp.concatenate([top, rhs], axis=0), rhs)
    n1 = each(lambda a: jnp.where(blk16, -a, 0.0), mats)
    n2 = each(mm, n1, n1)
    t = each(lambda m: eye + m, n1)
    x = each(stacked, t, n2)
    t, n4 = each(lambda ti, xi: ti + xi[:n], t, x), each(lambda xi: xi[n:], x)
    x = each(stacked, t, n4)
    t, n8 = each(lambda ti, xi: ti + xi[:n], t, x), each(lambda xi: xi[n:], x)
    t = each(lambda ti, m: ti + mm(ti, m), t, n8)
    v = each(lambda a, ti: mm(jnp.where(blk16, 0.0, a), ti), mats, t)
    x = each(stacked, t, v)
    p, v2 = each(lambda ti, xi: ti - xi[:n], t, x), each(lambda xi: xi[n:], x)
    return each(lambda pi, m: pi + mm(pi, m), p, v2)


def _log_decay(sm, alog_ref, dtb_ref):
    return -jnp.exp(alog_ref[...]) * jax.nn.softplus(sm + dtb_ref[...])


def _gdn_prep_kernel(x_ref, halo_ref, cw_ref, sm_ref, alog_ref, dtb_ref,
                     qd_ref, kdt_ref, at_ref, u_ref, w_ref, gcum_ref):
    n = x_ref.shape[1]
    width = cw_ref.shape[0]
    has_history = jnp.where(pl.program_id(1) > 0, 1.0, 0.0)

    def conv_silu(c0):
        cols = slice(c0, c0 + GDN_DH)
        xe = jnp.concatenate([halo_ref[0, :, cols] * has_history, x_ref[0, :, cols]], axis=0)
        y = xe[GDN_HALO:] * cw_ref[width - 1:width, cols]
        for k in range(width - 1):
            y = y + pltpu.roll(xe, width - 1 - k, axis=0)[GDN_HALO:] * cw_ref[k:k + 1, cols]
        return jax.nn.silu(y)

    def l2norm(y):
        return y * lax.rsqrt(jnp.sum(y * y, axis=-1, keepdims=True) + EPS)

    r = lax.broadcasted_iota(jnp.int32, (n, n), 0)
    c = lax.broadcasted_iota(jnp.int32, (n, n), 1)
    in_chunk = (r >> GDN_CHUNK_SHIFT) == (c >> GDN_CHUNK_SHIFT)
    causal = in_chunk & (r >= c)
    hi = lax.Precision.HIGHEST
    sm = sm_ref[0]
    beta = jax.nn.sigmoid(sm)
    g = _log_decay(sm, alog_ref, dtb_ref)
    gcum = _dot(jnp.where(causal, 1.0, 0.0), g, precision=hi)
    gtot = _dot(jnp.where(in_chunk, 1.0, 0.0), g, precision=hi)
    gcum_ref[0] = gcum
    pick = jnp.where(lax.broadcasted_iota(jnp.int32, (8, LANES), 1)
                     == lax.broadcasted_iota(jnp.int32, (8, LANES), 0) + SM_A0, 1.0, 0.0)
    gcum_rows = _dot_nt(pick, gcum, precision=hi)

    a_mats, rhs = [], []
    for h in range(GDN_HEADS):
        cols = slice(h * GDN_DH, (h + 1) * GDN_DH)
        q = l2norm(conv_silu(h * GDN_DH)) * (GDN_DH ** -0.5)
        k = l2norm(conv_silu(GDN_WIDTH + h * GDN_DH))
        v = conv_silu(2 * GDN_WIDTH + h * GDN_DH)
        gc = _lane_col(gcum, SM_A0 + h)
        bcol = _lane_col(beta, SM_BETA0 + h)
        decay = jnp.exp(jnp.where(causal, gc - gcum_rows[h:h + 1, :], NEG_INF))
        kb = k * bcol
        egc = jnp.exp(gc)
        qd_ref[0, :, cols] = (q * egc).astype(BF16)
        kdt_ref[0, cols, :] = (k * jnp.exp(_lane_col(gtot, SM_A0 + h) - gc)).T.astype(BF16)
        grams = _dot_nt(jnp.concatenate([kb, q], axis=0).astype(BF16), k.astype(BF16))
        attn = (grams[n:] * decay).astype(BF16)
        for ch in range(n // GDN_CHUNK):
            lt = ch * GDN_CHUNK // LANES
            at_ref[0, ch * GDN_CHUNK:(ch + 1) * GDN_CHUNK, cols] = (
                attn[ch * GDN_CHUNK:(ch + 1) * GDN_CHUNK, lt * LANES:(lt + 1) * LANES])
        a_mats.append(jnp.where(r > c, grams[:n] * decay, 0.0))
        rhs.append(jnp.concatenate([v * bcol, kb * egc], axis=1).astype(BF16))
    for h, t_inv in enumerate(_unit_lower_inverses(a_mats, r, c)):
        cols = slice(h * GDN_DH, (h + 1) * GDN_DH)
        uw = _dot(t_inv.astype(BF16), rhs[h])
        u_ref[0, :, cols] = uw[:, :GDN_DH]
        w_ref[0, :, cols] = uw[:, GDN_DH:].astype(BF16)


def _gdn_prep(qkv, cw, sm, alog_vec, dtb_vec):
    b, s, width = qkv.shape
    n = GDN_PREP_TILE
    per = n // GDN_HALO
    tok = lambda w: pl.BlockSpec((1, n, w), lambda i, j: (i, j, 0))
    out = lambda dt: jax.ShapeDtypeStruct((b, s, GDN_WIDTH), dt)
    return pl.pallas_call(
        _gdn_prep_kernel,
        grid=(b, s // n),
        in_specs=[tok(width),
                  pl.BlockSpec((1, GDN_HALO, width), lambda i, j: (i, jnp.maximum(j * per - 1, 0), 0)),
                  _const_spec(cw.shape), tok(LANES), _const_spec((1, LANES)), _const_spec((1, LANES))],
        out_specs=[tok(GDN_WIDTH), pl.BlockSpec((1, GDN_WIDTH, n), lambda i, j: (i, 0, j))]
        + [tok(GDN_WIDTH)] * 3 + [tok(LANES)],
        out_shape=[out(BF16), jax.ShapeDtypeStruct((b, GDN_WIDTH, s), BF16), out(BF16), out(F32), out(BF16),
                   jax.ShapeDtypeStruct((b, s, LANES), F32)],
        compiler_params=_params("parallel", "parallel"),
        name="gdn_prep",
    )(qkv, qkv, cw, sm, alog_vec, dtb_vec)


def _gdn_scan_kernel(qd_ref, kdt_ref, at_ref, u_ref, w_ref, z_ref, gcum_ref, ng_ref, o_ref, state_ref):
    c = GDN_CHUNK
    n_chunks = qd_ref.shape[1] // c

    @pl.when(pl.program_id(1) == 0)
    def _():
        state_ref[...] = jnp.zeros_like(state_ref)

    def chunk_step(ci, _):
        t0 = pl.multiple_of(ci * c, c)
        rows = pl.ds(t0, c)
        odd = (ci & 1) == 1
        pair = pl.ds(pl.multiple_of((ci >> 1) * (2 * c), 2 * c), 2 * c)
        chains = [(bi, h) for bi in range(qd_ref.shape[0]) for h in range(GDN_HEADS)]
        cols = lambda h: slice(h * GDN_DH, (h + 1) * GDN_DH)
        states = [state_ref[bi * GDN_HEADS + h] for bi, h in chains]
        ws_qs = [_dot(jnp.concatenate([w_ref[bi, rows, cols(h)], qd_ref[bi, rows, cols(h)]], axis=0),
                      s.astype(BF16)) for (bi, h), s in zip(chains, states)]
        av_kv = []
        for (bi, h), x in zip(chains, ws_qs):
            v_new = (u_ref[bi, rows, cols(h)] - x[:c]).astype(BF16)
            zero = jnp.zeros_like(v_new)
            v_pad = jnp.concatenate([jnp.where(odd, zero, v_new), jnp.where(odd, v_new, zero)], axis=0)
            av_kv.append(_dot(jnp.concatenate([at_ref[bi, rows, cols(h)], kdt_ref[bi, cols(h), pair]], axis=0),
                              v_pad))
        for (bi, h), s, x, y in zip(chains, states, ws_qs, av_kv):
            state_decay = jnp.exp(gcum_ref[bi, pl.ds(t0 + c - 1, 1), :])
            state_ref[bi * GDN_HEADS + h] = s * _lane_col(state_decay, SM_A0 + h) + y[c:]
            o = x[c:] + y[:c]
            o = o * lax.rsqrt(jnp.mean(o * o, axis=-1, keepdims=True) + EPS) * ng_ref[...]
            o_ref[bi, rows, cols(h)] = o * jax.nn.silu(z_ref[bi, rows, cols(h)])
        return 0

    lax.fori_loop(0, n_chunks, chunk_step, 0)


def _gdn_scan(qd, kdt, at, u, w, z, gcum, norm_g):
    b, s, _ = qd.shape
    ts = GDN_SEQ_TILE
    nb = GDN_SCAN_BATCH if b % GDN_SCAN_BATCH == 0 else 1
    tok = lambda wd: pl.BlockSpec((nb, ts, wd), lambda i, j: (i, j, 0))
    return pl.pallas_call(
        _gdn_scan_kernel,
        grid=(b // nb, s // ts),
        in_specs=[tok(GDN_WIDTH), pl.BlockSpec((nb, GDN_WIDTH, ts), lambda i, j: (i, 0, j))]
        + [tok(GDN_WIDTH)] * 4 + [tok(LANES), _const_spec((1, GDN_DH))],
        out_specs=tok(GDN_WIDTH),
        out_shape=jax.ShapeDtypeStruct((b, s, GDN_WIDTH), F32),
        scratch_shapes=[pltpu.VMEM((nb * GDN_HEADS, GDN_DH, GDN_DH), F32)],
        compiler_params=_params("parallel", "arbitrary"),
        name="gdn_scan",
    )(qd, kdt, at, u, w, z, gcum, norm_g)


def _outproj_kernel(on_ref, og_ref, x_ref, g_ref, wa_ref, wb_ref, o_ref):
    on = on_ref[...]
    ms = jnp.mean(on * on, axis=-1, keepdims=True)
    hn = (on * lax.rsqrt(ms + EPS) * g_ref[...]).astype(BF16)
    o_ref[...] = x_ref[...] + _dot(hn, wa_ref[...]) + _dot(og_ref[...].astype(BF16), wb_ref[...])


def _outproj(o_nsa, o_gdn, x2, g, wa, wb):
    n, d = x2.shape
    tm = ROW_TILE
    row = lambda w: pl.BlockSpec((tm, w), lambda i: (i, 0))
    return pl.pallas_call(
        _outproj_kernel,
        grid=(n // tm,),
        in_specs=[row(o_nsa.shape[1]), row(o_gdn.shape[1]), row(d),
                  _const_spec(g.shape), _const_spec(wa.shape), _const_spec(wb.shape)],
        out_specs=row(d),
        out_shape=jax.ShapeDtypeStruct((n, d), F32),
        compiler_params=_params("parallel"),
        name="outproj",
    )(o_nsa, o_gdn, x2, g, wa, wb)


def _ffn_kernel(x_ref, halo_ref, g_ref, wup_ref, cw_ref, wdn_ref, gf_ref, o_ref, *, final_norm):
    tm = x_ref.shape[1]
    d_ff = wdn_ref.shape[0]
    width = cw_ref.shape[0]
    x = x_ref[0]
    has_history = jnp.where(pl.program_id(1) > 0, 1.0, 0.0)
    xin = jnp.concatenate([halo_ref[0] * has_history, x], axis=0)
    ms = jnp.mean(xin * xin, axis=-1, keepdims=True)
    h = (xin * lax.rsqrt(ms + EPS) * g_ref[...]).astype(BF16)

    def conv(u, c0):
        y = u[FFN_HALO:] * cw_ref[width - 1:width, c0:c0 + FFN_COL_CHUNK]
        for k in range(width - 1):
            shifted = pltpu.roll(u, width - 1 - k, axis=0)[FFN_HALO:]
            y = y + shifted * cw_ref[k:k + 1, c0:c0 + FFN_COL_CHUNK]
        return y

    def up_proj(j):
        c0 = j * FFN_COL_CHUNK
        return (_dot(h, wup_ref[:, c0:c0 + FFN_COL_CHUNK]),
                _dot(h, wup_ref[:, d_ff + c0:d_ff + c0 + FFN_COL_CHUNK]))

    n_chunks = d_ff // FFN_COL_CHUNK
    y = x
    acts = []
    nxt = up_proj(0)
    for j in range(n_chunks):
        c0 = j * FFN_COL_CHUNK
        u_gate, u_up = nxt
        if j + 1 < n_chunks:
            nxt = up_proj(j + 1)
        acts.append((jax.nn.silu(conv(u_gate, c0)) * conv(u_up, d_ff + c0)).astype(BF16))
        if len(acts) == FFN_DOWN_GROUP or j + 1 == n_chunks:
            k0 = c0 + FFN_COL_CHUNK - len(acts) * FFN_COL_CHUNK
            y = y + _dot(jnp.concatenate(acts, axis=1), wdn_ref[k0:c0 + FFN_COL_CHUNK, :])
            acts = []
    if final_norm:
        y = y * lax.rsqrt(jnp.mean(y * y, axis=-1, keepdims=True) + EPS) * gf_ref[...]
    o_ref[0] = y


def _ffn(x3, g, wup, cw, wdn, g_final, final_norm):
    b, s, d = x3.shape
    tm = ROW_TILE
    per = tm // FFN_HALO
    return pl.pallas_call(
        functools.partial(_ffn_kernel, final_norm=final_norm),
        grid=(b, s // tm),
        in_specs=[pl.BlockSpec((1, tm, d), lambda i, j: (i, j, 0)),
                  pl.BlockSpec((1, FFN_HALO, d), lambda i, j: (i, jnp.maximum(j * per - 1, 0), 0)),
                  _const_spec(g.shape), _const_spec(wup.shape), _const_spec(cw.shape), _const_spec(wdn.shape),
                  _const_spec(g_final.shape)],
        out_specs=pl.BlockSpec((1, tm, d), lambda i, j: (i, j, 0)),
        out_shape=jax.ShapeDtypeStruct((b, s, d), F32),
        compiler_params=_params("parallel", "parallel"),
        name="convffn",
    )(x3, x3, g, wup, cw, wdn, g_final)


def _split_in_weights(w_in):
    d = w_in.shape[0]
    o = 0
    wq = w_in[:, o:o + NSA_WIDTH]; o += NSA_WIDTH
    wkv = w_in[:, o:o + 6 * NSA_KV_WIDTH]; o += 6 * NSA_KV_WIDTH
    wgate = w_in[:, o:o + 3 * NSA_HEADS]; o += 3 * NSA_HEADS
    wgdn = w_in[:, o:o + 3 * GDN_WIDTH]; o += 3 * GDN_WIDTH
    wz = w_in[:, o:o + GDN_WIDTH]; o += GDN_WIDTH
    wb = w_in[:, o:o + GDN_HEADS]; o += GDN_HEADS
    wa = w_in[:, o:o + GDN_HEADS]
    wq = wq.reshape(d, NSA_HEADS, NSA_DH) * (NSA_DH ** -0.5 * LOG2_E)
    zero = jnp.zeros_like(wq)
    kv_head = (jnp.arange(NSA_HEADS) // NSA_GROUP)[None, :, None]
    wq_pad = jnp.concatenate([jnp.where(kv_head == 0, wq, zero), jnp.where(kv_head == 1, wq, zero)], axis=-1)
    wq_pad = wq_pad.reshape(d, NSA_HEADS * LANES)
    wsm = jnp.concatenate([wgate, wb, wa, jnp.zeros((d, LANES - SM_A0 - GDN_HEADS), w_in.dtype)], axis=-1)
    return tuple(w.astype(BF16) for w in (wq_pad, wkv, wgdn, wz, wsm))


def _lane_vec(v, offset):
    return jnp.zeros((1, LANES), F32).at[0, offset:offset + v.shape[0]].set(v.astype(F32))


def _head_padded_w2(w2):
    z = jnp.zeros_like(w2)
    return jnp.stack([jnp.concatenate([w2, z], axis=-1), jnp.concatenate([z, w2], axis=-1)]).astype(BF16)


def kernel(x, norm_mix, w_in, cmp_pos_k, cmp_pos_v, cmp_k_w1, cmp_k_b1, cmp_k_w2, cmp_v_w1, cmp_v_b1, cmp_v_w2,
           nsa_norm, gdn_conv, gdn_a_log, gdn_dt_bias, gdn_norm, w_out, norm_ffn, ffn_up, ffn_conv, ffn_down,
           norm_final):
    b, s, d = x.shape
    depth = w_in.shape[0]
    group = CMP_BLOCK // 2
    assert CMP_STRIDE == group and s % GDN_SEQ_TILE == 0 and s >= WIN_KEYS and (b * s) % ROW_TILE == 0
    x2 = x.reshape(b * s, d)
    for l in range(depth):
        wq, wkv, wgdn, wz, wsm = _split_in_weights(w_in[l])
        q, kv, gdn, z, sm = _inproj(x2, norm_mix[l][None, :], wq, wkv, wgdn, wz, wsm)
        q, kv, gdn, z, sm = (a.reshape(b, s, a.shape[-1]) for a in (q, kv, gdn, z, sm))

        groups = kv[:, :, :2 * NSA_KV_WIDTH].reshape(b, s, 2 * NSA_KV_HEADS, NSA_DH)
        groups = groups.transpose(0, 2, 1, 3).reshape(b, 2 * NSA_KV_HEADS, s // group, group * NSA_DH)
        kc, vc = _compress(
            groups, cmp_pos_k[l].reshape(1, -1), cmp_pos_v[l].reshape(1, -1),
            cmp_k_w1[l].astype(BF16), cmp_k_b1[l][None, :], _head_padded_w2(cmp_k_w2[l]),
            cmp_v_w1[l].astype(BF16), cmp_v_b1[l][None, :], _head_padded_w2(cmp_v_w2[l]))
        o_nsa = _nsa_attention(q, sm, kc, vc, kv)

        alog_vec, dtb_vec = _lane_vec(gdn_a_log[l], SM_A0), _lane_vec(gdn_dt_bias[l], SM_A0)
        qd, kdt, at, gu, gw, gcum = _gdn_prep(gdn, gdn_conv[l], sm, alog_vec, dtb_vec)
        o_gdn = _gdn_scan(qd, kdt, at, gu, gw, z, gcum, gdn_norm[l][None, :])

        w_o = w_out[l].astype(BF16)
        x2 = _outproj(o_nsa.reshape(b * s, -1), o_gdn.reshape(b * s, -1), x2, nsa_norm[l][None, :],
                      w_o[:NSA_WIDTH], w_o[NSA_WIDTH:])
        x2 = _ffn(x2.reshape(b, s, d), norm_ffn[l][None, :], ffn_up[l].astype(BF16), ffn_conv[l],
                  ffn_down[l].astype(BF16), norm_final[None, :], final_norm=(l == depth - 1)).reshape(b * s, d)
    return x2.reshape(b, s, d)
```

```python
import functools

import jax
import jax.numpy as jnp
from jax import lax
from jax.experimental import pallas as pl
from jax.experimental.pallas import tpu as pltpu

F32 = jnp.float32
BF16 = jnp.bfloat16
EPS = 1e-6
NEG_INF = float("-inf")
MASK_BIAS = -1e30
LOG2_E = 1.4426950408889634

NSA_HEADS = 8
NSA_KV_HEADS = 2
NSA_GROUP = NSA_HEADS // NSA_KV_HEADS
NSA_DH = 64
NSA_WIDTH = NSA_HEADS * NSA_DH
NSA_KV_WIDTH = NSA_KV_HEADS * NSA_DH
CMP_BLOCK = 32
CMP_STRIDE = 16
SEL_BLOCK = 64
N_SELECT = 8
WINDOW = 512
GDN_HEADS = 4
GDN_DH = 128
GDN_WIDTH = GDN_HEADS * GDN_DH
GDN_CHUNK = 64
GDN_CHUNK_SHIFT = GDN_CHUNK.bit_length() - 1

LANES = 128
VMEM_LIMIT_BYTES = 56 * 1024 * 1024

ROW_TILE = 512
FFN_HALO = 8
FFN_COL_CHUNK = 256
FFN_DOWN_GROUP = 4
SEL_KEY_CHUNK = 512
NSA_STEP_BLOCKS = 2
WIN_KEYS = WINDOW + 2 * SEL_BLOCK
GDN_SEQ_TILE = 256
GDN_SCAN_BATCH = 4
GDN_PREP_TILE = 256
GDN_HALO = 8

SM_GATE0 = 0
SM_BETA0 = 3 * NSA_HEADS
SM_A0 = SM_BETA0 + GDN_HEADS


def _dot(a, b, precision=None):
    return jnp.dot(a, b, preferred_element_type=F32, precision=precision)


def _dot_nt(a, b, precision=None):
    return lax.dot_general(a, b, (((1,), (1,)), ((), ())), preferred_element_type=F32, precision=precision)


def _dot_tn(a, b, precision=None):
    return lax.dot_general(a, b, (((0,), (0,)), ((), ())), preferred_element_type=F32, precision=precision)


def _lane_col(x, idx):
    lane = lax.broadcasted_iota(jnp.int32, x.shape, 1)
    return jnp.sum(jnp.where(lane == idx, x, 0.0), axis=-1, keepdims=True)


def _const_spec(shape):
    zeros = (0,) * len(shape)
    return pl.BlockSpec(shape, lambda *_: zeros, pipeline_mode=pl.Buffered(1))


def _params(*semantics):
    return pltpu.CompilerParams(dimension_semantics=semantics, vmem_limit_bytes=VMEM_LIMIT_BYTES)


def _inproj_kernel(x_ref, g_ref, wq_ref, wkv_ref, wgdn_ref, wz_ref, wsm_ref,
                   q_ref, kv_ref, gdn_ref, z_ref, sm_ref):
    x = x_ref[...]
    ms = jnp.mean(x * x, axis=-1, keepdims=True)
    h = (x * lax.rsqrt(ms + EPS) * g_ref[...]).astype(BF16)
    q_ref[...] = _dot(h, wq_ref[...]).astype(BF16)
    kv_ref[...] = _dot(h, wkv_ref[...]).astype(BF16)
    gdn_ref[...] = _dot(h, wgdn_ref[...])
    z_ref[...] = _dot(h, wz_ref[...])
    sm_ref[...] = _dot(h, wsm_ref[...])


def _inproj(x2, g, wq, wkv, wgdn, wz, wsm):
    n, d = x2.shape
    tm = ROW_TILE
    row = lambda w: pl.BlockSpec((tm, w), lambda i: (i, 0))
    widths = (wq.shape[1], wkv.shape[1], wgdn.shape[1], wz.shape[1], wsm.shape[1])
    dtypes = (BF16, BF16, F32, F32, F32)
    return pl.pallas_call(
        _inproj_kernel,
        grid=(n // tm,),
        in_specs=[row(d), _const_spec((1, d))] + [_const_spec(w.shape) for w in (wq, wkv, wgdn, wz, wsm)],
        out_specs=[row(w) for w in widths],
        out_shape=[jax.ShapeDtypeStruct((n, w), dt) for w, dt in zip(widths, dtypes)],
        compiler_params=_params("parallel"),
        name="inproj",
    )(x2, g, wq, wkv, wgdn, wz, wsm)


def _compress_kernel(g_ref, posk_ref, posv_ref, w1k_ref, b1k_ref, w2k_ref, w1v_ref, b1v_ref, w2v_ref,
                     kc_ref, vc_ref):
    ng = g_ref.shape[2]
    half = w1k_ref.shape[0] // 2
    row = lax.broadcasted_iota(jnp.int32, (ng, 1), 0)
    for sel, (pos_ref, w1_ref, b1_ref, w2_ref, out_ref) in enumerate(
            ((posk_ref, w1k_ref, b1k_ref, w2k_ref, kc_ref), (posv_ref, w1v_ref, b1v_ref, w2v_ref, vc_ref))):
        w1 = w1_ref[...]
        pos8 = jnp.broadcast_to(pos_ref[...], (8, pos_ref.shape[1])).astype(BF16)
        bias = _dot(pos8, w1)[0:1, :] + b1_ref[...]
        acc = jnp.zeros((ng, LANES), F32)
        for h in range(NSA_KV_HEADS):
            grp = g_ref[0, sel * NSA_KV_HEADS + h]
            top = _dot(grp, w1[:half])
            bot = _dot(grp, w1[half:])
            hid = top + pltpu.roll(bot, ng - 1, axis=0) + bias
            act = jax.nn.gelu(hid, approximate=True).astype(BF16)
            acc = acc + _dot(act, w2_ref[h])
        out_ref[0] = jnp.where(row < ng - 1, acc, 0.0).astype(BF16)


def _compress(groups, posk, posv, w1k, b1k, w2k, w1v, b1v, w2v):
    b, _, ng, gw = groups.shape
    out = jax.ShapeDtypeStruct((b, ng, LANES), BF16)
    return pl.pallas_call(
        _compress_kernel,
        grid=(b,),
        in_specs=[pl.BlockSpec((1, 2 * NSA_KV_HEADS, ng, gw), lambda i: (i, 0, 0, 0))]
        + [_const_spec(a.shape) for a in (posk, posv, w1k, b1k, w2k, w1v, b1v, w2v)],
        out_specs=[pl.BlockSpec((1, ng, LANES), lambda i: (i, 0, 0))] * 2,
        out_shape=[out, out],
        compiler_params=_params("parallel"),
        name="nsa_compress",
    )(groups, posk, posv, w1k, b1k, w2k, w1v, b1v, w2v)


def _row_reduce(x, combine, reduce):
    acc = x[:, :LANES]
    for t in range(1, x.shape[1] // LANES):
        acc = combine(acc, x[:, t * LANES:(t + 1) * LANES])
    return reduce(acc, axis=-1, keepdims=True)


def _row_max(x):
    return _row_reduce(x, jnp.maximum, jnp.max)


def _row_sum(x):
    return _row_reduce(x, jnp.add, jnp.sum)


def _softmax_parts(s, mask):
    sm = jnp.where(mask, s, NEG_INF)
    m = jnp.max(sm, axis=-1, keepdims=True)
    m = jnp.where(jnp.isfinite(m), m, 0.0)
    e = jnp.where(mask, jnp.exp2(sm - m), 0.0)
    return e, jnp.sum(e, axis=-1, keepdims=True)


def _nsa_kernel(q_ref, sm_ref, kc_ref, vc_ref, ks_ref, vs_ref, kw_ref, vw_ref, oh_ref, o_ref,
                m_ref, l_ref, acc_ref):
    qb = SEL_BLOCK
    nq = q_ref.shape[1] // qb
    seq = ks_ref.shape[1]
    ncmp = kc_ref.shape[1]
    nsb = seq // SEL_BLOCK
    rows = NSA_HEADS * qb
    both = NSA_KV_HEADS * qb
    blocks = range(nq)
    each = lambda f, *lists: [f(*args) for args in zip(*lists)]
    blk = [pl.program_id(1) * nq + i for i in blocks]
    s0 = [b * qb for b in blk]

    lane = lax.broadcasted_iota(jnp.int32, (qb, LANES), 1)
    t_q = [s + lax.broadcasted_iota(jnp.int32, (qb, 1), 0) for s in s0]
    per_head = lambda x: jnp.concatenate([x] * NSA_HEADS, axis=0)
    jn = lax.broadcasted_iota(jnp.int32, (nsb, ncmp), 0)
    nn = lax.broadcasted_iota(jnp.int32, (nsb, ncmp), 1)
    overlap_t = jnp.where((nn * CMP_STRIDE < jn * SEL_BLOCK + SEL_BLOCK)
                          & (jn * SEL_BLOCK < nn * CMP_STRIDE + CMP_BLOCK) & (nn < ncmp - 1), 1.0, 0.0)
    n_cmp_idx = lax.broadcasted_iota(jnp.int32, (1, ncmp), 1)
    jj = lax.broadcasted_iota(jnp.int32, (nsb, both), 0)
    eye = jnp.where(lax.broadcasted_iota(jnp.int32, (both, both), 0)
                    == lax.broadcasted_iota(jnp.int32, (both, both), 1), 1.0, 0.0).astype(BF16)

    gates = [jax.nn.sigmoid(sm_ref[0, i * qb:(i + 1) * qb]) for i in blocks]
    gcol = [[_lane_col(g, SM_GATE0 + j) for j in range(3 * NSA_HEADS)] for g in gates]

    qs = [jnp.concatenate([q_ref[0, i * qb:(i + 1) * qb, hq * LANES:(hq + 1) * LANES]
                           for hq in range(NSA_HEADS)], axis=0) for i in blocks]
    cmp_mask = [per_head(jnp.where((n_cmp_idx * CMP_STRIDE + (CMP_BLOCK - 1) <= t) & (n_cmp_idx < ncmp - 1),
                                   1.0, 0.0)) > 0.5 for t in t_q]
    d_lo = [pl.multiple_of(jnp.maximum(s - SEL_BLOCK, 0), SEL_BLOCK) for s in s0]
    diag_bias = [per_head(jnp.where(d + lax.broadcasted_iota(jnp.int32, (1, 2 * SEL_BLOCK), 1) <= t,
                                    0.0, MASK_BIAS)) for d, t in zip(d_lo, t_q)]
    w_lo = [pl.multiple_of(jnp.minimum(jnp.maximum(s - WINDOW, 0), seq - WIN_KEYS), SEL_BLOCK) for s in s0]
    kpos_w = [w + lax.broadcasted_iota(jnp.int32, (1, WIN_KEYS), 1) for w in w_lo]
    win_bias = [per_head(jnp.where((k <= t) & (k > t - WINDOW), 0.0, MASK_BIAS)) for k, t in zip(kpos_w, t_q)]

    s_c = [_dot_nt(q, kc_ref[0]) for q in qs]
    s_w = [_dot_nt(q, kw_ref[0, pl.ds(w, WIN_KEYS), :]) + b for q, w, b in zip(qs, w_lo, win_bias)]
    s_d = [_dot_nt(q, ks_ref[0, pl.ds(d, 2 * SEL_BLOCK), :]) + b for q, d, b in zip(qs, d_lo, diag_bias)]

    parts = each(_softmax_parts, s_c, cmp_mask)
    p_cmp = [e / jnp.maximum(l, 1e-30) for e, l in parts]
    o_cmp = [_dot(p.astype(BF16), vc_ref[0]) for p in p_cmp]

    def group_sums(p):
        out = []
        for h in range(NSA_KV_HEADS):
            acc = p[h * NSA_GROUP * qb:(h * NSA_GROUP + 1) * qb]
            for g in range(1, NSA_GROUP):
                acc = acc + p[(h * NSA_GROUP + g) * qb:(h * NSA_GROUP + g + 1) * qb]
            out.append(acc)
        return jnp.concatenate(out, axis=0)

    imp_t = [_dot_nt(overlap_t, group_sums(p), precision=lax.Precision.HIGHEST) for p in p_cmp]

    def selection_bias(imp, b):
        forced = (jj == 0) | (jj == b) | (jj == b - 1)
        score = jnp.where(forced, jnp.inf, jnp.where(jj <= b, imp, NEG_INF))
        rank = jnp.zeros((nsb, both), F32)
        for j2 in range(nsb):
            other = score[j2:j2 + 1, :]
            ahead = (other > score) | ((other == score) & (jj > j2))
            rank = rank + jnp.where(ahead, 1.0, 0.0)
        chosen = (rank < N_SELECT) & (score > NEG_INF) & (jj < b - 1)
        bias_t = jnp.concatenate([jnp.where(chosen, 0.0, MASK_BIAS), jnp.zeros((LANES - nsb, both), F32)], axis=0)
        return bias_t.astype(BF16)

    bias_t = each(selection_bias, imp_t, blk)
    bias_q = [_dot_nt(eye, b).astype(BF16) for b in bias_t]

    e_w = [jnp.exp2(s - _row_max(s)) for s in s_w]
    o_win = [_dot(e.astype(BF16), vw_ref[0, pl.ds(w, WIN_KEYS), :]) / jnp.maximum(_row_sum(e), 1e-30)
             for e, w in zip(e_w, w_lo)]

    m0 = [jnp.max(s, axis=-1, keepdims=True) for s in s_d]
    p0 = [jnp.exp2(s - m) for s, m in zip(s_d, m0)]
    for i in blocks:
        m_ref[i] = jnp.broadcast_to(m0[i], m_ref.shape[1:])
        l_ref[i] = jnp.broadcast_to(jnp.sum(p0[i], axis=-1, keepdims=True), l_ref.shape[1:])
        acc_ref[i] = _dot(p0[i].astype(BF16), vs_ref[0, pl.ds(d_lo[i], 2 * SEL_BLOCK), :])
    q_aug = [jnp.concatenate([q, jnp.concatenate([b[h * qb:(h + 1) * qb] for h in range(NSA_KV_HEADS)
                                                  for _ in range(NSA_GROUP)], axis=0)], axis=1)
             for q, b in zip(qs, bias_q)]
    n_chunks = (blk[-1] + SEL_KEY_CHUNK // SEL_BLOCK - 2) // (SEL_KEY_CHUNK // SEL_BLOCK)

    def sel_step(c, _):
        k0 = pl.multiple_of(c * SEL_KEY_CHUNK, SEL_KEY_CHUNK)
        k_aug = jnp.concatenate([ks_ref[0, pl.ds(k0, SEL_KEY_CHUNK), :], oh_ref[pl.ds(k0, SEL_KEY_CHUNK), :]],
                                axis=1)
        v_blk = vs_ref[0, pl.ds(k0, SEL_KEY_CHUNK), :]
        s = [_dot_nt(q, k_aug) for q in q_aug]
        for i in blocks:
            m_prev = m_ref[i]
            m_new = jnp.maximum(m_prev, _row_max(s[i]))
            p = jnp.exp2(s[i] - jnp.concatenate([m_new] * (SEL_KEY_CHUNK // LANES), axis=1))
            alpha = jnp.exp2(m_prev - m_new)
            m_ref[i] = m_new
            l_ref[i] = alpha * l_ref[i] + _row_sum(p)
            acc_ref[i] = alpha * acc_ref[i] + _dot(p.astype(BF16), v_blk)
        return 0

    head_rows = lambda x, hq: x[hq * qb:(hq + 1) * qb]
    partial = [[gcol[i][3 * hq] * head_rows(o_cmp[i], hq) + gcol[i][3 * hq + 2] * head_rows(o_win[i], hq)
                for hq in range(NSA_HEADS)] for i in blocks]
    lax.fori_loop(0, n_chunks, sel_step, 0)

    for i in blocks:
        o_sel = acc_ref[i] / jnp.maximum(l_ref[i], 1e-30)
        mixed = [partial[i][hq] + gcol[i][3 * hq + 1] * head_rows(o_sel, hq) for hq in range(NSA_HEADS)]
        for pair in range(NSA_HEADS // 2):
            a, b = mixed[2 * pair], mixed[2 * pair + 1]
            if 2 * pair // NSA_GROUP == 0:
                b = pltpu.roll(b, NSA_DH, axis=1)
            else:
                a = pltpu.roll(a, NSA_DH, axis=1)
            o_ref[0, i * qb:(i + 1) * qb, pair * LANES:(pair + 1) * LANES] = jnp.where(lane < NSA_DH, a, b)


def _nsa_attention(q, sm, kc, vc, kv):
    b, s, qw = q.shape
    tq = NSA_STEP_BLOCKS * SEL_BLOCK
    ncmp = kc.shape[1]
    onehot = (jnp.arange(s)[:, None] // SEL_BLOCK == jnp.arange(LANES)[None, :]).astype(BF16)
    kv_spec = lambda col: pl.BlockSpec((1, s, LANES), lambda i, j: (i, 0, col))
    return pl.pallas_call(
        _nsa_kernel,
        grid=(b, s // tq),
        in_specs=[pl.BlockSpec((1, tq, qw), lambda i, j: (i, j, 0)),
                  pl.BlockSpec((1, tq, LANES), lambda i, j: (i, j, 0)),
                  pl.BlockSpec((1, ncmp, LANES), lambda i, j: (i, 0, 0)),
                  pl.BlockSpec((1, ncmp, LANES), lambda i, j: (i, 0, 0)),
                  kv_spec(2), kv_spec(3), kv_spec(4), kv_spec(5), _const_spec((s, LANES))],
        out_specs=pl.BlockSpec((1, tq, NSA_WIDTH), lambda i, j: (i, j, 0)),
        out_shape=jax.ShapeDtypeStruct((b, s, NSA_WIDTH), F32),
        scratch_shapes=[pltpu.VMEM((NSA_STEP_BLOCKS, NSA_HEADS * SEL_BLOCK, LANES), F32)] * 3,
        compiler_params=_params("parallel", "arbitrary"),
        name="nsa_attention",
    )(q, sm, kc, vc, kv, kv, kv, kv, onehot)


def _unit_lower_inverses(mats, r, c):
    n = mats[0].shape[0]
    eye = jnp.where(r == c, 1.0, 0.0)
    blk16 = (r >> 4) == (c >> 4)
    each = lambda f, *lists: [f(*args) for args in zip(*lists)]
    mm = lambda x, y: _dot(x.astype(BF16), y.astype(BF16))
    stacked = lambda top, rhs: mm(jnp.concatenate([top, rhs], axis=0), rhs)
    n1 = each(lambda a: jnp.where(blk16, -a, 0.0), mats)
    n2 = each(mm, n1, n1)
    t = each(lambda m: eye + m, n1)
    x = each(stacked, t, n2)
    t, n4 = each(lambda ti, xi: ti + xi[:n], t, x), each(lambda xi: xi[n:], x)
    x = each(stacked, t, n4)
    t, n8 = each(lambda ti, xi: ti + xi[:n], t, x), each(lambda xi: xi[n:], x)
    t = each(lambda ti, m: ti + mm(ti, m), t, n8)
    v = each(lambda a, ti: mm(jnp.where(blk16, 0.0, a), ti), mats, t)
    x = each(stacked, t, v)
    p, v2 = each(lambda ti, xi: ti - xi[:n], t, x), each(lambda xi: xi[n:], x)
    return each(lambda pi, m: pi + mm(pi, m), p, v2)


def _log_decay(sm, alog_ref, dtb_ref):
    return -jnp.exp(alog_ref[...]) * jax.nn.softplus(sm + dtb_ref[...])


def _gdn_prep_kernel(x_ref, halo_ref, cw_ref, sm_ref, alog_ref, dtb_ref,
                     qd_ref, kdt_ref, at_ref, u_ref, w_ref, gcum_ref):
    n = x_ref.shape[1]
    width = cw_ref.shape[0]
    has_history = jnp.where(pl.program_id(1) > 0, 1.0, 0.0)

    def head_cols(c0):
        cols = slice(c0, c0 + GDN_DH)
        xe = jnp.concatenate([halo_ref[0, :, cols] * has_history, x_ref[0, :, cols]], axis=0)
        y = xe[GDN_HALO:] * cw_ref[width - 1:width, cols]
        for k in range(width - 1):
            y = y + pltpu.roll(xe, width - 1 - k, axis=0)[GDN_HALO:] * cw_ref[k:k + 1, cols]
        return jax.nn.silu(y)

    def l2norm(y):
        return y * lax.rsqrt(jnp.sum(y * y, axis=-1, keepdims=True) + EPS)

    r = lax.broadcasted_iota(jnp.int32, (n, n), 0)
    c = lax.broadcasted_iota(jnp.int32, (n, n), 1)
    in_chunk = (r >> GDN_CHUNK_SHIFT) == (c >> GDN_CHUNK_SHIFT)
    causal = in_chunk & (r >= c)
    hi = lax.Precision.HIGHEST
    sm = sm_ref[0]
    beta = jax.nn.sigmoid(sm)
    g = _log_decay(sm, alog_ref, dtb_ref)
    gcum = _dot(jnp.where(causal, 1.0, 0.0), g, precision=hi)
    gtot = _dot(jnp.where(in_chunk, 1.0, 0.0), g, precision=hi)
    gcum_ref[0] = gcum
    pick = jnp.where(lax.broadcasted_iota(jnp.int32, (8, LANES), 1)
                     == lax.broadcasted_iota(jnp.int32, (8, LANES), 0) + SM_A0, 1.0, 0.0)
    gcum_rows = _dot_nt(pick, gcum, precision=hi)

    a_mats, rhs = [], []
    for h in range(GDN_HEADS):
        cols = slice(h * GDN_DH, (h + 1) * GDN_DH)
        q = l2norm(head_cols(h * GDN_DH)) * (GDN_DH ** -0.5)
        k = l2norm(head_cols(GDN_WIDTH + h * GDN_DH))
        v = head_cols(2 * GDN_WIDTH + h * GDN_DH)
        gc = _lane_col(gcum, SM_A0 + h)
        bcol = _lane_col(beta, SM_BETA0 + h)
        decay = jnp.exp(jnp.where(causal, gc - gcum_rows[h:h + 1, :], NEG_INF))
        kb = k * bcol
        egc = jnp.exp(gc)
        qd_ref[0, :, cols] = (q * egc).astype(BF16)
        kdt_ref[0, cols, :] = (k * jnp.exp(_lane_col(gtot, SM_A0 + h) - gc)).T.astype(BF16)
        grams = _dot_nt(jnp.concatenate([kb, q], axis=0).astype(BF16), k.astype(BF16))
        attn = (grams[n:] * decay).astype(BF16)
        for ch in range(n // GDN_CHUNK):
            lt = ch * GDN_CHUNK // LANES
            at_ref[0, ch * GDN_CHUNK:(ch + 1) * GDN_CHUNK, cols] = (
                attn[ch * GDN_CHUNK:(ch + 1) * GDN_CHUNK, lt * LANES:(lt + 1) * LANES])
        a_mats.append(jnp.where(r > c, grams[:n] * decay, 0.0))
        rhs.append(jnp.concatenate([v * bcol, kb * egc], axis=1).astype(BF16))
    for h, t_inv in enumerate(_unit_lower_inverses(a_mats, r, c)):
        cols = slice(h * GDN_DH, (h + 1) * GDN_DH)
        uw = _dot(t_inv.astype(BF16), rhs[h])
        u_ref[0, :, cols] = uw[:, :GDN_DH]
        w_ref[0, :, cols] = uw[:, GDN_DH:].astype(BF16)


def _gdn_prep(qkv, cw, sm, alog_vec, dtb_vec):
    b, s, width = qkv.shape
    n = GDN_PREP_TILE
    per = n // GDN_HALO
    tok = lambda w: pl.BlockSpec((1, n, w), lambda i, j: (i, j, 0))
    out = lambda dt: jax.ShapeDtypeStruct((b, s, GDN_WIDTH), dt)
    return pl.pallas_call(
        _gdn_prep_kernel,
        grid=(b, s // n),
        in_specs=[tok(width),
                  pl.BlockSpec((1, GDN_HALO, width), lambda i, j: (i, jnp.maximum(j * per - 1, 0), 0)),
                  _const_spec(cw.shape), tok(LANES), _const_spec((1, LANES)), _const_spec((1, LANES))],
        out_specs=[tok(GDN_WIDTH), pl.BlockSpec((1, GDN_WIDTH, n), lambda i, j: (i, 0, j))]
        + [tok(GDN_WIDTH)] * 3 + [tok(LANES)],
        out_shape=[out(BF16), jax.ShapeDtypeStruct((b, GDN_WIDTH, s), BF16), out(BF16), out(F32), out(BF16),
                   jax.ShapeDtypeStruct((b, s, LANES), F32)],
        compiler_params=_params("parallel", "parallel"),
        name="gdn_prep",
    )(qkv, qkv, cw, sm, alog_vec, dtb_vec)


def _gdn_scan_kernel(qd_ref, kdt_ref, at_ref, u_ref, w_ref, z_ref, gcum_ref, ng_ref, o_ref, state_ref):
    c = GDN_CHUNK
    n_chunks = qd_ref.shape[1] // c

    @pl.when(pl.program_id(1) == 0)
    def _():
        state_ref[...] = jnp.zeros_like(state_ref)

    def chunk_step(ci, _):
        t0 = pl.multiple_of(ci * c, c)
        rows = pl.ds(t0, c)
        odd = (ci & 1) == 1
        pair = pl.ds(pl.multiple_of((ci >> 1) * (2 * c), 2 * c), 2 * c)
        chains = [(bi, h) for bi in range(qd_ref.shape[0]) for h in range(GDN_HEADS)]
        cols = lambda h: slice(h * GDN_DH, (h + 1) * GDN_DH)
        states = [state_ref[bi * GDN_HEADS + h] for bi, h in chains]
        ws_qs = [_dot(jnp.concatenate([w_ref[bi, rows, cols(h)], qd_ref[bi, rows, cols(h)]], axis=0),
                      s.astype(BF16)) for (bi, h), s in zip(chains, states)]
        av_kv = []
        for (bi, h), x in zip(chains, ws_qs):
            v_new = (u_ref[bi, rows, cols(h)] - x[:c]).astype(BF16)
            zero = jnp.zeros_like(v_new)
            v_pad = jnp.concatenate([jnp.where(odd, zero, v_new), jnp.where(odd, v_new, zero)], axis=0)
            av_kv.append(_dot(jnp.concatenate([at_ref[bi, rows, cols(h)], kdt_ref[bi, cols(h), pair]], axis=0),
                              v_pad))
        for (bi, h), s, x, y in zip(chains, states, ws_qs, av_kv):
            state_decay = jnp.exp(gcum_ref[bi, pl.ds(t0 + c - 1, 1), :])
            state_ref[bi * GDN_HEADS + h] = s * _lane_col(state_decay, SM_A0 + h) + y[c:]
            o = x[c:] + y[:c]
            o = o * lax.rsqrt(jnp.mean(o * o, axis=-1, keepdims=True) + EPS) * ng_ref[...]
            o_ref[bi, rows, cols(h)] = o * jax.nn.silu(z_ref[bi, rows, cols(h)])
        return 0

    lax.fori_loop(0, n_chunks, chunk_step, 0)


def _gdn_scan(qd, kdt, at, u, w, z, gcum, norm_g):
    b, s, _ = qd.shape
    ts = GDN_SEQ_TILE
    nb = GDN_SCAN_BATCH if b % GDN_SCAN_BATCH == 0 else 1
    tok = lambda wd: pl.BlockSpec((nb, ts, wd), lambda i, j: (i, j, 0))
    return pl.pallas_call(
        _gdn_scan_kernel,
        grid=(b // nb, s // ts),
        in_specs=[tok(GDN_WIDTH), pl.BlockSpec((nb, GDN_WIDTH, ts), lambda i, j: (i, 0, j))]
        + [tok(GDN_WIDTH)] * 4 + [tok(LANES), _const_spec((1, GDN_DH))],
        out_specs=tok(GDN_WIDTH),
        out_shape=jax.ShapeDtypeStruct((b, s, GDN_WIDTH), F32),
        scratch_shapes=[pltpu.VMEM((nb * GDN_HEADS, GDN_DH, GDN_DH), F32)],
        compiler_params=_params("parallel", "arbitrary"),
        name="gdn_scan",
    )(qd, kdt, at, u, w, z, gcum, norm_g)


def _mix_ffn_kernel(x_ref, xh_ref, on_ref, onh_ref, og_ref, ogh_ref, gn_ref, wo_ref,
                    g_ref, wup_ref, cw_ref, wdn_ref, gf_ref, o_ref, *, final_norm):
    tm = x_ref.shape[1]
    d_ff = wdn_ref.shape[0]
    width = cw_ref.shape[0]
    has_history = jnp.where(pl.program_id(1) > 0, 1.0, 0.0)
    with_halo = lambda ref, halo: jnp.concatenate([halo[0] * has_history, ref[0]], axis=0)
    on = with_halo(on_ref, onh_ref)
    hn = (on * lax.rsqrt(jnp.mean(on * on, axis=-1, keepdims=True) + EPS) * gn_ref[...]).astype(BF16)
    mixed = jnp.concatenate([hn, with_halo(og_ref, ogh_ref).astype(BF16)], axis=1)
    xin = with_halo(x_ref, xh_ref) + _dot(mixed, wo_ref[...])
    x = xin[FFN_HALO:]
    ms = jnp.mean(xin * xin, axis=-1, keepdims=True)
    h = (xin * lax.rsqrt(ms + EPS) * g_ref[...]).astype(BF16)

    def conv(u, c0):
        y = u[FFN_HALO:] * cw_ref[width - 1:width, c0:c0 + FFN_COL_CHUNK]
        for k in range(width - 1):
            shifted = pltpu.roll(u, width - 1 - k, axis=0)[FFN_HALO:]
            y = y + shifted * cw_ref[k:k + 1, c0:c0 + FFN_COL_CHUNK]
        return y

    def up_proj(j):
        c0 = j * FFN_COL_CHUNK
        return (_dot(h, wup_ref[:, c0:c0 + FFN_COL_CHUNK]),
                _dot(h, wup_ref[:, d_ff + c0:d_ff + c0 + FFN_COL_CHUNK]))

    n_chunks = d_ff // FFN_COL_CHUNK
    y = x
    acts = []
    nxt = up_proj(0)
    for j in range(n_chunks):
        c0 = j * FFN_COL_CHUNK
        u_gate, u_up = nxt
        if j + 1 < n_chunks:
            nxt = up_proj(j + 1)
        acts.append((jax.nn.silu(conv(u_gate, c0)) * conv(u_up, d_ff + c0)).astype(BF16))
        if len(acts) == FFN_DOWN_GROUP or j + 1 == n_chunks:
            k0 = c0 + FFN_COL_CHUNK - len(acts) * FFN_COL_CHUNK
            y = y + _dot(jnp.concatenate(acts, axis=1), wdn_ref[k0:c0 + FFN_COL_CHUNK, :])
            acts = []
    if final_norm:
        y = y * lax.rsqrt(jnp.mean(y * y, axis=-1, keepdims=True) + EPS) * gf_ref[...]
    o_ref[0] = y


def _mix_ffn(x3, o_nsa, o_gdn, g_nsa, wo, g, wup, cw, wdn, g_final, final_norm):
    b, s, d = x3.shape
    tm = ROW_TILE
    per = tm // FFN_HALO
    rows = lambda w: pl.BlockSpec((1, tm, w), lambda i, j: (i, j, 0))
    halo = lambda w: pl.BlockSpec((1, FFN_HALO, w), lambda i, j: (i, jnp.maximum(j * per - 1, 0), 0))
    consts = (g_nsa, wo, g, wup, cw, wdn, g_final)
    return pl.pallas_call(
        functools.partial(_mix_ffn_kernel, final_norm=final_norm),
        grid=(b, s // tm),
        in_specs=[rows(d), halo(d), rows(o_nsa.shape[2]), halo(o_nsa.shape[2]),
                  rows(o_gdn.shape[2]), halo(o_gdn.shape[2])] + [_const_spec(c.shape) for c in consts],
        out_specs=rows(d),
        out_shape=jax.ShapeDtypeStruct((b, s, d), F32),
        compiler_params=_params("parallel", "parallel"),
        name="mix_convffn",
    )(x3, x3, o_nsa, o_nsa, o_gdn, o_gdn, *consts)


def _split_in_weights(w_in):
    d = w_in.shape[0]
    o = 0
    wq = w_in[:, o:o + NSA_WIDTH]; o += NSA_WIDTH
    wkv = w_in[:, o:o + 6 * NSA_KV_WIDTH]; o += 6 * NSA_KV_WIDTH
    wgate = w_in[:, o:o + 3 * NSA_HEADS]; o += 3 * NSA_HEADS
    wgdn = w_in[:, o:o + 3 * GDN_WIDTH]; o += 3 * GDN_WIDTH
    wz = w_in[:, o:o + GDN_WIDTH]; o += GDN_WIDTH
    wb = w_in[:, o:o + GDN_HEADS]; o += GDN_HEADS
    wa = w_in[:, o:o + GDN_HEADS]
    wq = wq.reshape(d, NSA_HEADS, NSA_DH) * (NSA_DH ** -0.5 * LOG2_E)
    zero = jnp.zeros_like(wq)
    kv_head = (jnp.arange(NSA_HEADS) // NSA_GROUP)[None, :, None]
    wq_pad = jnp.concatenate([jnp.where(kv_head == 0, wq, zero), jnp.where(kv_head == 1, wq, zero)], axis=-1)
    wq_pad = wq_pad.reshape(d, NSA_HEADS * LANES)
    wsm = jnp.concatenate([wgate, wb, wa, jnp.zeros((d, LANES - SM_A0 - GDN_HEADS), w_in.dtype)], axis=-1)
    return tuple(w.astype(BF16) for w in (wq_pad, wkv, wgdn, wz, wsm))


def _lane_vec(v, offset):
    return jnp.zeros((1, LANES), F32).at[0, offset:offset + v.shape[0]].set(v.astype(F32))


def _head_padded_w2(w2):
    z = jnp.zeros_like(w2)
    return jnp.stack([jnp.concatenate([w2, z], axis=-1), jnp.concatenate([z, w2], axis=-1)]).astype(BF16)


def kernel(x, norm_mix, w_in, cmp_pos_k, cmp_pos_v, cmp_k_w1, cmp_k_b1, cmp_k_w2, cmp_v_w1, cmp_v_b1, cmp_v_w2,
           nsa_norm, gdn_conv, gdn_a_log, gdn_dt_bias, gdn_norm, w_out, norm_ffn, ffn_up, ffn_conv, ffn_down,
           norm_final):
    b, s, d = x.shape
    depth = w_in.shape[0]
    group = CMP_BLOCK // 2
    assert CMP_STRIDE == group and s % GDN_SEQ_TILE == 0 and s >= WIN_KEYS and (b * s) % ROW_TILE == 0
    x2 = x.reshape(b * s, d)
    for l in range(depth):
        wq, wkv, wgdn, wz, wsm = _split_in_weights(w_in[l])
        q, kv, gdn, z, sm = _inproj(x2, norm_mix[l][None, :], wq, wkv, wgdn, wz, wsm)
        q, kv, gdn, z, sm = (a.reshape(b, s, a.shape[-1]) for a in (q, kv, gdn, z, sm))

        groups = kv[:, :, :2 * NSA_KV_WIDTH].reshape(b, s, 2 * NSA_KV_HEADS, NSA_DH)
        groups = groups.transpose(0, 2, 1, 3).reshape(b, 2 * NSA_KV_HEADS, s // group, group * NSA_DH)
        kc, vc = _compress(
            groups, cmp_pos_k[l].reshape(1, -1), cmp_pos_v[l].reshape(1, -1),
            cmp_k_w1[l].astype(BF16), cmp_k_b1[l][None, :], _head_padded_w2(cmp_k_w2[l]),
            cmp_v_w1[l].astype(BF16), cmp_v_b1[l][None, :], _head_padded_w2(cmp_v_w2[l]))
        o_nsa = _nsa_attention(q, sm, kc, vc, kv)

        alog_vec, dtb_vec = _lane_vec(gdn_a_log[l], SM_A0), _lane_vec(gdn_dt_bias[l], SM_A0)
        qd, kdt, at, gu, gw, gcum = _gdn_prep(gdn, gdn_conv[l], sm, alog_vec, dtb_vec)
        o_gdn = _gdn_scan(qd, kdt, at, gu, gw, z, gcum, gdn_norm[l][None, :])

        x2 = _mix_ffn(x2.reshape(b, s, d), o_nsa, o_gdn, nsa_norm[l][None, :], w_out[l].astype(BF16),
                      norm_ffn[l][None, :], ffn_up[l].astype(BF16), ffn_conv[l], ffn_down[l].astype(BF16),
                      norm_final[None, :], final_norm=(l == depth - 1)).reshape(b * s, d)
    return x2.reshape(b, s, d)
```

```python
import functools

import jax
import jax.numpy as jnp
from jax import lax
from jax.experimental import pallas as pl
from jax.experimental.pallas import tpu as pltpu

F32 = jnp.float32
BF16 = jnp.bfloat16
EPS = 1e-6
NEG_INF = float("-inf")
MASK_BIAS = -1e30
LOG2_E = 1.4426950408889634

NSA_HEADS = 8
NSA_KV_HEADS = 2
NSA_GROUP = NSA_HEADS // NSA_KV_HEADS
NSA_DH = 64
NSA_WIDTH = NSA_HEADS * NSA_DH
NSA_KV_WIDTH = NSA_KV_HEADS * NSA_DH
CMP_BLOCK = 32
CMP_STRIDE = 16
SEL_BLOCK = 64
N_SELECT = 8
WINDOW = 512
GDN_HEADS = 4
GDN_DH = 128
GDN_WIDTH = GDN_HEADS * GDN_DH
GDN_CHUNK = 64
GDN_CHUNK_SHIFT = GDN_CHUNK.bit_length() - 1

LANES = 128
VMEM_LIMIT_BYTES = 56 * 1024 * 1024

ROW_TILE = 512
FFN_HALO = 8
FFN_COL_CHUNK = 256
FFN_DOWN_GROUP = 4
SEL_KEY_CHUNK = 512
NSA_STEP_BLOCKS = 2
WIN_KEYS = WINDOW + 2 * SEL_BLOCK
GDN_SEQ_TILE = 256
GDN_SCAN_BATCH = 4
GDN_PREP_TILE = 256
GDN_HALO = 16

SM_GATE0 = 0
SM_BETA0 = 3 * NSA_HEADS
SM_A0 = SM_BETA0 + GDN_HEADS


def _dot(a, b, precision=None):
    return jnp.dot(a, b, preferred_element_type=F32, precision=precision)


def _dot_nt(a, b, precision=None):
    return lax.dot_general(a, b, (((1,), (1,)), ((), ())), preferred_element_type=F32, precision=precision)


def _dot_tn(a, b, precision=None):
    return lax.dot_general(a, b, (((0,), (0,)), ((), ())), preferred_element_type=F32, precision=precision)


def _lane_col(x, idx):
    lane = lax.broadcasted_iota(jnp.int32, x.shape, 1)
    return jnp.sum(jnp.where(lane == idx, x, 0.0), axis=-1, keepdims=True)


def _const_spec(shape):
    zeros = (0,) * len(shape)
    return pl.BlockSpec(shape, lambda *_: zeros, pipeline_mode=pl.Buffered(1))


def _params(*semantics):
    return pltpu.CompilerParams(dimension_semantics=semantics, vmem_limit_bytes=VMEM_LIMIT_BYTES)


def _inproj_kernel(x_ref, xh_ref, g_ref, wq_ref, wkv_ref, wgdn_ref, wz_ref, wsm_ref, cw_ref,
                   q_ref, kv_ref, gdn_ref, z_ref, sm_ref, *, tiles_per_seq):
    def norm(x):
        return (x * lax.rsqrt(jnp.mean(x * x, axis=-1, keepdims=True) + EPS) * g_ref[...]).astype(BF16)

    h = norm(x_ref[...])
    q_ref[...] = _dot(h, wq_ref[...]).astype(BF16)
    kv_ref[...] = _dot(h, wkv_ref[...]).astype(BF16)
    z_ref[...] = _dot(h, wz_ref[...])
    sm_ref[...] = _dot(h, wsm_ref[...])
    has_history = jnp.where(pl.program_id(0) % tiles_per_seq > 0, 1.0, 0.0)
    u = jnp.concatenate([_dot(norm(xh_ref[...] * has_history), wgdn_ref[...]), _dot(h, wgdn_ref[...])],
                        axis=0)
    width = cw_ref.shape[0]
    for t in range(u.shape[1] // LANES):
        cols = slice(t * LANES, (t + 1) * LANES)
        ut = u[:, cols]
        y = ut[GDN_HALO:] * cw_ref[width - 1:width, cols]
        for k in range(width - 1):
            y = y + pltpu.roll(ut, width - 1 - k, axis=0)[GDN_HALO:] * cw_ref[k:k + 1, cols]
        gdn_ref[:, cols] = jax.nn.silu(y)


def _inproj(x2, g, wq, wkv, wgdn, wz, wsm, cw, seq):
    n, d = x2.shape
    tm = ROW_TILE
    per = tm // GDN_HALO
    row = lambda w: pl.BlockSpec((tm, w), lambda i: (i, 0))
    widths = (wq.shape[1], wkv.shape[1], wgdn.shape[1], wz.shape[1], wsm.shape[1])
    dtypes = (BF16, BF16, F32, F32, F32)
    return pl.pallas_call(
        functools.partial(_inproj_kernel, tiles_per_seq=seq // tm),
        grid=(n // tm,),
        in_specs=[row(d), pl.BlockSpec((GDN_HALO, d), lambda i: (jnp.maximum(i * per - 1, 0), 0)),
                  _const_spec((1, d))] + [_const_spec(w.shape) for w in (wq, wkv, wgdn, wz, wsm, cw)],
        out_specs=[row(w) for w in widths],
        out_shape=[jax.ShapeDtypeStruct((n, w), dt) for w, dt in zip(widths, dtypes)],
        compiler_params=_params("parallel"),
        name="inproj",
    )(x2, x2, g, wq, wkv, wgdn, wz, wsm, cw)


def _compress_kernel(g_ref, posk_ref, posv_ref, w1k_ref, b1k_ref, w2k_ref, w1v_ref, b1v_ref, w2v_ref,
                     kc_ref, vc_ref):
    ng = g_ref.shape[2]
    half = w1k_ref.shape[0] // 2
    row = lax.broadcasted_iota(jnp.int32, (ng, 1), 0)
    for sel, (pos_ref, w1_ref, b1_ref, w2_ref, out_ref) in enumerate(
            ((posk_ref, w1k_ref, b1k_ref, w2k_ref, kc_ref), (posv_ref, w1v_ref, b1v_ref, w2v_ref, vc_ref))):
        w1 = w1_ref[...]
        pos8 = jnp.broadcast_to(pos_ref[...], (8, pos_ref.shape[1])).astype(BF16)
        bias = _dot(pos8, w1)[0:1, :] + b1_ref[...]
        acc = jnp.zeros((ng, LANES), F32)
        for h in range(NSA_KV_HEADS):
            grp = g_ref[0, sel * NSA_KV_HEADS + h]
            top = _dot(grp, w1[:half])
            bot = _dot(grp, w1[half:])
            hid = top + pltpu.roll(bot, ng - 1, axis=0) + bias
            act = jax.nn.gelu(hid, approximate=True).astype(BF16)
            acc = acc + _dot(act, w2_ref[h])
        out_ref[0] = jnp.where(row < ng - 1, acc, 0.0).astype(BF16)


def _compress(groups, posk, posv, w1k, b1k, w2k, w1v, b1v, w2v):
    b, _, ng, gw = groups.shape
    out = jax.ShapeDtypeStruct((b, ng, LANES), BF16)
    return pl.pallas_call(
        _compress_kernel,
        grid=(b,),
        in_specs=[pl.BlockSpec((1, 2 * NSA_KV_HEADS, ng, gw), lambda i: (i, 0, 0, 0))]
        + [_const_spec(a.shape) for a in (posk, posv, w1k, b1k, w2k, w1v, b1v, w2v)],
        out_specs=[pl.BlockSpec((1, ng, LANES), lambda i: (i, 0, 0))] * 2,
        out_shape=[out, out],
        compiler_params=_params("parallel"),
        name="nsa_compress",
    )(groups, posk, posv, w1k, b1k, w2k, w1v, b1v, w2v)


def _row_reduce(x, combine, reduce):
    acc = x[:, :LANES]
    for t in range(1, x.shape[1] // LANES):
        acc = combine(acc, x[:, t * LANES:(t + 1) * LANES])
    return reduce(acc, axis=-1, keepdims=True)


def _row_max(x):
    return _row_reduce(x, jnp.maximum, jnp.max)


def _row_sum(x):
    return _row_reduce(x, jnp.add, jnp.sum)


def _softmax_parts(s, mask):
    sm = jnp.where(mask, s, NEG_INF)
    m = jnp.max(sm, axis=-1, keepdims=True)
    m = jnp.where(jnp.isfinite(m), m, 0.0)
    e = jnp.where(mask, jnp.exp2(sm - m), 0.0)
    return e, jnp.sum(e, axis=-1, keepdims=True)


def _nsa_kernel(q_ref, sm_ref, kc_ref, vc_ref, ks_ref, vs_ref, kw_ref, vw_ref, oh_ref, o_ref,
                m_ref, l_ref, acc_ref):
    qb = SEL_BLOCK
    nq = q_ref.shape[1] // qb
    seq = ks_ref.shape[1]
    ncmp = kc_ref.shape[1]
    nsb = seq // SEL_BLOCK
    rows = NSA_HEADS * qb
    both = NSA_KV_HEADS * qb
    blocks = range(nq)
    each = lambda f, *lists: [f(*args) for args in zip(*lists)]
    blk = [pl.program_id(1) * nq + i for i in blocks]
    s0 = [b * qb for b in blk]

    lane = lax.broadcasted_iota(jnp.int32, (qb, LANES), 1)
    t_q = [s + lax.broadcasted_iota(jnp.int32, (qb, 1), 0) for s in s0]
    per_head = lambda x: jnp.concatenate([x] * NSA_HEADS, axis=0)
    jn = lax.broadcasted_iota(jnp.int32, (nsb, ncmp), 0)
    nn = lax.broadcasted_iota(jnp.int32, (nsb, ncmp), 1)
    overlap_t = jnp.where((nn * CMP_STRIDE < jn * SEL_BLOCK + SEL_BLOCK)
                          & (jn * SEL_BLOCK < nn * CMP_STRIDE + CMP_BLOCK) & (nn < ncmp - 1), 1.0, 0.0)
    n_cmp_idx = lax.broadcasted_iota(jnp.int32, (1, ncmp), 1)
    jj = lax.broadcasted_iota(jnp.int32, (nsb, both), 0)
    eye = jnp.where(lax.broadcasted_iota(jnp.int32, (both, both), 0)
                    == lax.broadcasted_iota(jnp.int32, (both, both), 1), 1.0, 0.0).astype(BF16)

    gates = [jax.nn.sigmoid(sm_ref[0, i * qb:(i + 1) * qb]) for i in blocks]
    gcol = [[_lane_col(g, SM_GATE0 + j) for j in range(3 * NSA_HEADS)] for g in gates]

    qs = [jnp.concatenate([q_ref[0, i * qb:(i + 1) * qb, hq * LANES:(hq + 1) * LANES]
                           for hq in range(NSA_HEADS)], axis=0) for i in blocks]
    cmp_mask = [per_head(jnp.where((n_cmp_idx * CMP_STRIDE + (CMP_BLOCK - 1) <= t) & (n_cmp_idx < ncmp - 1),
                                   1.0, 0.0)) > 0.5 for t in t_q]
    d_lo = [pl.multiple_of(jnp.maximum(s - SEL_BLOCK, 0), SEL_BLOCK) for s in s0]
    diag_bias = [per_head(jnp.where(d + lax.broadcasted_iota(jnp.int32, (1, 2 * SEL_BLOCK), 1) <= t,
                                    0.0, MASK_BIAS)) for d, t in zip(d_lo, t_q)]
    w_lo = [pl.multiple_of(jnp.minimum(jnp.maximum(s - WINDOW, 0), seq - WIN_KEYS), SEL_BLOCK) for s in s0]
    kpos_w = [w + lax.broadcasted_iota(jnp.int32, (1, WIN_KEYS), 1) for w in w_lo]
    win_bias = [per_head(jnp.where((k <= t) & (k > t - WINDOW), 0.0, MASK_BIAS)) for k, t in zip(kpos_w, t_q)]

    s_c = [_dot_nt(q, kc_ref[0]) for q in qs]
    s_w = [_dot_nt(q, kw_ref[0, pl.ds(w, WIN_KEYS), :]) + b for q, w, b in zip(qs, w_lo, win_bias)]
    s_d = [_dot_nt(q, ks_ref[0, pl.ds(d, 2 * SEL_BLOCK), :]) + b for q, d, b in zip(qs, d_lo, diag_bias)]

    parts = each(_softmax_parts, s_c, cmp_mask)
    p_cmp = [e / jnp.maximum(l, 1e-30) for e, l in parts]
    o_cmp = [_dot(p.astype(BF16), vc_ref[0]) for p in p_cmp]

    def group_sums(p):
        out = []
        for h in range(NSA_KV_HEADS):
            acc = p[h * NSA_GROUP * qb:(h * NSA_GROUP + 1) * qb]
            for g in range(1, NSA_GROUP):
                acc = acc + p[(h * NSA_GROUP + g) * qb:(h * NSA_GROUP + g + 1) * qb]
            out.append(acc)
        return jnp.concatenate(out, axis=0)

    imp_t = [_dot_nt(overlap_t, group_sums(p), precision=lax.Precision.HIGHEST) for p in p_cmp]

    def selection_bias(imp, b):
        forced = (jj == 0) | (jj == b) | (jj == b - 1)
        score = jnp.where(forced, jnp.inf, jnp.where(jj <= b, imp, NEG_INF))
        rank = jnp.zeros((nsb, both), F32)
        for j2 in range(nsb):
            other = score[j2:j2 + 1, :]
            ahead = (other > score) | ((other == score) & (jj > j2))
            rank = rank + jnp.where(ahead, 1.0, 0.0)
        chosen = (rank < N_SELECT) & (score > NEG_INF) & (jj < b - 1)
        bias_t = jnp.concatenate([jnp.where(chosen, 0.0, MASK_BIAS), jnp.zeros((LANES - nsb, both), F32)], axis=0)
        return bias_t.astype(BF16)

    bias_t = each(selection_bias, imp_t, blk)
    bias_q = [_dot_nt(eye, b).astype(BF16) for b in bias_t]

    e_w = [jnp.exp2(s - _row_max(s)) for s in s_w]
    o_win = [_dot(e.astype(BF16), vw_ref[0, pl.ds(w, WIN_KEYS), :]) / jnp.maximum(_row_sum(e), 1e-30)
             for e, w in zip(e_w, w_lo)]

    m0 = [jnp.max(s, axis=-1, keepdims=True) for s in s_d]
    p0 = [jnp.exp2(s - m) for s, m in zip(s_d, m0)]
    for i in blocks:
        m_ref[i] = jnp.broadcast_to(m0[i], m_ref.shape[1:])
        l_ref[i] = jnp.broadcast_to(jnp.sum(p0[i], axis=-1, keepdims=True), l_ref.shape[1:])
        acc_ref[i] = _dot(p0[i].astype(BF16), vs_ref[0, pl.ds(d_lo[i], 2 * SEL_BLOCK), :])
    q_aug = [jnp.concatenate([q, jnp.concatenate([b[h * qb:(h + 1) * qb] for h in range(NSA_KV_HEADS)
                                                  for _ in range(NSA_GROUP)], axis=0)], axis=1)
             for q, b in zip(qs, bias_q)]
    n_chunks = (blk[-1] + SEL_KEY_CHUNK // SEL_BLOCK - 2) // (SEL_KEY_CHUNK // SEL_BLOCK)

    def sel_step(c, _):
        k0 = pl.multiple_of(c * SEL_KEY_CHUNK, SEL_KEY_CHUNK)
        k_aug = jnp.concatenate([ks_ref[0, pl.ds(k0, SEL_KEY_CHUNK), :], oh_ref[pl.ds(k0, SEL_KEY_CHUNK), :]],
                                axis=1)
        v_blk = vs_ref[0, pl.ds(k0, SEL_KEY_CHUNK), :]
        s = [_dot_nt(q, k_aug) for q in q_aug]
        for i in blocks:
            m_prev = m_ref[i]
            m_new = jnp.maximum(m_prev, _row_max(s[i]))
            p = jnp.exp2(s[i] - jnp.concatenate([m_new] * (SEL_KEY_CHUNK // LANES), axis=1))
            alpha = jnp.exp2(m_prev - m_new)
            m_ref[i] = m_new
            l_ref[i] = alpha * l_ref[i] + _row_sum(p)
            acc_ref[i] = alpha * acc_ref[i] + _dot(p.astype(BF16), v_blk)
        return 0

    head_rows = lambda x, hq: x[hq * qb:(hq + 1) * qb]
    partial = [[gcol[i][3 * hq] * head_rows(o_cmp[i], hq) + gcol[i][3 * hq + 2] * head_rows(o_win[i], hq)
                for hq in range(NSA_HEADS)] for i in blocks]
    lax.fori_loop(0, n_chunks, sel_step, 0)

    for i in blocks:
        o_sel = acc_ref[i] / jnp.maximum(l_ref[i], 1e-30)
        mixed = [partial[i][hq] + gcol[i][3 * hq + 1] * head_rows(o_sel, hq) for hq in range(NSA_HEADS)]
        for pair in range(NSA_HEADS // 2):
            a, b = mixed[2 * pair], mixed[2 * pair + 1]
            if 2 * pair // NSA_GROUP == 0:
                b = pltpu.roll(b, NSA_DH, axis=1)
            else:
                a = pltpu.roll(a, NSA_DH, axis=1)
            o_ref[0, i * qb:(i + 1) * qb, pair * LANES:(pair + 1) * LANES] = jnp.where(lane < NSA_DH, a, b)


def _nsa_attention(q, sm, kc, vc, kv):
    b, s, qw = q.shape
    tq = NSA_STEP_BLOCKS * SEL_BLOCK
    ncmp = kc.shape[1]
    onehot = (jnp.arange(s)[:, None] // SEL_BLOCK == jnp.arange(LANES)[None, :]).astype(BF16)
    kv_spec = lambda col: pl.BlockSpec((1, s, LANES), lambda i, j: (i, 0, col))
    return pl.pallas_call(
        _nsa_kernel,
        grid=(b, s // tq),
        in_specs=[pl.BlockSpec((1, tq, qw), lambda i, j: (i, j, 0)),
                  pl.BlockSpec((1, tq, LANES), lambda i, j: (i, j, 0)),
                  pl.BlockSpec((1, ncmp, LANES), lambda i, j: (i, 0, 0)),
                  pl.BlockSpec((1, ncmp, LANES), lambda i, j: (i, 0, 0)),
                  kv_spec(2), kv_spec(3), kv_spec(4), kv_spec(5), _const_spec((s, LANES))],
        out_specs=pl.BlockSpec((1, tq, NSA_WIDTH), lambda i, j: (i, j, 0)),
        out_shape=jax.ShapeDtypeStruct((b, s, NSA_WIDTH), F32),
        scratch_shapes=[pltpu.VMEM((NSA_STEP_BLOCKS, NSA_HEADS * SEL_BLOCK, LANES), F32)] * 3,
        compiler_params=_params("parallel", "arbitrary"),
        name="nsa_attention",
    )(q, sm, kc, vc, kv, kv, kv, kv, onehot)


def _unit_lower_inverses(mats, r, c):
    n = mats[0].shape[0]
    eye = jnp.where(r == c, 1.0, 0.0)
    blk16 = (r >> 4) == (c >> 4)
    each = lambda f, *lists: [f(*args) for args in zip(*lists)]
    mm = lambda x, y: _dot(x.astype(BF16), y.astype(BF16))
    stacked = lambda top, rhs: mm(jnp.concatenate([top, rhs], axis=0), rhs)
    n1 = each(lambda a: jnp.where(blk16, -a, 0.0), mats)
    n2 = each(mm, n1, n1)
    t = each(lambda m: eye + m, n1)
    x = each(stacked, t, n2)
    t, n4 = each(lambda ti, xi: ti + xi[:n], t, x), each(lambda xi: xi[n:], x)
    x = each(stacked, t, n4)
    t, n8 = each(lambda ti, xi: ti + xi[:n], t, x), each(lambda xi: xi[n:], x)
    t = each(lambda ti, m: ti + mm(ti, m), t, n8)
    v = each(lambda a, ti: mm(jnp.where(blk16, 0.0, a), ti), mats, t)
    x = each(stacked, t, v)
    p, v2 = each(lambda ti, xi: ti - xi[:n], t, x), each(lambda xi: xi[n:], x)
    return each(lambda pi, m: pi + mm(pi, m), p, v2)


def _log_decay(sm, alog_ref, dtb_ref):
    return -jnp.exp(alog_ref[...]) * jax.nn.softplus(sm + dtb_ref[...])


def _gdn_prep_kernel(x_ref, sm_ref, alog_ref, dtb_ref, qd_ref, kdt_ref, at_ref, u_ref, w_ref, gcum_ref):
    n = x_ref.shape[1]

    def head_cols(c0):
        return x_ref[0, :, c0:c0 + GDN_DH]

    def l2norm(y):
        return y * lax.rsqrt(jnp.sum(y * y, axis=-1, keepdims=True) + EPS)

    r = lax.broadcasted_iota(jnp.int32, (n, n), 0)
    c = lax.broadcasted_iota(jnp.int32, (n, n), 1)
    in_chunk = (r >> GDN_CHUNK_SHIFT) == (c >> GDN_CHUNK_SHIFT)
    causal = in_chunk & (r >= c)
    hi = lax.Precision.HIGHEST
    sm = sm_ref[0]
    beta = jax.nn.sigmoid(sm)
    g = _log_decay(sm, alog_ref, dtb_ref)
    gcum = _dot(jnp.where(causal, 1.0, 0.0), g, precision=hi)
    gtot = _dot(jnp.where(in_chunk, 1.0, 0.0), g, precision=hi)
    gcum_ref[0] = gcum
    pick = jnp.where(lax.broadcasted_iota(jnp.int32, (8, LANES), 1)
                     == lax.broadcasted_iota(jnp.int32, (8, LANES), 0) + SM_A0, 1.0, 0.0)
    gcum_rows = _dot_nt(pick, gcum, precision=hi)

    a_mats, rhs = [], []
    for h in range(GDN_HEADS):
        cols = slice(h * GDN_DH, (h + 1) * GDN_DH)
        q = l2norm(head_cols(h * GDN_DH)) * (GDN_DH ** -0.5)
        k = l2norm(head_cols(GDN_WIDTH + h * GDN_DH))
        v = head_cols(2 * GDN_WIDTH + h * GDN_DH)
        gc = _lane_col(gcum, SM_A0 + h)
        bcol = _lane_col(beta, SM_BETA0 + h)
        decay = jnp.exp(jnp.where(causal, gc - gcum_rows[h:h + 1, :], NEG_INF))
        kb = k * bcol
        egc = jnp.exp(gc)
        qd_ref[0, :, cols] = (q * egc).astype(BF16)
        kdt_ref[0, cols, :] = (k * jnp.exp(_lane_col(gtot, SM_A0 + h) - gc)).T.astype(BF16)
        grams = _dot_nt(jnp.concatenate([kb, q], axis=0).astype(BF16), k.astype(BF16))
        attn = (grams[n:] * decay).astype(BF16)
        for ch in range(n // GDN_CHUNK):
            lt = ch * GDN_CHUNK // LANES
            at_ref[0, ch * GDN_CHUNK:(ch + 1) * GDN_CHUNK, cols] = (
                attn[ch * GDN_CHUNK:(ch + 1) * GDN_CHUNK, lt * LANES:(lt + 1) * LANES])
        a_mats.append(jnp.where(r > c, grams[:n] * decay, 0.0))
        rhs.append(jnp.concatenate([v * bcol, kb * egc], axis=1).astype(BF16))
    for h, t_inv in enumerate(_unit_lower_inverses(a_mats, r, c)):
        cols = slice(h * GDN_DH, (h + 1) * GDN_DH)
        uw = _dot(t_inv.astype(BF16), rhs[h])
        u_ref[0, :, cols] = uw[:, :GDN_DH]
        w_ref[0, :, cols] = uw[:, GDN_DH:].astype(BF16)


def _gdn_prep(qkv, sm, alog_vec, dtb_vec):
    b, s, width = qkv.shape
    n = GDN_PREP_TILE
    tok = lambda w: pl.BlockSpec((1, n, w), lambda i, j: (i, j, 0))
    out = lambda dt: jax.ShapeDtypeStruct((b, s, GDN_WIDTH), dt)
    return pl.pallas_call(
        _gdn_prep_kernel,
        grid=(b, s // n),
        in_specs=[tok(width), tok(LANES), _const_spec((1, LANES)), _const_spec((1, LANES))],
        out_specs=[tok(GDN_WIDTH), pl.BlockSpec((1, GDN_WIDTH, n), lambda i, j: (i, 0, j))]
        + [tok(GDN_WIDTH)] * 3 + [tok(LANES)],
        out_shape=[out(BF16), jax.ShapeDtypeStruct((b, GDN_WIDTH, s), BF16), out(BF16), out(F32), out(BF16),
                   jax.ShapeDtypeStruct((b, s, LANES), F32)],
        compiler_params=_params("parallel", "parallel"),
        name="gdn_prep",
    )(qkv, sm, alog_vec, dtb_vec)


def _gdn_scan_kernel(qd_ref, kdt_ref, at_ref, u_ref, w_ref, z_ref, gcum_ref, ng_ref, o_ref, state_ref):
    c = GDN_CHUNK
    n_chunks = qd_ref.shape[1] // c

    @pl.when(pl.program_id(1) == 0)
    def _():
        state_ref[...] = jnp.zeros_like(state_ref)

    def chunk_step(ci, _):
        t0 = pl.multiple_of(ci * c, c)
        rows = pl.ds(t0, c)
        odd = (ci & 1) == 1
        pair = pl.ds(pl.multiple_of((ci >> 1) * (2 * c), 2 * c), 2 * c)
        chains = [(bi, h) for bi in range(qd_ref.shape[0]) for h in range(GDN_HEADS)]
        cols = lambda h: slice(h * GDN_DH, (h + 1) * GDN_DH)
        states = [state_ref[bi * GDN_HEADS + h] for bi, h in chains]
        ws_qs = [_dot(jnp.concatenate([w_ref[bi, rows, cols(h)], qd_ref[bi, rows, cols(h)]], axis=0),
                      s.astype(BF16)) for (bi, h), s in zip(chains, states)]
        av_kv = []
        for (bi, h), x in zip(chains, ws_qs):
            v_new = (u_ref[bi, rows, cols(h)] - x[:c]).astype(BF16)
            zero = jnp.zeros_like(v_new)
            v_pad = jnp.concatenate([jnp.where(odd, zero, v_new), jnp.where(odd, v_new, zero)], axis=0)
            av_kv.append(_dot(jnp.concatenate([at_ref[bi, rows, cols(h)], kdt_ref[bi, cols(h), pair]], axis=0),
                              v_pad))
        for (bi, h), s, x, y in zip(chains, states, ws_qs, av_kv):
            state_decay = jnp.exp(gcum_ref[bi, pl.ds(t0 + c - 1, 1), :])
            state_ref[bi * GDN_HEADS + h] = s * _lane_col(state_decay, SM_A0 + h) + y[c:]
            o = x[c:] + y[:c]
            o = o * lax.rsqrt(jnp.mean(o * o, axis=-1, keepdims=True) + EPS) * ng_ref[...]
            o_ref[bi, rows, cols(h)] = o * jax.nn.silu(z_ref[bi, rows, cols(h)])
        return 0

    lax.fori_loop(0, n_chunks, chunk_step, 0)


def _gdn_scan(qd, kdt, at, u, w, z, gcum, norm_g):
    b, s, _ = qd.shape
    ts = GDN_SEQ_TILE
    nb = GDN_SCAN_BATCH if b % GDN_SCAN_BATCH == 0 else 1
    tok = lambda wd: pl.BlockSpec((nb, ts, wd), lambda i, j: (i, j, 0))
    return pl.pallas_call(
        _gdn_scan_kernel,
        grid=(b // nb, s // ts),
        in_specs=[tok(GDN_WIDTH), pl.BlockSpec((nb, GDN_WIDTH, ts), lambda i, j: (i, 0, j))]
        + [tok(GDN_WIDTH)] * 4 + [tok(LANES), _const_spec((1, GDN_DH))],
        out_specs=tok(GDN_WIDTH),
        out_shape=jax.ShapeDtypeStruct((b, s, GDN_WIDTH), F32),
        scratch_shapes=[pltpu.VMEM((nb * GDN_HEADS, GDN_DH, GDN_DH), F32)],
        compiler_params=_params("parallel", "arbitrary"),
        name="gdn_scan",
    )(qd, kdt, at, u, w, z, gcum, norm_g)


def _mix_ffn_kernel(x_ref, xh_ref, on_ref, onh_ref, og_ref, ogh_ref, gn_ref, wo_ref,
                    g_ref, wup_ref, cw_ref, wdn_ref, gf_ref, o_ref, *, final_norm):
    tm = x_ref.shape[1]
    d_ff = wdn_ref.shape[0]
    width = cw_ref.shape[0]
    has_history = jnp.where(pl.program_id(1) > 0, 1.0, 0.0)
    with_halo = lambda ref, halo: jnp.concatenate([halo[0] * has_history, ref[0]], axis=0)
    on = with_halo(on_ref, onh_ref)
    hn = (on * lax.rsqrt(jnp.mean(on * on, axis=-1, keepdims=True) + EPS) * gn_ref[...]).astype(BF16)
    mixed = jnp.concatenate([hn, with_halo(og_ref, ogh_ref).astype(BF16)], axis=1)
    xin = with_halo(x_ref, xh_ref) + _dot(mixed, wo_ref[...])
    x = xin[FFN_HALO:]
    ms = jnp.mean(xin * xin, axis=-1, keepdims=True)
    h = (xin * lax.rsqrt(ms + EPS) * g_ref[...]).astype(BF16)

    def conv(u, c0):
        y = u[FFN_HALO:] * cw_ref[width - 1:width, c0:c0 + FFN_COL_CHUNK]
        for k in range(width - 1):
            shifted = pltpu.roll(u, width - 1 - k, axis=0)[FFN_HALO:]
            y = y + shifted * cw_ref[k:k + 1, c0:c0 + FFN_COL_CHUNK]
        return y

    def up_proj(j):
        c0 = j * FFN_COL_CHUNK
        return (_dot(h, wup_ref[:, c0:c0 + FFN_COL_CHUNK]),
                _dot(h, wup_ref[:, d_ff + c0:d_ff + c0 + FFN_COL_CHUNK]))

    n_chunks = d_ff // FFN_COL_CHUNK
    y = x
    acts = []
    nxt = up_proj(0)
    for j in range(n_chunks):
        c0 = j * FFN_COL_CHUNK
        u_gate, u_up = nxt
        if j + 1 < n_chunks:
            nxt = up_proj(j + 1)
        acts.append((jax.nn.silu(conv(u_gate, c0)) * conv(u_up, d_ff + c0)).astype(BF16))
        if len(acts) == FFN_DOWN_GROUP or j + 1 == n_chunks:
            k0 = c0 + FFN_COL_CHUNK - len(acts) * FFN_COL_CHUNK
            y = y + _dot(jnp.concatenate(acts, axis=1), wdn_ref[k0:c0 + FFN_COL_CHUNK, :])
            acts = []
    if final_norm:
        y = y * lax.rsqrt(jnp.mean(y * y, axis=-1, keepdims=True) + EPS) * gf_ref[...]
    o_ref[0] = y


def _mix_ffn(x3, o_nsa, o_gdn, g_nsa, wo, g, wup, cw, wdn, g_final, final_norm):
    b, s, d = x3.shape
    tm = ROW_TILE
    per = tm // FFN_HALO
    rows = lambda w: pl.BlockSpec((1, tm, w), lambda i, j: (i, j, 0))
    halo = lambda w: pl.BlockSpec((1, FFN_HALO, w), lambda i, j: (i, jnp.maximum(j * per - 1, 0), 0))
    consts = (g_nsa, wo, g, wup, cw, wdn, g_final)
    return pl.pallas_call(
        functools.partial(_mix_ffn_kernel, final_norm=final_norm),
        grid=(b, s // tm),
        in_specs=[rows(d), halo(d), rows(o_nsa.shape[2]), halo(o_nsa.shape[2]),
                  rows(o_gdn.shape[2]), halo(o_gdn.shape[2])] + [_const_spec(c.shape) for c in consts],
        out_specs=rows(d),
        out_shape=jax.ShapeDtypeStruct((b, s, d), F32),
        compiler_params=_params("parallel", "parallel"),
        name="mix_convffn",
    )(x3, x3, o_nsa, o_nsa, o_gdn, o_gdn, *consts)


def _split_in_weights(w_in):
    d = w_in.shape[0]
    o = 0
    wq = w_in[:, o:o + NSA_WIDTH]; o += NSA_WIDTH
    wkv = w_in[:, o:o + 6 * NSA_KV_WIDTH]; o += 6 * NSA_KV_WIDTH
    wgate = w_in[:, o:o + 3 * NSA_HEADS]; o += 3 * NSA_HEADS
    wgdn = w_in[:, o:o + 3 * GDN_WIDTH]; o += 3 * GDN_WIDTH
    wz = w_in[:, o:o + GDN_WIDTH]; o += GDN_WIDTH
    wb = w_in[:, o:o + GDN_HEADS]; o += GDN_HEADS
    wa = w_in[:, o:o + GDN_HEADS]
    wq = wq.reshape(d, NSA_HEADS, NSA_DH) * (NSA_DH ** -0.5 * LOG2_E)
    zero = jnp.zeros_like(wq)
    kv_head = (jnp.arange(NSA_HEADS) // NSA_GROUP)[None, :, None]
    wq_pad = jnp.concatenate([jnp.where(kv_head == 0, wq, zero), jnp.where(kv_head == 1, wq, zero)], axis=-1)
    wq_pad = wq_pad.reshape(d, NSA_HEADS * LANES)
    wsm = jnp.concatenate([wgate, wb, wa, jnp.zeros((d, LANES - SM_A0 - GDN_HEADS), w_in.dtype)], axis=-1)
    return tuple(w.astype(BF16) for w in (wq_pad, wkv, wgdn, wz, wsm))


def _lane_vec(v, offset):
    return jnp.zeros((1, LANES), F32).at[0, offset:offset + v.shape[0]].set(v.astype(F32))


def _head_padded_w2(w2):
    z = jnp.zeros_like(w2)
    return jnp.stack([jnp.concatenate([w2, z], axis=-1), jnp.concatenate([z, w2], axis=-1)]).astype(BF16)


def kernel(x, norm_mix, w_in, cmp_pos_k, cmp_pos_v, cmp_k_w1, cmp_k_b1, cmp_k_w2, cmp_v_w1, cmp_v_b1, cmp_v_w2,
           nsa_norm, gdn_conv, gdn_a_log, gdn_dt_bias, gdn_norm, w_out, norm_ffn, ffn_up, ffn_conv, ffn_down,
           norm_final):
    b, s, d = x.shape
    depth = w_in.shape[0]
    group = CMP_BLOCK // 2
    assert CMP_STRIDE == group and s % GDN_SEQ_TILE == 0 and s >= WIN_KEYS and (b * s) % ROW_TILE == 0
    x2 = x.reshape(b * s, d)
    for l in range(depth):
        wq, wkv, wgdn, wz, wsm = _split_in_weights(w_in[l])
        q, kv, gdn, z, sm = _inproj(x2, norm_mix[l][None, :], wq, wkv, wgdn, wz, wsm, gdn_conv[l], s)
        q, kv, gdn, z, sm = (a.reshape(b, s, a.shape[-1]) for a in (q, kv, gdn, z, sm))

        groups = kv[:, :, :2 * NSA_KV_WIDTH].reshape(b, s, 2 * NSA_KV_HEADS, NSA_DH)
        groups = groups.transpose(0, 2, 1, 3).reshape(b, 2 * NSA_KV_HEADS, s // group, group * NSA_DH)
        kc, vc = _compress(
            groups, cmp_pos_k[l].reshape(1, -1), cmp_pos_v[l].reshape(1, -1),
            cmp_k_w1[l].astype(BF16), cmp_k_b1[l][None, :], _head_padded_w2(cmp_k_w2[l]),
            cmp_v_w1[l].astype(BF16), cmp_v_b1[l][None, :], _head_padded_w2(cmp_v_w2[l]))
        o_nsa = _nsa_attention(q, sm, kc, vc, kv)

        alog_vec, dtb_vec = _lane_vec(gdn_a_log[l], SM_A0), _lane_vec(gdn_dt_bias[l], SM_A0)
        qd, kdt, at, gu, gw, gcum = _gdn_prep(gdn, sm, alog_vec, dtb_vec)
        o_gdn = _gdn_scan(qd, kdt, at, gu, gw, z, gcum, gdn_norm[l][None, :])

        x2 = _mix_ffn(x2.reshape(b, s, d), o_nsa, o_gdn, nsa_norm[l][None, :], w_out[l].astype(BF16),
                      norm_ffn[l][None, :], ffn_up[l].astype(BF16), ffn_conv[l], ffn_down[l].astype(BF16),
                      norm_final[None, :], final_norm=(l == depth - 1)).reshape(b * s, d)
    return x2.reshape(b, s, d)
```

```python
import functools

import jax
import jax.numpy as jnp
from jax import lax
from jax.experimental import pallas as pl
from jax.experimental.pallas import tpu as pltpu

F32 = jnp.float32
BF16 = jnp.bfloat16
EPS = 1e-6
NEG_INF = float("-inf")
MASK_BIAS = -1e30
LOG2_E = 1.4426950408889634

NSA_HEADS = 8
NSA_KV_HEADS = 2
NSA_GROUP = NSA_HEADS // NSA_KV_HEADS
NSA_DH = 64
NSA_WIDTH = NSA_HEADS * NSA_DH
NSA_KV_WIDTH = NSA_KV_HEADS * NSA_DH
CMP_BLOCK = 32
CMP_STRIDE = 16
SEL_BLOCK = 64
N_SELECT = 8
WINDOW = 512
GDN_HEADS = 4
GDN_DH = 128
GDN_WIDTH = GDN_HEADS * GDN_DH
GDN_CHUNK = 64
GDN_CHUNK_SHIFT = GDN_CHUNK.bit_length() - 1

LANES = 128
VMEM_LIMIT_BYTES = 56 * 1024 * 1024

ROW_TILE = 512
FFN_HALO = 8
FFN_COL_CHUNK = 256
FFN_DOWN_GROUP = 4
SEL_KEY_CHUNK = 512
NSA_STEP_BLOCKS = 2
WIN_KEYS = WINDOW + 2 * SEL_BLOCK
GDN_SEQ_TILE = 256
GDN_SCAN_BATCH = 4
GDN_PREP_TILE = 256
GDN_PREP_GROUPS = 2
GDN_HALO = 8

SM_GATE0 = 0
SM_BETA0 = 3 * NSA_HEADS
SM_A0 = SM_BETA0 + GDN_HEADS


def _dot(a, b, precision=None):
    return jnp.dot(a, b, preferred_element_type=F32, precision=precision)


def _dot_nt(a, b, precision=None):
    return lax.dot_general(a, b, (((1,), (1,)), ((), ())), preferred_element_type=F32, precision=precision)


def _lane_col(x, idx):
    lane = lax.broadcasted_iota(jnp.int32, x.shape, 1)
    return jnp.sum(jnp.where(lane == idx, x, 0.0), axis=-1, keepdims=True)


def _const_spec(shape):
    zeros = (0,) * len(shape)
    return pl.BlockSpec(shape, lambda *_: zeros, pipeline_mode=pl.Buffered(1))


def _params(*semantics):
    return pltpu.CompilerParams(dimension_semantics=semantics, vmem_limit_bytes=VMEM_LIMIT_BYTES)


def _inproj_kernel(x_ref, g_ref, wq_ref, wkv_ref, wgdn_ref, wz_ref, wsm_ref,
                   q_ref, kv_ref, gdn_ref, z_ref, sm_ref):
    x = x_ref[...]
    ms = jnp.mean(x * x, axis=-1, keepdims=True)
    h = (x * lax.rsqrt(ms + EPS) * g_ref[...]).astype(BF16)
    q_ref[...] = _dot(h, wq_ref[...]).astype(BF16)
    kv_ref[...] = _dot(h, wkv_ref[...]).astype(BF16)
    gdn_ref[...] = _dot(h, wgdn_ref[...])
    z_ref[...] = _dot(h, wz_ref[...])
    sm_ref[...] = _dot(h, wsm_ref[...])


def _inproj(x2, g, wq, wkv, wgdn, wz, wsm):
    n, d = x2.shape
    tm = ROW_TILE
    row = lambda w: pl.BlockSpec((tm, w), lambda i: (i, 0))
    widths = (wq.shape[1], wkv.shape[1], wgdn.shape[1], wz.shape[1], wsm.shape[1])
    dtypes = (BF16, BF16, F32, F32, F32)
    return pl.pallas_call(
        _inproj_kernel,
        grid=(n // tm,),
        in_specs=[row(d), _const_spec((1, d))] + [_const_spec(w.shape) for w in (wq, wkv, wgdn, wz, wsm)],
        out_specs=[row(w) for w in widths],
        out_shape=[jax.ShapeDtypeStruct((n, w), dt) for w, dt in zip(widths, dtypes)],
        compiler_params=_params("parallel"),
        name="inproj",
    )(x2, g, wq, wkv, wgdn, wz, wsm)


def _compress_kernel(g_ref, posk_ref, posv_ref, w1k_ref, b1k_ref, w2k_ref, w1v_ref, b1v_ref, w2v_ref,
                     kc_ref, vc_ref):
    ng = g_ref.shape[2]
    half = w1k_ref.shape[0] // 2
    row = lax.broadcasted_iota(jnp.int32, (ng, 1), 0)
    for sel, (pos_ref, w1_ref, b1_ref, w2_ref, out_ref) in enumerate(
            ((posk_ref, w1k_ref, b1k_ref, w2k_ref, kc_ref), (posv_ref, w1v_ref, b1v_ref, w2v_ref, vc_ref))):
        w1 = w1_ref[...]
        pos8 = jnp.broadcast_to(pos_ref[...], (8, pos_ref.shape[1])).astype(BF16)
        bias = _dot(pos8, w1)[0:1, :] + b1_ref[...]
        acc = jnp.zeros((ng, LANES), F32)
        for h in range(NSA_KV_HEADS):
            grp = g_ref[0, sel * NSA_KV_HEADS + h]
            top = _dot(grp, w1[:half])
            bot = _dot(grp, w1[half:])
            hid = top + pltpu.roll(bot, ng - 1, axis=0) + bias
            act = jax.nn.gelu(hid, approximate=True).astype(BF16)
            acc = acc + _dot(act, w2_ref[h])
        out_ref[0] = jnp.where(row < ng - 1, acc, 0.0).astype(BF16)


def _compress(groups, posk, posv, w1k, b1k, w2k, w1v, b1v, w2v):
    b, _, ng, gw = groups.shape
    out = jax.ShapeDtypeStruct((b, ng, LANES), BF16)
    return pl.pallas_call(
        _compress_kernel,
        grid=(b,),
        in_specs=[pl.BlockSpec((1, 2 * NSA_KV_HEADS, ng, gw), lambda i: (i, 0, 0, 0))]
        + [_const_spec(a.shape) for a in (posk, posv, w1k, b1k, w2k, w1v, b1v, w2v)],
        out_specs=[pl.BlockSpec((1, ng, LANES), lambda i: (i, 0, 0))] * 2,
        out_shape=[out, out],
        compiler_params=_params("parallel"),
        name="nsa_compress",
    )(groups, posk, posv, w1k, b1k, w2k, w1v, b1v, w2v)


def _row_reduce(x, combine, reduce):
    acc = x[:, :LANES]
    for t in range(1, x.shape[1] // LANES):
        acc = combine(acc, x[:, t * LANES:(t + 1) * LANES])
    return reduce(acc, axis=-1, keepdims=True)


def _row_max(x):
    return _row_reduce(x, jnp.maximum, jnp.max)


def _row_sum(x):
    return _row_reduce(x, jnp.add, jnp.sum)


def _softmax_parts(s, mask):
    sm = jnp.where(mask, s, NEG_INF)
    m = jnp.max(sm, axis=-1, keepdims=True)
    m = jnp.where(jnp.isfinite(m), m, 0.0)
    e = jnp.where(mask, jnp.exp2(sm - m), 0.0)
    return e, jnp.sum(e, axis=-1, keepdims=True)


def _nsa_kernel(q_ref, sm_ref, kc_ref, vc_ref, ks_ref, vs_ref, kw_ref, vw_ref, oh_ref, o_ref,
                m_ref, l_ref, acc_ref):
    qb = SEL_BLOCK
    nq = q_ref.shape[1] // qb
    seq = ks_ref.shape[1]
    ncmp = kc_ref.shape[1]
    nsb = seq // SEL_BLOCK
    both = NSA_KV_HEADS * qb
    blocks = range(nq)
    each = lambda f, *lists: [f(*args) for args in zip(*lists)]
    blk = [pl.program_id(1) * nq + i for i in blocks]
    s0 = [b * qb for b in blk]

    lane = lax.broadcasted_iota(jnp.int32, (qb, LANES), 1)
    t_q = [s + lax.broadcasted_iota(jnp.int32, (qb, 1), 0) for s in s0]
    per_head = lambda x: jnp.concatenate([x] * NSA_HEADS, axis=0)
    jn = lax.broadcasted_iota(jnp.int32, (nsb, ncmp), 0)
    nn = lax.broadcasted_iota(jnp.int32, (nsb, ncmp), 1)
    overlap_t = jnp.where((nn * CMP_STRIDE < jn * SEL_BLOCK + SEL_BLOCK)
                          & (jn * SEL_BLOCK < nn * CMP_STRIDE + CMP_BLOCK) & (nn < ncmp - 1), 1.0, 0.0)
    n_cmp_idx = lax.broadcasted_iota(jnp.int32, (1, ncmp), 1)
    jj = lax.broadcasted_iota(jnp.int32, (nsb, both), 0)
    eye = jnp.where(lax.broadcasted_iota(jnp.int32, (both, both), 0)
                    == lax.broadcasted_iota(jnp.int32, (both, both), 1), 1.0, 0.0).astype(BF16)

    gates = [jax.nn.sigmoid(sm_ref[0, i * qb:(i + 1) * qb]) for i in blocks]
    gcol = [[_lane_col(g, SM_GATE0 + j) for j in range(3 * NSA_HEADS)] for g in gates]

    qs = [jnp.concatenate([q_ref[0, i * qb:(i + 1) * qb, hq * LANES:(hq + 1) * LANES]
                           for hq in range(NSA_HEADS)], axis=0) for i in blocks]
    cmp_mask = [per_head(jnp.where((n_cmp_idx * CMP_STRIDE + (CMP_BLOCK - 1) <= t) & (n_cmp_idx < ncmp - 1),
                                   1.0, 0.0)) > 0.5 for t in t_q]
    d_lo = [pl.multiple_of(jnp.maximum(s - SEL_BLOCK, 0), SEL_BLOCK) for s in s0]
    diag_bias = [per_head(jnp.where(d + lax.broadcasted_iota(jnp.int32, (1, 2 * SEL_BLOCK), 1) <= t,
                                    0.0, MASK_BIAS)) for d, t in zip(d_lo, t_q)]
    w_lo = [pl.multiple_of(jnp.minimum(jnp.maximum(s - WINDOW, 0), seq - WIN_KEYS), SEL_BLOCK) for s in s0]
    kpos_w = [w + lax.broadcasted_iota(jnp.int32, (1, WIN_KEYS), 1) for w in w_lo]
    win_bias = [per_head(jnp.where((k <= t) & (k > t - WINDOW), 0.0, MASK_BIAS)) for k, t in zip(kpos_w, t_q)]

    s_c = [_dot_nt(q, kc_ref[0]) for q in qs]
    s_w = [_dot_nt(q, kw_ref[0, pl.ds(w, WIN_KEYS), :]) + b for q, w, b in zip(qs, w_lo, win_bias)]
    s_d = [_dot_nt(q, ks_ref[0, pl.ds(d, 2 * SEL_BLOCK), :]) + b for q, d, b in zip(qs, d_lo, diag_bias)]

    parts = each(_softmax_parts, s_c, cmp_mask)
    p_cmp = [e / jnp.maximum(l, 1e-30) for e, l in parts]
    o_cmp = [_dot(p.astype(BF16), vc_ref[0]) for p in p_cmp]

    def group_sums(p):
        out = []
        for h in range(NSA_KV_HEADS):
            acc = p[h * NSA_GROUP * qb:(h * NSA_GROUP + 1) * qb]
            for g in range(1, NSA_GROUP):
                acc = acc + p[(h * NSA_GROUP + g) * qb:(h * NSA_GROUP + g + 1) * qb]
            out.append(acc)
        return jnp.concatenate(out, axis=0)

    imp_t = [_dot_nt(overlap_t, group_sums(p), precision=lax.Precision.HIGHEST) for p in p_cmp]

    def selection_bias(imp, b):
        forced = (jj == 0) | (jj == b) | (jj == b - 1)
        score = jnp.where(forced, jnp.inf, jnp.where(jj <= b, imp, NEG_INF))
        rank = jnp.zeros((nsb, both), F32)
        for j2 in range(nsb):
            other = score[j2:j2 + 1, :]
            ahead = (other > score) | ((other == score) & (jj > j2))
            rank = rank + jnp.where(ahead, 1.0, 0.0)
        chosen = (rank < N_SELECT) & (score > NEG_INF) & (jj < b - 1)
        bias_t = jnp.concatenate([jnp.where(chosen, 0.0, MASK_BIAS), jnp.zeros((LANES - nsb, both), F32)], axis=0)
        return bias_t.astype(BF16)

    bias_t = each(selection_bias, imp_t, blk)
    bias_q = [_dot_nt(eye, b).astype(BF16) for b in bias_t]

    e_w = [jnp.exp2(s - _row_max(s)) for s in s_w]
    o_win = [_dot(e.astype(BF16), vw_ref[0, pl.ds(w, WIN_KEYS), :]) / jnp.maximum(_row_sum(e), 1e-30)
             for e, w in zip(e_w, w_lo)]

    m0 = [jnp.max(s, axis=-1, keepdims=True) for s in s_d]
    p0 = [jnp.exp2(s - m) for s, m in zip(s_d, m0)]
    for i in blocks:
        m_ref[i] = jnp.broadcast_to(m0[i], m_ref.shape[1:])
        l_ref[i] = jnp.broadcast_to(jnp.sum(p0[i], axis=-1, keepdims=True), l_ref.shape[1:])
        acc_ref[i] = _dot(p0[i].astype(BF16), vs_ref[0, pl.ds(d_lo[i], 2 * SEL_BLOCK), :])
    q_aug = [jnp.concatenate([q, jnp.concatenate([b[h * qb:(h + 1) * qb] for h in range(NSA_KV_HEADS)
                                                  for _ in range(NSA_GROUP)], axis=0)], axis=1)
             for q, b in zip(qs, bias_q)]
    n_chunks = (blk[-1] + SEL_KEY_CHUNK // SEL_BLOCK - 2) // (SEL_KEY_CHUNK // SEL_BLOCK)

    def sel_step(c, _):
        k0 = pl.multiple_of(c * SEL_KEY_CHUNK, SEL_KEY_CHUNK)
        k_aug = jnp.concatenate([ks_ref[0, pl.ds(k0, SEL_KEY_CHUNK), :], oh_ref[pl.ds(k0, SEL_KEY_CHUNK), :]],
                                axis=1)
        v_blk = vs_ref[0, pl.ds(k0, SEL_KEY_CHUNK), :]
        s = [_dot_nt(q, k_aug) for q in q_aug]
        for i in blocks:
            m_prev = m_ref[i]
            m_new = jnp.maximum(m_prev, _row_max(s[i]))
            p = jnp.exp2(s[i] - jnp.concatenate([m_new] * (SEL_KEY_CHUNK // LANES), axis=1))
            alpha = jnp.exp2(m_prev - m_new)
            m_ref[i] = m_new
            l_ref[i] = alpha * l_ref[i] + _row_sum(p)
            acc_ref[i] = alpha * acc_ref[i] + _dot(p.astype(BF16), v_blk)
        return 0

    head_rows = lambda x, hq: x[hq * qb:(hq + 1) * qb]
    partial = [[gcol[i][3 * hq] * head_rows(o_cmp[i], hq) + gcol[i][3 * hq + 2] * head_rows(o_win[i], hq)
                for hq in range(NSA_HEADS)] for i in blocks]
    lax.fori_loop(0, n_chunks, sel_step, 0)

    for i in blocks:
        o_sel = acc_ref[i] / jnp.maximum(l_ref[i], 1e-30)
        mixed = [partial[i][hq] + gcol[i][3 * hq + 1] * head_rows(o_sel, hq) for hq in range(NSA_HEADS)]
        for pair in range(NSA_HEADS // 2):
            a, b = mixed[2 * pair], mixed[2 * pair + 1]
            if 2 * pair // NSA_GROUP == 0:
                b = pltpu.roll(b, NSA_DH, axis=1)
            else:
                a = pltpu.roll(a, NSA_DH, axis=1)
            o_ref[0, i * qb:(i + 1) * qb, pair * LANES:(pair + 1) * LANES] = jnp.where(lane < NSA_DH, a, b)


def _nsa_attention(q, sm, kc, vc, kv):
    b, s, qw = q.shape
    tq = NSA_STEP_BLOCKS * SEL_BLOCK
    ncmp = kc.shape[1]
    onehot = (jnp.arange(s)[:, None] // SEL_BLOCK == jnp.arange(LANES)[None, :]).astype(BF16)
    kv_spec = lambda col: pl.BlockSpec((1, s, LANES), lambda i, j: (i, 0, col))
    return pl.pallas_call(
        _nsa_kernel,
        grid=(b, s // tq),
        in_specs=[pl.BlockSpec((1, tq, qw), lambda i, j: (i, j, 0)),
                  pl.BlockSpec((1, tq, LANES), lambda i, j: (i, j, 0)),
                  pl.BlockSpec((1, ncmp, LANES), lambda i, j: (i, 0, 0)),
                  pl.BlockSpec((1, ncmp, LANES), lambda i, j: (i, 0, 0)),
                  kv_spec(2), kv_spec(3), kv_spec(4), kv_spec(5), _const_spec((s, LANES))],
        out_specs=pl.BlockSpec((1, tq, NSA_WIDTH), lambda i, j: (i, j, 0)),
        out_shape=jax.ShapeDtypeStruct((b, s, NSA_WIDTH), F32),
        scratch_shapes=[pltpu.VMEM((NSA_STEP_BLOCKS, NSA_HEADS * SEL_BLOCK, LANES), F32)] * 3,
        compiler_params=_params("parallel", "arbitrary"),
        name="nsa_attention",
    )(q, sm, kc, vc, kv, kv, kv, kv, onehot)


def _unit_lower_inverses(mats, r, c):
    n = mats[0].shape[0]
    eye = jnp.where(r == c, 1.0, 0.0)
    blk16 = (r >> 4) == (c >> 4)
    each = lambda f, *lists: [f(*args) for args in zip(*lists)]
    bf = lambda xs: [x.astype(BF16) for x in xs]
    stacked = lambda top, rhs: _dot(jnp.concatenate([top, rhs], axis=0), rhs)
    n1 = each(lambda a: jnp.where(blk16, -a, 0.0), mats)
    n1b = bf(n1)
    n2b = bf(each(_dot, n1b, n1b))
    t = each(lambda m: eye + m, n1)
    x = each(stacked, bf(t), n2b)
    t, n4b = each(lambda ti, xi: ti + xi[:n], t, x), bf(each(lambda xi: xi[n:], x))
    x = each(stacked, bf(t), n4b)
    t, n8b = each(lambda ti, xi: ti + xi[:n], t, x), bf(each(lambda xi: xi[n:], x))
    t = each(lambda ti, tb, m: ti + _dot(tb, m), t, bf(t), n8b)
    tb = bf(t)
    vb = bf(each(_dot, bf(each(lambda a: jnp.where(blk16, 0.0, a), mats)), tb))
    x = each(stacked, tb, vb)
    p, v2b = each(lambda ti, xi: ti - xi[:n], t, x), bf(each(lambda xi: xi[n:], x))
    return each(lambda pi, pb, m: pi + _dot(pb, m), p, bf(p), v2b)


def _log_decay(sm, alog_ref, dtb_ref):
    return -jnp.exp(alog_ref[...]) * jax.nn.softplus(sm + dtb_ref[...])


def _gdn_prep_kernel(x_ref, halo_ref, cw_ref, sm_ref, alog_ref, dtb_ref,
                     qd_ref, kdt_ref, at_ref, u_ref, w_ref, gcum_ref):
    n = GDN_PREP_TILE
    n_groups = x_ref.shape[1] // n
    width = cw_ref.shape[0]
    has_history = jnp.where(pl.program_id(1) > 0, 1.0, 0.0)

    def head_cols(c0):
        cols = slice(c0, c0 + GDN_DH)
        xe = jnp.concatenate([halo_ref[0, :, cols] * has_history, x_ref[0, :, cols]], axis=0)
        y = xe[GDN_HALO:] * cw_ref[width - 1:width, cols]
        for k in range(width - 1):
            y = y + pltpu.roll(xe, width - 1 - k, axis=0)[GDN_HALO:] * cw_ref[k:k + 1, cols]
        return jax.nn.silu(y)

    def l2norm(y):
        return y * lax.rsqrt(jnp.sum(y * y, axis=-1, keepdims=True) + EPS)

    r = lax.broadcasted_iota(jnp.int32, (n, n), 0)
    c = lax.broadcasted_iota(jnp.int32, (n, n), 1)
    in_chunk = (r >> GDN_CHUNK_SHIFT) == (c >> GDN_CHUNK_SHIFT)
    causal = in_chunk & (r >= c)
    hi = lax.Precision.HIGHEST
    pick = jnp.where(lax.broadcasted_iota(jnp.int32, (8, LANES), 1)
                     == lax.broadcasted_iota(jnp.int32, (8, LANES), 0) + SM_A0, 1.0, 0.0)
    q_all = [l2norm(head_cols(h * GDN_DH)) * (GDN_DH ** -0.5) for h in range(GDN_HEADS)]
    k_all = [l2norm(head_cols(GDN_WIDTH + h * GDN_DH)) for h in range(GDN_HEADS)]
    v_all = [head_cols(2 * GDN_WIDTH + h * GDN_DH) for h in range(GDN_HEADS)]

    a_mats, rhs, dest = [], [], []
    for grp in range(n_groups):
        tok = slice(grp * n, (grp + 1) * n)
        sm = sm_ref[0, tok]
        beta = jax.nn.sigmoid(sm)
        g = _log_decay(sm, alog_ref, dtb_ref)
        gcum = _dot(jnp.where(causal, 1.0, 0.0), g, precision=hi)
        gtot = _dot(jnp.where(in_chunk, 1.0, 0.0), g, precision=hi)
        gcum_ref[0, tok] = gcum
        gcum_rows = _dot_nt(pick, gcum, precision=hi)
        for h in range(GDN_HEADS):
            cols = slice(h * GDN_DH, (h + 1) * GDN_DH)
            q, k, v = q_all[h][tok], k_all[h][tok], v_all[h][tok]
            gc = _lane_col(gcum, SM_A0 + h)
            bcol = _lane_col(beta, SM_BETA0 + h)
            decay = jnp.exp(jnp.where(causal, gc - gcum_rows[h:h + 1, :], NEG_INF))
            kb = k * bcol
            egc = jnp.exp(gc)
            qd_ref[0, tok, cols] = (q * egc).astype(BF16)
            kdt_ref[0, cols, tok] = (k * jnp.exp(_lane_col(gtot, SM_A0 + h) - gc)).T.astype(BF16)
            grams = _dot_nt(jnp.concatenate([kb, q], axis=0).astype(BF16), k.astype(BF16))
            attn = (grams[n:] * decay).astype(BF16)
            for ch in range(n // GDN_CHUNK):
                lt = ch * GDN_CHUNK // LANES
                at_ref[0, grp * n + ch * GDN_CHUNK:grp * n + (ch + 1) * GDN_CHUNK, cols] = (
                    attn[ch * GDN_CHUNK:(ch + 1) * GDN_CHUNK, lt * LANES:(lt + 1) * LANES])
            a_mats.append(jnp.where(r > c, grams[:n] * decay, 0.0))
            rhs.append(jnp.concatenate([v * bcol, kb * egc], axis=1).astype(BF16))
            dest.append((tok, cols))
    for (tok, cols), t_inv, rh in zip(dest, _unit_lower_inverses(a_mats, r, c), rhs):
        uw = _dot(t_inv.astype(BF16), rh)
        u_ref[0, tok, cols] = uw[:, :GDN_DH]
        w_ref[0, tok, cols] = uw[:, GDN_DH:].astype(BF16)


def _gdn_prep(qkv, cw, sm, alog_vec, dtb_vec):
    b, s, width = qkv.shape
    n = GDN_PREP_GROUPS * GDN_PREP_TILE
    per = n // GDN_HALO
    tok = lambda w: pl.BlockSpec((1, n, w), lambda i, j: (i, j, 0))
    out = lambda dt: jax.ShapeDtypeStruct((b, s, GDN_WIDTH), dt)
    return pl.pallas_call(
        _gdn_prep_kernel,
        grid=(b, s // n),
        in_specs=[tok(width),
                  pl.BlockSpec((1, GDN_HALO, width), lambda i, j: (i, jnp.maximum(j * per - 1, 0), 0)),
                  _const_spec(cw.shape), tok(LANES), _const_spec((1, LANES)), _const_spec((1, LANES))],
        out_specs=[tok(GDN_WIDTH), pl.BlockSpec((1, GDN_WIDTH, n), lambda i, j: (i, 0, j))]
        + [tok(GDN_WIDTH)] * 3 + [tok(LANES)],
        out_shape=[out(BF16), jax.ShapeDtypeStruct((b, GDN_WIDTH, s), BF16), out(BF16), out(F32), out(BF16),
                   jax.ShapeDtypeStruct((b, s, LANES), F32)],
        compiler_params=_params("parallel", "parallel"),
        name="gdn_prep",
    )(qkv, qkv, cw, sm, alog_vec, dtb_vec)


def _gdn_scan_kernel(qd_ref, kdt_ref, at_ref, u_ref, w_ref, z_ref, gcum_ref, ng_ref, o_ref, state_ref):
    c = GDN_CHUNK
    n_chunks = qd_ref.shape[1] // c

    @pl.when(pl.program_id(1) == 0)
    def _():
        state_ref[...] = jnp.zeros_like(state_ref)

    def chunk_step(ci, _):
        t0 = pl.multiple_of(ci * c, c)
        rows = pl.ds(t0, c)
        odd = (ci & 1) == 1
        pair = pl.ds(pl.multiple_of((ci >> 1) * (2 * c), 2 * c), 2 * c)
        chains = [(bi, h) for bi in range(qd_ref.shape[0]) for h in range(GDN_HEADS)]
        cols = lambda h: slice(h * GDN_DH, (h + 1) * GDN_DH)
        states = [state_ref[bi * GDN_HEADS + h] for bi, h in chains]
        ws_qs = [_dot(jnp.concatenate([w_ref[bi, rows, cols(h)], qd_ref[bi, rows, cols(h)]], axis=0),
                      s.astype(BF16)) for (bi, h), s in zip(chains, states)]
        av_kv = []
        for (bi, h), x in zip(chains, ws_qs):
            v_new = (u_ref[bi, rows, cols(h)] - x[:c]).astype(BF16)
            zero = jnp.zeros_like(v_new)
            v_pad = jnp.concatenate([jnp.where(odd, zero, v_new), jnp.where(odd, v_new, zero)], axis=0)
            av_kv.append(_dot(jnp.concatenate([at_ref[bi, rows, cols(h)], kdt_ref[bi, cols(h), pair]], axis=0),
                              v_pad))
        for (bi, h), s, x, y in zip(chains, states, ws_qs, av_kv):
            state_decay = jnp.exp(gcum_ref[bi, pl.ds(t0 + c - 1, 1), :])
            state_ref[bi * GDN_HEADS + h] = s * _lane_col(state_decay, SM_A0 + h) + y[c:]
            o = x[c:] + y[:c]
            o = o * lax.rsqrt(jnp.mean(o * o, axis=-1, keepdims=True) + EPS) * ng_ref[...]
            o_ref[bi, rows, cols(h)] = o * jax.nn.silu(z_ref[bi, rows, cols(h)])
        return 0

    lax.fori_loop(0, n_chunks, chunk_step, 0)


def _gdn_scan(qd, kdt, at, u, w, z, gcum, norm_g):
    b, s, _ = qd.shape
    ts = GDN_SEQ_TILE
    nb = GDN_SCAN_BATCH if b % GDN_SCAN_BATCH == 0 else 1
    tok = lambda wd: pl.BlockSpec((nb, ts, wd), lambda i, j: (i, j, 0))
    return pl.pallas_call(
        _gdn_scan_kernel,
        grid=(b // nb, s // ts),
        in_specs=[tok(GDN_WIDTH), pl.BlockSpec((nb, GDN_WIDTH, ts), lambda i, j: (i, 0, j))]
        + [tok(GDN_WIDTH)] * 4 + [tok(LANES), _const_spec((1, GDN_DH))],
        out_specs=tok(GDN_WIDTH),
        out_shape=jax.ShapeDtypeStruct((b, s, GDN_WIDTH), F32),
        scratch_shapes=[pltpu.VMEM((nb * GDN_HEADS, GDN_DH, GDN_DH), F32)],
        compiler_params=_params("parallel", "arbitrary"),
        name="gdn_scan",
    )(qd, kdt, at, u, w, z, gcum, norm_g)


def _mix_ffn_kernel(x_ref, xh_ref, on_ref, onh_ref, og_ref, ogh_ref, gn_ref, wo_ref,
                    g_ref, wup_ref, cw_ref, wdn_ref, gf_ref, o_ref, *, final_norm):
    tm = x_ref.shape[1]
    d_ff = wdn_ref.shape[0]
    width = cw_ref.shape[0]
    has_history = jnp.where(pl.program_id(1) > 0, 1.0, 0.0)
    with_halo = lambda ref, halo: jnp.concatenate([halo[0] * has_history, ref[0]], axis=0)
    on = with_halo(on_ref, onh_ref)
    hn = (on * lax.rsqrt(jnp.mean(on * on, axis=-1, keepdims=True) + EPS) * gn_ref[...]).astype(BF16)
    mixed = jnp.concatenate([hn, with_halo(og_ref, ogh_ref).astype(BF16)], axis=1)
    xin = with_halo(x_ref, xh_ref) + _dot(mixed, wo_ref[...])
    x = xin[FFN_HALO:]
    ms = jnp.mean(xin * xin, axis=-1, keepdims=True)
    h = (xin * lax.rsqrt(ms + EPS) * g_ref[...]).astype(BF16)

    def conv(u, c0):
        y = u[FFN_HALO:] * cw_ref[width - 1:width, c0:c0 + FFN_COL_CHUNK]
        for k in range(width - 1):
            shifted = pltpu.roll(u, width - 1 - k, axis=0)[FFN_HALO:]
            y = y + shifted * cw_ref[k:k + 1, c0:c0 + FFN_COL_CHUNK]
        return y

    def up_proj(j):
        c0 = j * FFN_COL_CHUNK
        return (_dot(h, wup_ref[:, c0:c0 + FFN_COL_CHUNK]),
                _dot(h, wup_ref[:, d_ff + c0:d_ff + c0 + FFN_COL_CHUNK]))

    n_chunks = d_ff // FFN_COL_CHUNK
    y = x
    acts = []
    nxt = up_proj(0)
    for j in range(n_chunks):
        c0 = j * FFN_COL_CHUNK
        u_gate, u_up = nxt
        if j + 1 < n_chunks:
            nxt = up_proj(j + 1)
        acts.append((jax.nn.silu(conv(u_gate, c0)) * conv(u_up, d_ff + c0)).astype(BF16))
        if len(acts) == FFN_DOWN_GROUP or j + 1 == n_chunks:
            k0 = c0 + FFN_COL_CHUNK - len(acts) * FFN_COL_CHUNK
            y = y + _dot(jnp.concatenate(acts, axis=1), wdn_ref[k0:c0 + FFN_COL_CHUNK, :])
            acts = []
    if final_norm:
        y = y * lax.rsqrt(jnp.mean(y * y, axis=-1, keepdims=True) + EPS) * gf_ref[...]
    o_ref[0] = y


def _mix_ffn(x3, o_nsa, o_gdn, g_nsa, wo, g, wup, cw, wdn, g_final, final_norm):
    b, s, d = x3.shape
    tm = ROW_TILE
    per = tm // FFN_HALO
    rows = lambda w: pl.BlockSpec((1, tm, w), lambda i, j: (i, j, 0))
    halo = lambda w: pl.BlockSpec((1, FFN_HALO, w), lambda i, j: (i, jnp.maximum(j * per - 1, 0), 0))
    consts = (g_nsa, wo, g, wup, cw, wdn, g_final)
    return pl.pallas_call(
        functools.partial(_mix_ffn_kernel, final_norm=final_norm),
        grid=(b, s // tm),
        in_specs=[rows(d), halo(d), rows(o_nsa.shape[2]), halo(o_nsa.shape[2]),
                  rows(o_gdn.shape[2]), halo(o_gdn.shape[2])] + [_const_spec(c.shape) for c in consts],
        out_specs=rows(d),
        out_shape=jax.ShapeDtypeStruct((b, s, d), F32),
        compiler_params=_params("parallel", "parallel"),
        name="mix_convffn",
    )(x3, x3, o_nsa, o_nsa, o_gdn, o_gdn, *consts)


def _split_in_weights(w_in):
    d = w_in.shape[0]
    o = 0
    wq = w_in[:, o:o + NSA_WIDTH]; o += NSA_WIDTH
    wkv = w_in[:, o:o + 6 * NSA_KV_WIDTH]; o += 6 * NSA_KV_WIDTH
    wgate = w_in[:, o:o + 3 * NSA_HEADS]; o += 3 * NSA_HEADS
    wgdn = w_in[:, o:o + 3 * GDN_WIDTH]; o += 3 * GDN_WIDTH
    wz = w_in[:, o:o + GDN_WIDTH]; o += GDN_WIDTH
    wb = w_in[:, o:o + GDN_HEADS]; o += GDN_HEADS
    wa = w_in[:, o:o + GDN_HEADS]
    wq = wq.reshape(d, NSA_HEADS, NSA_DH) * (NSA_DH ** -0.5 * LOG2_E)
    zero = jnp.zeros_like(wq)
    kv_head = (jnp.arange(NSA_HEADS) // NSA_GROUP)[None, :, None]
    wq_pad = jnp.concatenate([jnp.where(kv_head == 0, wq, zero), jnp.where(kv_head == 1, wq, zero)], axis=-1)
    wq_pad = wq_pad.reshape(d, NSA_HEADS * LANES)
    wsm = jnp.concatenate([wgate, wb, wa, jnp.zeros((d, LANES - SM_A0 - GDN_HEADS), w_in.dtype)], axis=-1)
    return tuple(w.astype(BF16) for w in (wq_pad, wkv, wgdn, wz, wsm))


def _lane_vec(v, offset):
    return jnp.zeros((1, LANES), F32).at[0, offset:offset + v.shape[0]].set(v.astype(F32))


def _head_padded_w2(w2):
    z = jnp.zeros_like(w2)
    return jnp.stack([jnp.concatenate([w2, z], axis=-1), jnp.concatenate([z, w2], axis=-1)]).astype(BF16)


def kernel(x, norm_mix, w_in, cmp_pos_k, cmp_pos_v, cmp_k_w1, cmp_k_b1, cmp_k_w2, cmp_v_w1, cmp_v_b1, cmp_v_w2,
           nsa_norm, gdn_conv, gdn_a_log, gdn_dt_bias, gdn_norm, w_out, norm_ffn, ffn_up, ffn_conv, ffn_down,
           norm_final):
    b, s, d = x.shape
    depth = w_in.shape[0]
    group = CMP_BLOCK // 2
    assert CMP_STRIDE == group and s >= WIN_KEYS and s // SEL_BLOCK <= LANES
    assert s % max(ROW_TILE, SEL_KEY_CHUNK, GDN_PREP_TILE, GDN_SEQ_TILE, NSA_STEP_BLOCKS * SEL_BLOCK) == 0
    x2 = x.reshape(b * s, d)
    for l in range(depth):
        wq, wkv, wgdn, wz, wsm = _split_in_weights(w_in[l])
        q, kv, gdn, z, sm = _inproj(x2, norm_mix[l][None, :], wq, wkv, wgdn, wz, wsm)
        q, kv, gdn, z, sm = (a.reshape(b, s, a.shape[-1]) for a in (q, kv, gdn, z, sm))

        groups = kv[:, :, :2 * NSA_KV_WIDTH].reshape(b, s, 2 * NSA_KV_HEADS, NSA_DH)
        groups = groups.transpose(0, 2, 1, 3).reshape(b, 2 * NSA_KV_HEADS, s // group, group * NSA_DH)
        kc, vc = _compress(
            groups, cmp_pos_k[l].reshape(1, -1), cmp_pos_v[l].reshape(1, -1),
            cmp_k_w1[l].astype(BF16), cmp_k_b1[l][None, :], _head_padded_w2(cmp_k_w2[l]),
            cmp_v_w1[l].astype(BF16), cmp_v_b1[l][None, :], _head_padded_w2(cmp_v_w2[l]))
        o_nsa = _nsa_attention(q, sm, kc, vc, kv)

        alog_vec, dtb_vec = _lane_vec(gdn_a_log[l], SM_A0), _lane_vec(gdn_dt_bias[l], SM_A0)
        qd, kdt, at, gu, gw, gcum = _gdn_prep(gdn, gdn_conv[l], sm, alog_vec, dtb_vec)
        o_gdn = _gdn_scan(qd, kdt, at, gu, gw, z, gcum, gdn_norm[l][None, :])

        x2 = _mix_ffn(x2.reshape(b, s, d), o_nsa, o_gdn, nsa_norm[l][None, :], w_out[l].astype(BF16),
                      norm_ffn[l][None, :], ffn_up[l].astype(BF16), ffn_conv[l], ffn_down[l].astype(BF16),
                      norm_final[None, :], final_norm=(l == depth - 1)).reshape(b * s, d)
    return x2.reshape(b, s, d)
```

```python
import functools

import jax
import jax.numpy as jnp
from jax import lax
from jax.experimental import pallas as pl
from jax.experimental.pallas import tpu as pltpu

F32 = jnp.float32
BF16 = jnp.bfloat16
EPS = 1e-6
NEG_INF = float("-inf")
MASK_BIAS = -1e30
LOG2_E = 1.4426950408889634

NSA_HEADS = 8
NSA_KV_HEADS = 2
NSA_GROUP = NSA_HEADS // NSA_KV_HEADS
NSA_DH = 64
NSA_WIDTH = NSA_HEADS * NSA_DH
NSA_KV_WIDTH = NSA_KV_HEADS * NSA_DH
CMP_BLOCK = 32
CMP_STRIDE = 16
SEL_BLOCK = 64
N_SELECT = 8
WINDOW = 512
GDN_HEADS = 4
GDN_DH = 128
GDN_WIDTH = GDN_HEADS * GDN_DH
GDN_CHUNK = 64
GDN_CHUNK_SHIFT = GDN_CHUNK.bit_length() - 1

LANES = 128
VMEM_LIMIT_BYTES = 56 * 1024 * 1024

ROW_TILE = 512
FFN_HALO = 8
FFN_COL_CHUNK = 256
FFN_DOWN_GROUP = 4
SEL_KEY_CHUNK = 512
NSA_STEP_BLOCKS = 2
WIN_KEYS = WINDOW + 2 * SEL_BLOCK
GDN_SEQ_TILE = 256
GDN_SCAN_BATCH = 4
GDN_PREP_TILE = 256
GDN_PREP_GROUPS = 2
GDN_HALO = 8

SM_GATE0 = 0
SM_BETA0 = 3 * NSA_HEADS
SM_A0 = SM_BETA0 + GDN_HEADS


def _dot(a, b, precision=None):
    return jnp.dot(a, b, preferred_element_type=F32, precision=precision)


def _dot_nt(a, b, precision=None):
    return lax.dot_general(a, b, (((1,), (1,)), ((), ())), preferred_element_type=F32, precision=precision)


def _lane_col(x, idx):
    lane = lax.broadcasted_iota(jnp.int32, x.shape, 1)
    return jnp.sum(jnp.where(lane == idx, x, 0.0), axis=-1, keepdims=True)


def _const_spec(shape):
    zeros = (0,) * len(shape)
    return pl.BlockSpec(shape, lambda *_: zeros, pipeline_mode=pl.Buffered(1))


def _params(*semantics):
    return pltpu.CompilerParams(dimension_semantics=semantics, vmem_limit_bytes=VMEM_LIMIT_BYTES)


def _inproj_kernel(x_ref, g_ref, wq_ref, wkv_ref, wgdn_ref, wz_ref, wsm_ref,
                   q_ref, kv_ref, gdn_ref, z_ref, sm_ref, cmp_ref):
    x = x_ref[...]
    ms = jnp.mean(x * x, axis=-1, keepdims=True)
    h = (x * lax.rsqrt(ms + EPS) * g_ref[...]).astype(BF16)
    q_ref[...] = _dot(h, wq_ref[...]).astype(BF16)
    kv = _dot(h, wkv_ref[...])
    kv_ref[...] = kv.astype(BF16)
    for part in range(cmp_ref.shape[0]):
        cmp_ref[part] = kv[:, part * NSA_DH:(part + 1) * NSA_DH].astype(BF16)
    gdn_ref[...] = _dot(h, wgdn_ref[...])
    z_ref[...] = _dot(h, wz_ref[...])
    sm_ref[...] = _dot(h, wsm_ref[...])


def _inproj(x2, g, wq, wkv, wgdn, wz, wsm):
    n, d = x2.shape
    tm = ROW_TILE
    row = lambda w: pl.BlockSpec((tm, w), lambda i: (i, 0))
    widths = (wq.shape[1], wkv.shape[1], wgdn.shape[1], wz.shape[1], wsm.shape[1])
    dtypes = (BF16, BF16, F32, F32, F32)
    return pl.pallas_call(
        _inproj_kernel,
        grid=(n // tm,),
        in_specs=[row(d), _const_spec((1, d))] + [_const_spec(w.shape) for w in (wq, wkv, wgdn, wz, wsm)],
        out_specs=[row(w) for w in widths]
        + [pl.BlockSpec((2 * NSA_KV_HEADS, tm, NSA_DH), lambda i: (0, i, 0))],
        out_shape=[jax.ShapeDtypeStruct((n, w), dt) for w, dt in zip(widths, dtypes)]
        + [jax.ShapeDtypeStruct((2 * NSA_KV_HEADS, n, NSA_DH), BF16)],
        compiler_params=_params("parallel"),
        name="inproj",
    )(x2, g, wq, wkv, wgdn, wz, wsm)


def _compress_kernel(g_ref, posk_ref, posv_ref, w1k_ref, b1k_ref, w2k_ref, w1v_ref, b1v_ref, w2v_ref,
                     kc_ref, vc_ref):
    ng = g_ref.shape[2]
    half = w1k_ref.shape[0] // 2
    row = lax.broadcasted_iota(jnp.int32, (ng, 1), 0)
    for sel, (pos_ref, w1_ref, b1_ref, w2_ref, out_ref) in enumerate(
            ((posk_ref, w1k_ref, b1k_ref, w2k_ref, kc_ref), (posv_ref, w1v_ref, b1v_ref, w2v_ref, vc_ref))):
        w1 = w1_ref[...]
        pos8 = jnp.broadcast_to(pos_ref[...], (8, pos_ref.shape[1])).astype(BF16)
        bias = _dot(pos8, w1)[0:1, :] + b1_ref[...]
        acc = jnp.zeros((ng, LANES), F32)
        for h in range(NSA_KV_HEADS):
            grp = g_ref[sel * NSA_KV_HEADS + h, 0]
            top = _dot(grp, w1[:half])
            bot = _dot(grp, w1[half:])
            hid = top + pltpu.roll(bot, ng - 1, axis=0) + bias
            act = jax.nn.gelu(hid, approximate=True).astype(BF16)
            acc = acc + _dot(act, w2_ref[h])
        out_ref[0] = jnp.where(row < ng - 1, acc, 0.0).astype(BF16)


def _compress(groups, posk, posv, w1k, b1k, w2k, w1v, b1v, w2v):
    parts, b, ng, gw = groups.shape
    out = jax.ShapeDtypeStruct((b, ng, LANES), BF16)
    return pl.pallas_call(
        _compress_kernel,
        grid=(b,),
        in_specs=[pl.BlockSpec((parts, 1, ng, gw), lambda i: (0, i, 0, 0))]
        + [_const_spec(a.shape) for a in (posk, posv, w1k, b1k, w2k, w1v, b1v, w2v)],
        out_specs=[pl.BlockSpec((1, ng, LANES), lambda i: (i, 0, 0))] * 2,
        out_shape=[out, out],
        compiler_params=_params("parallel"),
        name="nsa_compress",
    )(groups, posk, posv, w1k, b1k, w2k, w1v, b1v, w2v)


def _row_reduce(x, combine, reduce):
    acc = x[:, :LANES]
    for t in range(1, x.shape[1] // LANES):
        acc = combine(acc, x[:, t * LANES:(t + 1) * LANES])
    return reduce(acc, axis=-1, keepdims=True)


def _row_max(x):
    return _row_reduce(x, jnp.maximum, jnp.max)


def _row_sum(x):
    return _row_reduce(x, jnp.add, jnp.sum)


def _softmax_parts(s, mask):
    sm = jnp.where(mask, s, NEG_INF)
    m = jnp.max(sm, axis=-1, keepdims=True)
    m = jnp.where(jnp.isfinite(m), m, 0.0)
    e = jnp.where(mask, jnp.exp2(sm - m), 0.0)
    return e, jnp.sum(e, axis=-1, keepdims=True)


def _nsa_kernel(q_ref, sm_ref, kc_ref, vc_ref, ks_ref, vs_ref, kw_ref, vw_ref, oh_ref, o_ref,
                m_ref, l_ref, acc_ref):
    qb = SEL_BLOCK
    nq = q_ref.shape[1] // qb
    seq = ks_ref.shape[1]
    ncmp = kc_ref.shape[1]
    nsb = seq // SEL_BLOCK
    both = NSA_KV_HEADS * qb
    blocks = range(nq)
    each = lambda f, *lists: [f(*args) for args in zip(*lists)]
    blk = [pl.program_id(1) * nq + i for i in blocks]
    s0 = [b * qb for b in blk]

    lane = lax.broadcasted_iota(jnp.int32, (qb, LANES), 1)
    t_q = [s + lax.broadcasted_iota(jnp.int32, (qb, 1), 0) for s in s0]
    per_head = lambda x: jnp.concatenate([x] * NSA_HEADS, axis=0)
    jn = lax.broadcasted_iota(jnp.int32, (nsb, ncmp), 0)
    nn = lax.broadcasted_iota(jnp.int32, (nsb, ncmp), 1)
    overlap_t = jnp.where((nn * CMP_STRIDE < jn * SEL_BLOCK + SEL_BLOCK)
                          & (jn * SEL_BLOCK < nn * CMP_STRIDE + CMP_BLOCK) & (nn < ncmp - 1), 1.0, 0.0)
    n_cmp_idx = lax.broadcasted_iota(jnp.int32, (1, ncmp), 1)
    jj = lax.broadcasted_iota(jnp.int32, (nsb, both), 0)
    eye = jnp.where(lax.broadcasted_iota(jnp.int32, (both, both), 0)
                    == lax.broadcasted_iota(jnp.int32, (both, both), 1), 1.0, 0.0).astype(BF16)

    gates = [jax.nn.sigmoid(sm_ref[0, i * qb:(i + 1) * qb]) for i in blocks]
    gcol = [[_lane_col(g, SM_GATE0 + j) for j in range(3 * NSA_HEADS)] for g in gates]

    qs = [jnp.concatenate([q_ref[0, i * qb:(i + 1) * qb, hq * LANES:(hq + 1) * LANES]
                           for hq in range(NSA_HEADS)], axis=0) for i in blocks]
    cmp_mask = [per_head(jnp.where((n_cmp_idx * CMP_STRIDE + (CMP_BLOCK - 1) <= t) & (n_cmp_idx < ncmp - 1),
                                   1.0, 0.0)) > 0.5 for t in t_q]
    d_lo = [pl.multiple_of(jnp.maximum(s - SEL_BLOCK, 0), SEL_BLOCK) for s in s0]
    diag_bias = [per_head(jnp.where(d + lax.broadcasted_iota(jnp.int32, (1, 2 * SEL_BLOCK), 1) <= t,
                                    0.0, MASK_BIAS)) for d, t in zip(d_lo, t_q)]
    w_lo = [pl.multiple_of(jnp.minimum(jnp.maximum(s - WINDOW, 0), seq - WIN_KEYS), SEL_BLOCK) for s in s0]
    kpos_w = [w + lax.broadcasted_iota(jnp.int32, (1, WIN_KEYS), 1) for w in w_lo]
    win_bias = [per_head(jnp.where((k <= t) & (k > t - WINDOW), 0.0, MASK_BIAS)) for k, t in zip(kpos_w, t_q)]

    s_c = [_dot_nt(q, kc_ref[0]) for q in qs]
    s_w = [_dot_nt(q, kw_ref[0, pl.ds(w, WIN_KEYS), :]) + b for q, w, b in zip(qs, w_lo, win_bias)]
    s_d = [_dot_nt(q, ks_ref[0, pl.ds(d, 2 * SEL_BLOCK), :]) + b for q, d, b in zip(qs, d_lo, diag_bias)]

    parts = each(_softmax_parts, s_c, cmp_mask)
    p_cmp = [e / jnp.maximum(l, 1e-30) for e, l in parts]
    o_cmp = [_dot(p.astype(BF16), vc_ref[0]) for p in p_cmp]

    def group_sums(p):
        out = []
        for h in range(NSA_KV_HEADS):
            acc = p[h * NSA_GROUP * qb:(h * NSA_GROUP + 1) * qb]
            for g in range(1, NSA_GROUP):
                acc = acc + p[(h * NSA_GROUP + g) * qb:(h * NSA_GROUP + g + 1) * qb]
            out.append(acc)
        return jnp.concatenate(out, axis=0)

    imp_t = [_dot_nt(overlap_t, group_sums(p), precision=lax.Precision.HIGHEST) for p in p_cmp]

    def selection_bias(imp, b):
        forced = (jj == 0) | (jj == b) | (jj == b - 1)
        score = jnp.where(forced, jnp.inf, jnp.where(jj <= b, imp, NEG_INF))
        rank = jnp.zeros((nsb, both), F32)
        for j2 in range(nsb):
            other = score[j2:j2 + 1, :]
            ahead = (other > score) | ((other == score) & (jj > j2))
            rank = rank + jnp.where(ahead, 1.0, 0.0)
        chosen = (rank < N_SELECT) & (score > NEG_INF) & (jj < b - 1)
        bias_t = jnp.concatenate([jnp.where(chosen, 0.0, MASK_BIAS), jnp.zeros((LANES - nsb, both), F32)], axis=0)
        return bias_t.astype(BF16)

    bias_t = each(selection_bias, imp_t, blk)
    bias_q = [_dot_nt(eye, b).astype(BF16) for b in bias_t]

    e_w = [jnp.exp2(s - _row_max(s)) for s in s_w]
    o_win = [_dot(e.astype(BF16), vw_ref[0, pl.ds(w, WIN_KEYS), :]) / jnp.maximum(_row_sum(e), 1e-30)
             for e, w in zip(e_w, w_lo)]

    m0 = [jnp.max(s, axis=-1, keepdims=True) for s in s_d]
    p0 = [jnp.exp2(s - m) for s, m in zip(s_d, m0)]
    for i in blocks:
        m_ref[i] = jnp.broadcast_to(m0[i], m_ref.shape[1:])
        l_ref[i] = jnp.broadcast_to(jnp.sum(p0[i], axis=-1, keepdims=True), l_ref.shape[1:])
        acc_ref[i] = _dot(p0[i].astype(BF16), vs_ref[0, pl.ds(d_lo[i], 2 * SEL_BLOCK), :])
    q_aug = [jnp.concatenate([q, jnp.concatenate([b[h * qb:(h + 1) * qb] for h in range(NSA_KV_HEADS)
                                                  for _ in range(NSA_GROUP)], axis=0)], axis=1)
             for q, b in zip(qs, bias_q)]
    n_chunks = (blk[-1] + SEL_KEY_CHUNK // SEL_BLOCK - 2) // (SEL_KEY_CHUNK // SEL_BLOCK)

    def sel_step(c, _):
        k0 = pl.multiple_of(c * SEL_KEY_CHUNK, SEL_KEY_CHUNK)
        k_aug = jnp.concatenate([ks_ref[0, pl.ds(k0, SEL_KEY_CHUNK), :], oh_ref[pl.ds(k0, SEL_KEY_CHUNK), :]],
                                axis=1)
        v_blk = vs_ref[0, pl.ds(k0, SEL_KEY_CHUNK), :]
        s = [_dot_nt(q, k_aug) for q in q_aug]
        for i in blocks:
            m_prev = m_ref[i]
            m_new = jnp.maximum(m_prev, _row_max(s[i]))
            p = jnp.exp2(s[i] - jnp.concatenate([m_new] * (SEL_KEY_CHUNK // LANES), axis=1))
            alpha = jnp.exp2(m_prev - m_new)
            m_ref[i] = m_new
            l_ref[i] = alpha * l_ref[i] + _row_sum(p)
            acc_ref[i] = alpha * acc_ref[i] + _dot(p.astype(BF16), v_blk)
        return 0

    head_rows = lambda x, hq: x[hq * qb:(hq + 1) * qb]
    partial = [[gcol[i][3 * hq] * head_rows(o_cmp[i], hq) + gcol[i][3 * hq + 2] * head_rows(o_win[i], hq)
                for hq in range(NSA_HEADS)] for i in blocks]
    lax.fori_loop(0, n_chunks, sel_step, 0)

    for i in blocks:
        o_sel = acc_ref[i] / jnp.maximum(l_ref[i], 1e-30)
        mixed = [partial[i][hq] + gcol[i][3 * hq + 1] * head_rows(o_sel, hq) for hq in range(NSA_HEADS)]
        for pair in range(NSA_HEADS // 2):
            a, b = mixed[2 * pair], mixed[2 * pair + 1]
            if 2 * pair // NSA_GROUP == 0:
                b = pltpu.roll(b, NSA_DH, axis=1)
            else:
                a = pltpu.roll(a, NSA_DH, axis=1)
            o_ref[0, i * qb:(i + 1) * qb, pair * LANES:(pair + 1) * LANES] = jnp.where(lane < NSA_DH, a, b)


def _nsa_attention(q, sm, kc, vc, kv):
    b, s, qw = q.shape
    tq = NSA_STEP_BLOCKS * SEL_BLOCK
    ncmp = kc.shape[1]
    onehot = (jnp.arange(s)[:, None] // SEL_BLOCK == jnp.arange(LANES)[None, :]).astype(BF16)
    kv_spec = lambda col: pl.BlockSpec((1, s, LANES), lambda i, j: (i, 0, col))
    return pl.pallas_call(
        _nsa_kernel,
        grid=(b, s // tq),
        in_specs=[pl.BlockSpec((1, tq, qw), lambda i, j: (i, j, 0)),
                  pl.BlockSpec((1, tq, LANES), lambda i, j: (i, j, 0)),
                  pl.BlockSpec((1, ncmp, LANES), lambda i, j: (i, 0, 0)),
                  pl.BlockSpec((1, ncmp, LANES), lambda i, j: (i, 0, 0)),
                  kv_spec(2), kv_spec(3), kv_spec(4), kv_spec(5), _const_spec((s, LANES))],
        out_specs=pl.BlockSpec((1, tq, NSA_WIDTH), lambda i, j: (i, j, 0)),
        out_shape=jax.ShapeDtypeStruct((b, s, NSA_WIDTH), F32),
        scratch_shapes=[pltpu.VMEM((NSA_STEP_BLOCKS, NSA_HEADS * SEL_BLOCK, LANES), F32)] * 3,
        compiler_params=_params("parallel", "arbitrary"),
        name="nsa_attention",
    )(q, sm, kc, vc, kv, kv, kv, kv, onehot)


def _unit_lower_inverses(mats, r, c):
    n = mats[0].shape[0]
    eye = jnp.where(r == c, 1.0, 0.0)
    blk16 = (r >> 4) == (c >> 4)
    each = lambda f, *lists: [f(*args) for args in zip(*lists)]
    bf = lambda xs: [x.astype(BF16) for x in xs]
    stacked = lambda top, rhs: _dot(jnp.concatenate([top, rhs], axis=0), rhs)
    n1 = each(lambda a: jnp.where(blk16, -a, 0.0), mats)
    n1b = bf(n1)
    n2b = bf(each(_dot, n1b, n1b))
    t = each(lambda m: eye + m, n1)
    x = each(stacked, bf(t), n2b)
    t, n4b = each(lambda ti, xi: ti + xi[:n], t, x), bf(each(lambda xi: xi[n:], x))
    x = each(stacked, bf(t), n4b)
    t, n8b = each(lambda ti, xi: ti + xi[:n], t, x), bf(each(lambda xi: xi[n:], x))
    t = each(lambda ti, tb, m: ti + _dot(tb, m), t, bf(t), n8b)
    tb = bf(t)
    vb = bf(each(_dot, bf(each(lambda a: jnp.where(blk16, 0.0, a), mats)), tb))
    x = each(stacked, tb, vb)
    p, v2b = each(lambda ti, xi: ti - xi[:n], t, x), bf(each(lambda xi: xi[n:], x))
    return each(lambda pi, pb, m: pi + _dot(pb, m), p, bf(p), v2b)


def _log_decay(sm, alog_ref, dtb_ref):
    return -jnp.exp(alog_ref[...]) * jax.nn.softplus(sm + dtb_ref[...])


def _gdn_prep_kernel(x_ref, halo_ref, cw_ref, sm_ref, alog_ref, dtb_ref,
                     qd_ref, kdt_ref, at_ref, u_ref, w_ref, gcum_ref):
    n = GDN_PREP_TILE
    n_groups = x_ref.shape[1] // n
    width = cw_ref.shape[0]
    has_history = jnp.where(pl.program_id(1) > 0, 1.0, 0.0)

    def head_cols(c0):
        cols = slice(c0, c0 + GDN_DH)
        xe = jnp.concatenate([halo_ref[0, :, cols] * has_history, x_ref[0, :, cols]], axis=0)
        y = xe[GDN_HALO:] * cw_ref[width - 1:width, cols]
        for k in range(width - 1):
            y = y + pltpu.roll(xe, width - 1 - k, axis=0)[GDN_HALO:] * cw_ref[k:k + 1, cols]
        return jax.nn.silu(y)

    def l2norm(y):
        return y * lax.rsqrt(jnp.sum(y * y, axis=-1, keepdims=True) + EPS)

    r = lax.broadcasted_iota(jnp.int32, (n, n), 0)
    c = lax.broadcasted_iota(jnp.int32, (n, n), 1)
    in_chunk = (r >> GDN_CHUNK_SHIFT) == (c >> GDN_CHUNK_SHIFT)
    causal = in_chunk & (r >= c)
    hi = lax.Precision.HIGHEST
    pick = jnp.where(lax.broadcasted_iota(jnp.int32, (8, LANES), 1)
                     == lax.broadcasted_iota(jnp.int32, (8, LANES), 0) + SM_A0, 1.0, 0.0)
    q_all = [l2norm(head_cols(h * GDN_DH)) * (GDN_DH ** -0.5) for h in range(GDN_HEADS)]
    k_all = [l2norm(head_cols(GDN_WIDTH + h * GDN_DH)) for h in range(GDN_HEADS)]
    v_all = [head_cols(2 * GDN_WIDTH + h * GDN_DH) for h in range(GDN_HEADS)]

    a_mats, rhs, dest = [], [], []
    for grp in range(n_groups):
        tok = slice(grp * n, (grp + 1) * n)
        sm = sm_ref[0, tok]
        beta = jax.nn.sigmoid(sm)
        g = _log_decay(sm, alog_ref, dtb_ref)
        gcum = _dot(jnp.where(causal, 1.0, 0.0), g, precision=hi)
        gtot = _dot(jnp.where(in_chunk, 1.0, 0.0), g, precision=hi)
        gcum_ref[0, tok] = gcum
        gcum_rows = _dot_nt(pick, gcum, precision=hi)
        for h in range(GDN_HEADS):
            cols = slice(h * GDN_DH, (h + 1) * GDN_DH)
            q, k, v = q_all[h][tok], k_all[h][tok], v_all[h][tok]
            gc = _lane_col(gcum, SM_A0 + h)
            bcol = _lane_col(beta, SM_BETA0 + h)
            decay = jnp.exp(jnp.where(causal, gc - gcum_rows[h:h + 1, :], NEG_INF))
            kb = k * bcol
            egc = jnp.exp(gc)
            qd_ref[0, tok, cols] = (q * egc).astype(BF16)
            kdt_ref[0, cols, tok] = (k * jnp.exp(_lane_col(gtot, SM_A0 + h) - gc)).T.astype(BF16)
            grams = _dot_nt(jnp.concatenate([kb, q], axis=0).astype(BF16), k.astype(BF16))
            attn = (grams[n:] * decay).astype(BF16)
            for ch in range(n // GDN_CHUNK):
                lt = ch * GDN_CHUNK // LANES
                at_ref[0, grp * n + ch * GDN_CHUNK:grp * n + (ch + 1) * GDN_CHUNK, cols] = (
                    attn[ch * GDN_CHUNK:(ch + 1) * GDN_CHUNK, lt * LANES:(lt + 1) * LANES])
            a_mats.append(jnp.where(r > c, grams[:n] * decay, 0.0))
            rhs.append(jnp.concatenate([v * bcol, kb * egc], axis=1).astype(BF16))
            dest.append((tok, cols))
    for (tok, cols), t_inv, rh in zip(dest, _unit_lower_inverses(a_mats, r, c), rhs):
        uw = _dot(t_inv.astype(BF16), rh)
        u_ref[0, tok, cols] = uw[:, :GDN_DH]
        w_ref[0, tok, cols] = uw[:, GDN_DH:].astype(BF16)


def _gdn_prep(qkv, cw, sm, alog_vec, dtb_vec):
    b, s, width = qkv.shape
    n = GDN_PREP_GROUPS * GDN_PREP_TILE
    per = n // GDN_HALO
    tok = lambda w: pl.BlockSpec((1, n, w), lambda i, j: (i, j, 0))
    out = lambda dt: jax.ShapeDtypeStruct((b, s, GDN_WIDTH), dt)
    return pl.pallas_call(
        _gdn_prep_kernel,
        grid=(b, s // n),
        in_specs=[tok(width),
                  pl.BlockSpec((1, GDN_HALO, width), lambda i, j: (i, jnp.maximum(j * per - 1, 0), 0)),
                  _const_spec(cw.shape), tok(LANES), _const_spec((1, LANES)), _const_spec((1, LANES))],
        out_specs=[tok(GDN_WIDTH), pl.BlockSpec((1, GDN_WIDTH, n), lambda i, j: (i, 0, j))]
        + [tok(GDN_WIDTH)] * 3 + [tok(LANES)],
        out_shape=[out(BF16), jax.ShapeDtypeStruct((b, GDN_WIDTH, s), BF16), out(BF16), out(F32), out(BF16),
                   jax.ShapeDtypeStruct((b, s, LANES), F32)],
        compiler_params=_params("parallel", "parallel"),
        name="gdn_prep",
    )(qkv, qkv, cw, sm, alog_vec, dtb_vec)


def _gdn_scan_kernel(qd_ref, kdt_ref, at_ref, u_ref, w_ref, z_ref, gcum_ref, ng_ref, o_ref, state_ref):
    c = GDN_CHUNK
    n_chunks = qd_ref.shape[1] // c

    @pl.when(pl.program_id(1) == 0)
    def _():
        state_ref[...] = jnp.zeros_like(state_ref)

    def chunk_step(ci, _):
        t0 = pl.multiple_of(ci * c, c)
        rows = pl.ds(t0, c)
        odd = (ci & 1) == 1
        pair = pl.ds(pl.multiple_of((ci >> 1) * (2 * c), 2 * c), 2 * c)
        chains = [(bi, h) for bi in range(qd_ref.shape[0]) for h in range(GDN_HEADS)]
        cols = lambda h: slice(h * GDN_DH, (h + 1) * GDN_DH)
        states = [state_ref[bi * GDN_HEADS + h] for bi, h in chains]
        ws_qs = [_dot(jnp.concatenate([w_ref[bi, rows, cols(h)], qd_ref[bi, rows, cols(h)]], axis=0),
                      s.astype(BF16)) for (bi, h), s in zip(chains, states)]
        av_kv = []
        for (bi, h), x in zip(chains, ws_qs):
            v_new = (u_ref[bi, rows, cols(h)] - x[:c]).astype(BF16)
            zero = jnp.zeros_like(v_new)
            v_pad = jnp.concatenate([jnp.where(odd, zero, v_new), jnp.where(odd, v_new, zero)], axis=0)
            av_kv.append(_dot(jnp.concatenate([at_ref[bi, rows, cols(h)], kdt_ref[bi, cols(h), pair]], axis=0),
                              v_pad))
        for (bi, h), s, x, y in zip(chains, states, ws_qs, av_kv):
            state_decay = jnp.exp(gcum_ref[bi, pl.ds(t0 + c - 1, 1), :])
            state_ref[bi * GDN_HEADS + h] = s * _lane_col(state_decay, SM_A0 + h) + y[c:]
            o = x[c:] + y[:c]
            o = o * lax.rsqrt(jnp.mean(o * o, axis=-1, keepdims=True) + EPS) * ng_ref[...]
            o_ref[bi, rows, cols(h)] = o * jax.nn.silu(z_ref[bi, rows, cols(h)])
        return 0

    lax.fori_loop(0, n_chunks, chunk_step, 0)


def _gdn_scan(qd, kdt, at, u, w, z, gcum, norm_g):
    b, s, _ = qd.shape
    ts = GDN_SEQ_TILE
    nb = GDN_SCAN_BATCH if b % GDN_SCAN_BATCH == 0 else 1
    tok = lambda wd: pl.BlockSpec((nb, ts, wd), lambda i, j: (i, j, 0))
    return pl.pallas_call(
        _gdn_scan_kernel,
        grid=(b // nb, s // ts),
        in_specs=[tok(GDN_WIDTH), pl.BlockSpec((nb, GDN_WIDTH, ts), lambda i, j: (i, 0, j))]
        + [tok(GDN_WIDTH)] * 4 + [tok(LANES), _const_spec((1, GDN_DH))],
        out_specs=tok(GDN_WIDTH),
        out_shape=jax.ShapeDtypeStruct((b, s, GDN_WIDTH), F32),
        scratch_shapes=[pltpu.VMEM((nb * GDN_HEADS, GDN_DH, GDN_DH), F32)],
        compiler_params=_params("parallel", "arbitrary"),
        name="gdn_scan",
    )(qd, kdt, at, u, w, z, gcum, norm_g)


def _mix_ffn_kernel(x_ref, xh_ref, on_ref, onh_ref, og_ref, ogh_ref, gn_ref, wo_ref,
                    g_ref, wup_ref, cw_ref, wdn_ref, gf_ref, o_ref, *, final_norm):
    tm = x_ref.shape[1]
    d_ff = wdn_ref.shape[0]
    width = cw_ref.shape[0]
    has_history = jnp.where(pl.program_id(1) > 0, 1.0, 0.0)
    with_halo = lambda ref, halo: jnp.concatenate([halo[0] * has_history, ref[0]], axis=0)
    on = with_halo(on_ref, onh_ref)
    hn = (on * lax.rsqrt(jnp.mean(on * on, axis=-1, keepdims=True) + EPS) * gn_ref[...]).astype(BF16)
    mixed = jnp.concatenate([hn, with_halo(og_ref, ogh_ref).astype(BF16)], axis=1)
    xin = with_halo(x_ref, xh_ref) + _dot(mixed, wo_ref[...])
    x = xin[FFN_HALO:]
    ms = jnp.mean(xin * xin, axis=-1, keepdims=True)
    h = (xin * lax.rsqrt(ms + EPS) * g_ref[...]).astype(BF16)

    def conv(u, c0):
        y = u[FFN_HALO:] * cw_ref[width - 1:width, c0:c0 + FFN_COL_CHUNK]
        for k in range(width - 1):
            shifted = pltpu.roll(u, width - 1 - k, axis=0)[FFN_HALO:]
            y = y + shifted * cw_ref[k:k + 1, c0:c0 + FFN_COL_CHUNK]
        return y

    def up_proj(j):
        c0 = j * FFN_COL_CHUNK
        return (_dot(h, wup_ref[:, c0:c0 + FFN_COL_CHUNK]),
                _dot(h, wup_ref[:, d_ff + c0:d_ff + c0 + FFN_COL_CHUNK]))

    n_chunks = d_ff // FFN_COL_CHUNK
    y = x
    acts = []
    nxt = up_proj(0)
    for j in range(n_chunks):
        c0 = j * FFN_COL_CHUNK
        u_gate, u_up = nxt
        if j + 1 < n_chunks:
            nxt = up_proj(j + 1)
        acts.append((jax.nn.silu(conv(u_gate, c0)) * conv(u_up, d_ff + c0)).astype(BF16))
        if len(acts) == FFN_DOWN_GROUP or j + 1 == n_chunks:
            k0 = c0 + FFN_COL_CHUNK - len(acts) * FFN_COL_CHUNK
            y = y + _dot(jnp.concatenate(acts, axis=1), wdn_ref[k0:c0 + FFN_COL_CHUNK, :])
            acts = []
    if final_norm:
        y = y * lax.rsqrt(jnp.mean(y * y, axis=-1, keepdims=True) + EPS) * gf_ref[...]
    o_ref[0] = y


def _mix_ffn(x3, o_nsa, o_gdn, g_nsa, wo, g, wup, cw, wdn, g_final, final_norm):
    b, s, d = x3.shape
    tm = ROW_TILE
    per = tm // FFN_HALO
    rows = lambda w: pl.BlockSpec((1, tm, w), lambda i, j: (i, j, 0))
    halo = lambda w: pl.BlockSpec((1, FFN_HALO, w), lambda i, j: (i, jnp.maximum(j * per - 1, 0), 0))
    consts = (g_nsa, wo, g, wup, cw, wdn, g_final)
    return pl.pallas_call(
        functools.partial(_mix_ffn_kernel, final_norm=final_norm),
        grid=(b, s // tm),
        in_specs=[rows(d), halo(d), rows(o_nsa.shape[2]), halo(o_nsa.shape[2]),
                  rows(o_gdn.shape[2]), halo(o_gdn.shape[2])] + [_const_spec(c.shape) for c in consts],
        out_specs=rows(d),
        out_shape=jax.ShapeDtypeStruct((b, s, d), F32),
        compiler_params=_params("parallel", "parallel"),
        name="mix_convffn",
    )(x3, x3, o_nsa, o_nsa, o_gdn, o_gdn, *consts)


def _split_in_weights(w_in):
    d = w_in.shape[0]
    o = 0
    wq = w_in[:, o:o + NSA_WIDTH]; o += NSA_WIDTH
    wkv = w_in[:, o:o + 6 * NSA_KV_WIDTH]; o += 6 * NSA_KV_WIDTH
    wgate = w_in[:, o:o + 3 * NSA_HEADS]; o += 3 * NSA_HEADS
    wgdn = w_in[:, o:o + 3 * GDN_WIDTH]; o += 3 * GDN_WIDTH
    wz = w_in[:, o:o + GDN_WIDTH]; o += GDN_WIDTH
    wb = w_in[:, o:o + GDN_HEADS]; o += GDN_HEADS
    wa = w_in[:, o:o + GDN_HEADS]
    wq = wq.reshape(d, NSA_HEADS, NSA_DH) * (NSA_DH ** -0.5 * LOG2_E)
    zero = jnp.zeros_like(wq)
    kv_head = (jnp.arange(NSA_HEADS) // NSA_GROUP)[None, :, None]
    wq_pad = jnp.concatenate([jnp.where(kv_head == 0, wq, zero), jnp.where(kv_head == 1, wq, zero)], axis=-1)
    wq_pad = wq_pad.reshape(d, NSA_HEADS * LANES)
    wsm = jnp.concatenate([wgate, wb, wa, jnp.zeros((d, LANES - SM_A0 - GDN_HEADS), w_in.dtype)], axis=-1)
    return tuple(w.astype(BF16) for w in (wq_pad, wkv, wgdn, wz, wsm))


def _lane_vec(v, offset):
    return jnp.zeros((1, LANES), F32).at[0, offset:offset + v.shape[0]].set(v.astype(F32))


def _head_padded_w2(w2):
    z = jnp.zeros_like(w2)
    return jnp.stack([jnp.concatenate([w2, z], axis=-1), jnp.concatenate([z, w2], axis=-1)]).astype(BF16)


def kernel(x, norm_mix, w_in, cmp_pos_k, cmp_pos_v, cmp_k_w1, cmp_k_b1, cmp_k_w2, cmp_v_w1, cmp_v_b1, cmp_v_w2,
           nsa_norm, gdn_conv, gdn_a_log, gdn_dt_bias, gdn_norm, w_out, norm_ffn, ffn_up, ffn_conv, ffn_down,
           norm_final):
    b, s, d = x.shape
    depth = w_in.shape[0]
    group = CMP_BLOCK // 2
    assert CMP_STRIDE == group and s >= WIN_KEYS and s // SEL_BLOCK <= LANES
    assert s % max(ROW_TILE, SEL_KEY_CHUNK, GDN_PREP_TILE, GDN_SEQ_TILE, NSA_STEP_BLOCKS * SEL_BLOCK) == 0
    x2 = x.reshape(b * s, d)
    for l in range(depth):
        wq, wkv, wgdn, wz, wsm = _split_in_weights(w_in[l])
        q, kv, gdn, z, sm, cmp_kv = _inproj(x2, norm_mix[l][None, :], wq, wkv, wgdn, wz, wsm)
        q, kv, gdn, z, sm = (a.reshape(b, s, a.shape[-1]) for a in (q, kv, gdn, z, sm))

        groups = cmp_kv.reshape(2 * NSA_KV_HEADS, b, s // group, group * NSA_DH)
        kc, vc = _compress(
            groups, cmp_pos_k[l].reshape(1, -1), cmp_pos_v[l].reshape(1, -1),
            cmp_k_w1[l].astype(BF16), cmp_k_b1[l][None, :], _head_padded_w2(cmp_k_w2[l]),
            cmp_v_w1[l].astype(BF16), cmp_v_b1[l][None, :], _head_padded_w2(cmp_v_w2[l]))
        o_nsa = _nsa_attention(q, sm, kc, vc, kv)

        alog_vec, dtb_vec = _lane_vec(gdn_a_log[l], SM_A0), _lane_vec(gdn_dt_bias[l], SM_A0)
        qd, kdt, at, gu, gw, gcum = _gdn_prep(gdn, gdn_conv[l], sm, alog_vec, dtb_vec)
        o_gdn = _gdn_scan(qd, kdt, at, gu, gw, z, gcum, gdn_norm[l][None, :])

        x2 = _mix_ffn(x2.reshape(b, s, d), o_nsa, o_gdn, nsa_norm[l][None, :], w_out[l].astype(BF16),
                      norm_ffn[l][None, :], ffn_up[l].astype(BF16), ffn_conv[l], ffn_down[l].astype(BF16),
                      norm_final[None, :], final_norm=(l == depth - 1)).reshape(b * s, d)
    return x2.reshape(b, s, d)
```

```python
import functools

import jax
import jax.numpy as jnp
from jax import lax
from jax.experimental import pallas as pl
from jax.experimental.pallas import tpu as pltpu

F32 = jnp.float32
BF16 = jnp.bfloat16
EPS = 1e-6
NEG_INF = float("-inf")
MASK_BIAS = -1e30
LOG2_E = 1.4426950408889634

NSA_HEADS = 8
NSA_KV_HEADS = 2
NSA_GROUP = NSA_HEADS // NSA_KV_HEADS
NSA_DH = 64
NSA_WIDTH = NSA_HEADS * NSA_DH
NSA_KV_WIDTH = NSA_KV_HEADS * NSA_DH
CMP_BLOCK = 32
CMP_STRIDE = 16
SEL_BLOCK = 64
N_SELECT = 8
WINDOW = 512
GDN_HEADS = 4
GDN_DH = 128
GDN_WIDTH = GDN_HEADS * GDN_DH
GDN_CHUNK = 64
GDN_CHUNK_SHIFT = GDN_CHUNK.bit_length() - 1

LANES = 128
VMEM_LIMIT_BYTES = 56 * 1024 * 1024

ROW_TILE = 512
FFN_HALO = 8
FFN_COL_CHUNK = 256
FFN_DOWN_GROUP = 4
SEL_KEY_CHUNK = 512
NSA_STEP_BLOCKS = 2
WIN_KEYS = WINDOW + 2 * SEL_BLOCK
GDN_SEQ_TILE = 256
GDN_SCAN_BATCH = 4
GDN_PREP_TILE = 256
GDN_PREP_GROUPS = 2
GDN_HALO = 8

SM_GATE0 = 0
SM_BETA0 = 3 * NSA_HEADS
SM_A0 = SM_BETA0 + GDN_HEADS


def _dot(a, b, precision=None):
    return jnp.dot(a, b, preferred_element_type=F32, precision=precision)


def _dot_nt(a, b, precision=None):
    return lax.dot_general(a, b, (((1,), (1,)), ((), ())), preferred_element_type=F32, precision=precision)


def _lane_col(x, idx):
    lane = lax.broadcasted_iota(jnp.int32, x.shape, 1)
    return jnp.sum(jnp.where(lane == idx, x, 0.0), axis=-1, keepdims=True)


def _const_spec(shape):
    zeros = (0,) * len(shape)
    return pl.BlockSpec(shape, lambda *_: zeros, pipeline_mode=pl.Buffered(1))


def _params(*semantics):
    return pltpu.CompilerParams(dimension_semantics=semantics, vmem_limit_bytes=VMEM_LIMIT_BYTES)


def _inproj_kernel(x_ref, g_ref, wq_ref, wkv_ref, wgdn_ref, wz_ref, wsm_ref,
                   q_ref, kv_ref, gdn_ref, z_ref, sm_ref, cmp_ref):
    x = x_ref[...]
    ms = jnp.mean(x * x, axis=-1, keepdims=True)
    h = (x * lax.rsqrt(ms + EPS) * g_ref[...]).astype(BF16)
    q_ref[...] = _dot(h, wq_ref[...]).astype(BF16)
    kv = _dot(h, wkv_ref[...])
    kv_ref[...] = kv.astype(BF16)
    for part in range(cmp_ref.shape[0]):
        cmp_ref[part] = kv[:, part * NSA_DH:(part + 1) * NSA_DH].astype(BF16)
    gdn_ref[...] = _dot(h, wgdn_ref[...])
    z_ref[...] = _dot(h, wz_ref[...])
    sm_ref[...] = _dot(h, wsm_ref[...])


def _inproj(x2, g, wq, wkv, wgdn, wz, wsm):
    n, d = x2.shape
    tm = ROW_TILE
    row = lambda w: pl.BlockSpec((tm, w), lambda i: (i, 0))
    widths = (wq.shape[1], wkv.shape[1], wgdn.shape[1], wz.shape[1], wsm.shape[1])
    dtypes = (BF16, BF16, F32, F32, F32)
    return pl.pallas_call(
        _inproj_kernel,
        grid=(n // tm,),
        in_specs=[row(d), _const_spec((1, d))] + [_const_spec(w.shape) for w in (wq, wkv, wgdn, wz, wsm)],
        out_specs=[row(w) for w in widths]
        + [pl.BlockSpec((2 * NSA_KV_HEADS, tm, NSA_DH), lambda i: (0, i, 0))],
        out_shape=[jax.ShapeDtypeStruct((n, w), dt) for w, dt in zip(widths, dtypes)]
        + [jax.ShapeDtypeStruct((2 * NSA_KV_HEADS, n, NSA_DH), BF16)],
        compiler_params=_params("parallel"),
        name="inproj",
    )(x2, g, wq, wkv, wgdn, wz, wsm)


def _compress_kernel(g_ref, posk_ref, posv_ref, w1k_ref, b1k_ref, w2k_ref, w1v_ref, b1v_ref, w2v_ref,
                     kc_ref, vc_ref):
    ng = g_ref.shape[2]
    half = w1k_ref.shape[0] // 2
    row = lax.broadcasted_iota(jnp.int32, (ng, 1), 0)
    for sel, (pos_ref, w1_ref, b1_ref, w2_ref, out_ref) in enumerate(
            ((posk_ref, w1k_ref, b1k_ref, w2k_ref, kc_ref), (posv_ref, w1v_ref, b1v_ref, w2v_ref, vc_ref))):
        w1 = w1_ref[...]
        pos8 = jnp.broadcast_to(pos_ref[...], (8, pos_ref.shape[1])).astype(BF16)
        bias = _dot(pos8, w1)[0:1, :] + b1_ref[...]
        acc = jnp.zeros((ng, LANES), F32)
        for h in range(NSA_KV_HEADS):
            grp = g_ref[sel * NSA_KV_HEADS + h, 0]
            top = _dot(grp, w1[:half])
            bot = _dot(grp, w1[half:])
            hid = top + pltpu.roll(bot, ng - 1, axis=0) + bias
            act = jax.nn.gelu(hid, approximate=True).astype(BF16)
            acc = acc + _dot(act, w2_ref[h])
        out_ref[0] = jnp.where(row < ng - 1, acc, 0.0).astype(BF16)


def _compress(groups, posk, posv, w1k, b1k, w2k, w1v, b1v, w2v):
    parts, b, ng, gw = groups.shape
    out = jax.ShapeDtypeStruct((b, ng, LANES), BF16)
    return pl.pallas_call(
        _compress_kernel,
        grid=(b,),
        in_specs=[pl.BlockSpec((parts, 1, ng, gw), lambda i: (0, i, 0, 0))]
        + [_const_spec(a.shape) for a in (posk, posv, w1k, b1k, w2k, w1v, b1v, w2v)],
        out_specs=[pl.BlockSpec((1, ng, LANES), lambda i: (i, 0, 0))] * 2,
        out_shape=[out, out],
        compiler_params=_params("parallel"),
        name="nsa_compress",
    )(groups, posk, posv, w1k, b1k, w2k, w1v, b1v, w2v)


def _row_reduce(x, combine, reduce):
    acc = x[:, :LANES]
    for t in range(1, x.shape[1] // LANES):
        acc = combine(acc, x[:, t * LANES:(t + 1) * LANES])
    return reduce(acc, axis=-1, keepdims=True)


def _row_max(x):
    return _row_reduce(x, jnp.maximum, jnp.max)


def _row_sum(x):
    return _row_reduce(x, jnp.add, jnp.sum)


def _softmax_parts(s, mask):
    sm = jnp.where(mask, s, NEG_INF)
    m = jnp.max(sm, axis=-1, keepdims=True)
    m = jnp.where(jnp.isfinite(m), m, 0.0)
    e = jnp.where(mask, jnp.exp2(sm - m), 0.0)
    return e, jnp.sum(e, axis=-1, keepdims=True)


def _nsa_kernel(q_ref, sm_ref, kc_ref, vc_ref, ks_ref, vs_ref, kw_ref, vw_ref, oh_ref, o_ref,
                m_ref, l_ref, acc_ref):
    qb = SEL_BLOCK
    nq = q_ref.shape[1] // qb
    seq = ks_ref.shape[1]
    ncmp = kc_ref.shape[1]
    nsb = seq // SEL_BLOCK
    both = NSA_KV_HEADS * qb
    blocks = range(nq)
    each = lambda f, *lists: [f(*args) for args in zip(*lists)]
    blk = [pl.program_id(1) * nq + i for i in blocks]
    s0 = [b * qb for b in blk]

    lane = lax.broadcasted_iota(jnp.int32, (qb, LANES), 1)
    t_q = [s + lax.broadcasted_iota(jnp.int32, (qb, 1), 0) for s in s0]
    per_head = lambda x: jnp.concatenate([x] * NSA_HEADS, axis=0)
    jn = lax.broadcasted_iota(jnp.int32, (nsb, ncmp), 0)
    nn = lax.broadcasted_iota(jnp.int32, (nsb, ncmp), 1)
    overlap_t = jnp.where((nn * CMP_STRIDE < jn * SEL_BLOCK + SEL_BLOCK)
                          & (jn * SEL_BLOCK < nn * CMP_STRIDE + CMP_BLOCK) & (nn < ncmp - 1), 1.0, 0.0)
    n_cmp_idx = lax.broadcasted_iota(jnp.int32, (1, ncmp), 1)
    jj = lax.broadcasted_iota(jnp.int32, (nsb, both), 0)
    eye = jnp.where(lax.broadcasted_iota(jnp.int32, (both, both), 0)
                    == lax.broadcasted_iota(jnp.int32, (both, both), 1), 1.0, 0.0).astype(BF16)

    gates = [jax.nn.sigmoid(sm_ref[0, i * qb:(i + 1) * qb]) for i in blocks]
    gcol = [[_lane_col(g, SM_GATE0 + j) for j in range(3 * NSA_HEADS)] for g in gates]

    qs = [jnp.concatenate([q_ref[0, i * qb:(i + 1) * qb, hq * LANES:(hq + 1) * LANES]
                           for hq in range(NSA_HEADS)], axis=0) for i in blocks]
    cmp_mask = [per_head(jnp.where((n_cmp_idx * CMP_STRIDE + (CMP_BLOCK - 1) <= t) & (n_cmp_idx < ncmp - 1),
                                   1.0, 0.0)) > 0.5 for t in t_q]
    d_lo = [pl.multiple_of(jnp.maximum(s - SEL_BLOCK, 0), SEL_BLOCK) for s in s0]
    diag_bias = [per_head(jnp.where(d + lax.broadcasted_iota(jnp.int32, (1, 2 * SEL_BLOCK), 1) <= t,
                                    0.0, MASK_BIAS)) for d, t in zip(d_lo, t_q)]
    w_lo = [pl.multiple_of(jnp.minimum(jnp.maximum(s - WINDOW, 0), seq - WIN_KEYS), SEL_BLOCK) for s in s0]
    kpos_w = [w + lax.broadcasted_iota(jnp.int32, (1, WIN_KEYS), 1) for w in w_lo]
    win_bias = [per_head(jnp.where((k <= t) & (k > t - WINDOW), 0.0, MASK_BIAS)) for k, t in zip(kpos_w, t_q)]

    s_c = [_dot_nt(q, kc_ref[0]) for q in qs]
    s_w = [_dot_nt(q, kw_ref[0, pl.ds(w, WIN_KEYS), :]) + b for q, w, b in zip(qs, w_lo, win_bias)]
    s_d = [_dot_nt(q, ks_ref[0, pl.ds(d, 2 * SEL_BLOCK), :]) + b for q, d, b in zip(qs, d_lo, diag_bias)]

    parts = each(_softmax_parts, s_c, cmp_mask)
    p_cmp = [e / jnp.maximum(l, 1e-30) for e, l in parts]
    o_cmp = [_dot(p.astype(BF16), vc_ref[0]) for p in p_cmp]

    def group_sums(p):
        out = []
        for h in range(NSA_KV_HEADS):
            acc = p[h * NSA_GROUP * qb:(h * NSA_GROUP + 1) * qb]
            for g in range(1, NSA_GROUP):
                acc = acc + p[(h * NSA_GROUP + g) * qb:(h * NSA_GROUP + g + 1) * qb]
            out.append(acc)
        return jnp.concatenate(out, axis=0)

    imp_t = [_dot_nt(overlap_t, group_sums(p), precision=lax.Precision.HIGHEST) for p in p_cmp]

    def selection_bias(imp, b):
        forced = (jj == 0) | (jj == b) | (jj == b - 1)
        score = jnp.where(forced, jnp.inf, jnp.where(jj <= b, imp, NEG_INF))
        rank = jnp.zeros((nsb, both), F32)
        for j2 in range(nsb):
            other = score[j2:j2 + 1, :]
            ahead = (other > score) | ((other == score) & (jj > j2))
            rank = rank + jnp.where(ahead, 1.0, 0.0)
        chosen = (rank < N_SELECT) & (score > NEG_INF) & (jj < b - 1)
        bias_t = jnp.concatenate([jnp.where(chosen, 0.0, MASK_BIAS), jnp.zeros((LANES - nsb, both), F32)], axis=0)
        return bias_t.astype(BF16)

    bias_t = each(selection_bias, imp_t, blk)
    bias_q = [_dot_nt(eye, b).astype(BF16) for b in bias_t]

    e_w = [jnp.exp2(s - _row_max(s)) for s in s_w]
    o_win = [_dot(e.astype(BF16), vw_ref[0, pl.ds(w, WIN_KEYS), :]) / jnp.maximum(_row_sum(e), 1e-30)
             for e, w in zip(e_w, w_lo)]

    m0 = [jnp.max(s, axis=-1, keepdims=True) for s in s_d]
    p0 = [jnp.exp2(s - m) for s, m in zip(s_d, m0)]
    for i in blocks:
        m_ref[i] = jnp.broadcast_to(m0[i], m_ref.shape[1:])
        l_ref[i] = jnp.broadcast_to(jnp.sum(p0[i], axis=-1, keepdims=True), l_ref.shape[1:])
        acc_ref[i] = _dot(p0[i].astype(BF16), vs_ref[0, pl.ds(d_lo[i], 2 * SEL_BLOCK), :])
    q_aug = [jnp.concatenate([q, jnp.concatenate([b[h * qb:(h + 1) * qb] for h in range(NSA_KV_HEADS)
                                                  for _ in range(NSA_GROUP)], axis=0)], axis=1)
             for q, b in zip(qs, bias_q)]
    n_chunks = (blk[-1] + SEL_KEY_CHUNK // SEL_BLOCK - 2) // (SEL_KEY_CHUNK // SEL_BLOCK)

    def sel_step(c, _):
        k0 = pl.multiple_of(c * SEL_KEY_CHUNK, SEL_KEY_CHUNK)
        k_aug = jnp.concatenate([ks_ref[0, pl.ds(k0, SEL_KEY_CHUNK), :], oh_ref[pl.ds(k0, SEL_KEY_CHUNK), :]],
                                axis=1)
        v_blk = vs_ref[0, pl.ds(k0, SEL_KEY_CHUNK), :]
        s = [_dot_nt(q, k_aug) for q in q_aug]
        for i in blocks:
            m_prev = m_ref[i]
            m_new = jnp.maximum(m_prev, _row_max(s[i]))
            p = jnp.exp2(s[i] - jnp.concatenate([m_new] * (SEL_KEY_CHUNK // LANES), axis=1))
            alpha = jnp.exp2(m_prev - m_new)
            m_ref[i] = m_new
            l_ref[i] = alpha * l_ref[i] + _row_sum(p)
            acc_ref[i] = alpha * acc_ref[i] + _dot(p.astype(BF16), v_blk)
        return 0

    head_rows = lambda x, hq: x[hq * qb:(hq + 1) * qb]
    partial = [[gcol[i][3 * hq] * head_rows(o_cmp[i], hq) + gcol[i][3 * hq + 2] * head_rows(o_win[i], hq)
                for hq in range(NSA_HEADS)] for i in blocks]
    lax.fori_loop(0, n_chunks, sel_step, 0)

    for i in blocks:
        o_sel = acc_ref[i] / jnp.maximum(l_ref[i], 1e-30)
        mixed = [partial[i][hq] + gcol[i][3 * hq + 1] * head_rows(o_sel, hq) for hq in range(NSA_HEADS)]
        for pair in range(NSA_HEADS // 2):
            a, b = mixed[2 * pair], mixed[2 * pair + 1]
            if 2 * pair // NSA_GROUP == 0:
                b = pltpu.roll(b, NSA_DH, axis=1)
            else:
                a = pltpu.roll(a, NSA_DH, axis=1)
            o_ref[0, i * qb:(i + 1) * qb, pair * LANES:(pair + 1) * LANES] = jnp.where(lane < NSA_DH, a, b)


def _nsa_attention(q, sm, kc, vc, kv):
    b, s, qw = q.shape
    tq = NSA_STEP_BLOCKS * SEL_BLOCK
    ncmp = kc.shape[1]
    onehot = (jnp.arange(s)[:, None] // SEL_BLOCK == jnp.arange(LANES)[None, :]).astype(BF16)
    kv_spec = lambda col: pl.BlockSpec((1, s, LANES), lambda i, j: (i, 0, col))
    return pl.pallas_call(
        _nsa_kernel,
        grid=(b, s // tq),
        in_specs=[pl.BlockSpec((1, tq, qw), lambda i, j: (i, j, 0)),
                  pl.BlockSpec((1, tq, LANES), lambda i, j: (i, j, 0)),
                  pl.BlockSpec((1, ncmp, LANES), lambda i, j: (i, 0, 0)),
                  pl.BlockSpec((1, ncmp, LANES), lambda i, j: (i, 0, 0)),
                  kv_spec(2), kv_spec(3), kv_spec(4), kv_spec(5), _const_spec((s, LANES))],
        out_specs=pl.BlockSpec((1, tq, NSA_WIDTH), lambda i, j: (i, j, 0)),
        out_shape=jax.ShapeDtypeStruct((b, s, NSA_WIDTH), F32),
        scratch_shapes=[pltpu.VMEM((NSA_STEP_BLOCKS, NSA_HEADS * SEL_BLOCK, LANES), F32)] * 3,
        compiler_params=_params("parallel", "arbitrary"),
        name="nsa_attention",
    )(q, sm, kc, vc, kv, kv, kv, kv, onehot)


def _unit_lower_inverses(mats, r, c):
    eye = jnp.where(r == c, 1.0, 0.0)
    each = lambda f, *lists: [f(*args) for args in zip(*lists)]
    bf = lambda xs: [x.astype(BF16) for x in xs]
    same_block = lambda shift: (r >> shift) == (c >> shift)
    n1 = each(lambda a: jnp.where(same_block(2), -a, 0.0), mats)
    n1b = bf(n1)
    n2b = bf(each(_dot, n1b, n1b))
    t = each(lambda m: eye + m, n1)
    t = each(lambda ti, tb, m: ti + _dot(tb, m), t, bf(t), n2b)
    for shift in range(2, GDN_CHUNK_SHIFT):
        sub_diag = same_block(shift + 1) & jnp.logical_not(same_block(shift))
        tb = bf(t)
        xb = bf(each(lambda a, ti: _dot(jnp.where(sub_diag, a, 0.0).astype(BF16), ti), mats, tb))
        t = each(lambda ti, tbi, xi: ti - _dot(tbi, xi), t, tb, xb)
    return t


def _log_decay(sm, alog_ref, dtb_ref):
    return -jnp.exp(alog_ref[...]) * jax.nn.softplus(sm + dtb_ref[...])


def _gdn_prep_kernel(x_ref, halo_ref, cw_ref, sm_ref, alog_ref, dtb_ref,
                     qd_ref, kdt_ref, at_ref, u_ref, w_ref, gcum_ref):
    n = GDN_PREP_TILE
    n_groups = x_ref.shape[1] // n
    width = cw_ref.shape[0]
    has_history = jnp.where(pl.program_id(1) > 0, 1.0, 0.0)

    def head_cols(c0):
        cols = slice(c0, c0 + GDN_DH)
        xe = jnp.concatenate([halo_ref[0, :, cols] * has_history, x_ref[0, :, cols]], axis=0)
        y = xe[GDN_HALO:] * cw_ref[width - 1:width, cols]
        for k in range(width - 1):
            y = y + pltpu.roll(xe, width - 1 - k, axis=0)[GDN_HALO:] * cw_ref[k:k + 1, cols]
        return jax.nn.silu(y)

    def l2norm(y):
        return y * lax.rsqrt(jnp.sum(y * y, axis=-1, keepdims=True) + EPS)

    r = lax.broadcasted_iota(jnp.int32, (n, n), 0)
    c = lax.broadcasted_iota(jnp.int32, (n, n), 1)
    in_chunk = (r >> GDN_CHUNK_SHIFT) == (c >> GDN_CHUNK_SHIFT)
    causal = in_chunk & (r >= c)
    hi = lax.Precision.HIGHEST
    pick = jnp.where(lax.broadcasted_iota(jnp.int32, (8, LANES), 1)
                     == lax.broadcasted_iota(jnp.int32, (8, LANES), 0) + SM_A0, 1.0, 0.0)
    q_all = [l2norm(head_cols(h * GDN_DH)) * (GDN_DH ** -0.5) for h in range(GDN_HEADS)]
    k_all = [l2norm(head_cols(GDN_WIDTH + h * GDN_DH)) for h in range(GDN_HEADS)]
    v_all = [head_cols(2 * GDN_WIDTH + h * GDN_DH) for h in range(GDN_HEADS)]

    a_mats, rhs, dest = [], [], []
    for grp in range(n_groups):
        tok = slice(grp * n, (grp + 1) * n)
        sm = sm_ref[0, tok]
        beta = jax.nn.sigmoid(sm)
        g = _log_decay(sm, alog_ref, dtb_ref)
        gcum = _dot(jnp.where(causal, 1.0, 0.0), g, precision=hi)
        gtot = _dot(jnp.where(in_chunk, 1.0, 0.0), g, precision=hi)
        gcum_ref[0, tok] = gcum
        gcum_rows = _dot_nt(pick, gcum, precision=hi)
        for h in range(GDN_HEADS):
            cols = slice(h * GDN_DH, (h + 1) * GDN_DH)
            q, k, v = q_all[h][tok], k_all[h][tok], v_all[h][tok]
            gc = _lane_col(gcum, SM_A0 + h)
            bcol = _lane_col(beta, SM_BETA0 + h)
            decay = jnp.exp(jnp.where(causal, gc - gcum_rows[h:h + 1, :], NEG_INF))
            kb = k * bcol
            egc = jnp.exp(gc)
            qd_ref[0, tok, cols] = (q * egc).astype(BF16)
            kdt_ref[0, cols, tok] = (k * jnp.exp(_lane_col(gtot, SM_A0 + h) - gc)).T.astype(BF16)
            grams = _dot_nt(jnp.concatenate([kb, q], axis=0).astype(BF16), k.astype(BF16))
            attn = (grams[n:] * decay).astype(BF16)
            for ch in range(n // GDN_CHUNK):
                lt = ch * GDN_CHUNK // LANES
                at_ref[0, grp * n + ch * GDN_CHUNK:grp * n + (ch + 1) * GDN_CHUNK, cols] = (
                    attn[ch * GDN_CHUNK:(ch + 1) * GDN_CHUNK, lt * LANES:(lt + 1) * LANES])
            a_mats.append(jnp.where(r > c, grams[:n] * decay, 0.0))
            rhs.append(jnp.concatenate([v * bcol, kb * egc], axis=1).astype(BF16))
            dest.append((tok, cols))
    for (tok, cols), t_inv, rh in zip(dest, _unit_lower_inverses(a_mats, r, c), rhs):
        uw = _dot(t_inv.astype(BF16), rh)
        u_ref[0, tok, cols] = uw[:, :GDN_DH]
        w_ref[0, tok, cols] = uw[:, GDN_DH:].astype(BF16)


def _gdn_prep(qkv, cw, sm, alog_vec, dtb_vec):
    b, s, width = qkv.shape
    n = GDN_PREP_GROUPS * GDN_PREP_TILE
    per = n // GDN_HALO
    tok = lambda w: pl.BlockSpec((1, n, w), lambda i, j: (i, j, 0))
    out = lambda dt: jax.ShapeDtypeStruct((b, s, GDN_WIDTH), dt)
    return pl.pallas_call(
        _gdn_prep_kernel,
        grid=(b, s // n),
        in_specs=[tok(width),
                  pl.BlockSpec((1, GDN_HALO, width), lambda i, j: (i, jnp.maximum(j * per - 1, 0), 0)),
                  _const_spec(cw.shape), tok(LANES), _const_spec((1, LANES)), _const_spec((1, LANES))],
        out_specs=[tok(GDN_WIDTH), pl.BlockSpec((1, GDN_WIDTH, n), lambda i, j: (i, 0, j))]
        + [tok(GDN_WIDTH)] * 3 + [tok(LANES)],
        out_shape=[out(BF16), jax.ShapeDtypeStruct((b, GDN_WIDTH, s), BF16), out(BF16), out(F32), out(BF16),
                   jax.ShapeDtypeStruct((b, s, LANES), F32)],
        compiler_params=_params("parallel", "parallel"),
        name="gdn_prep",
    )(qkv, qkv, cw, sm, alog_vec, dtb_vec)


def _gdn_scan_kernel(qd_ref, kdt_ref, at_ref, u_ref, w_ref, z_ref, gcum_ref, ng_ref, o_ref, state_ref):
    c = GDN_CHUNK
    n_chunks = qd_ref.shape[1] // c

    @pl.when(pl.program_id(1) == 0)
    def _():
        state_ref[...] = jnp.zeros_like(state_ref)

    def chunk_step(ci, _):
        t0 = pl.multiple_of(ci * c, c)
        rows = pl.ds(t0, c)
        odd = (ci & 1) == 1
        pair = pl.ds(pl.multiple_of((ci >> 1) * (2 * c), 2 * c), 2 * c)
        chains = [(bi, h) for bi in range(qd_ref.shape[0]) for h in range(GDN_HEADS)]
        cols = lambda h: slice(h * GDN_DH, (h + 1) * GDN_DH)
        states = [state_ref[bi * GDN_HEADS + h] for bi, h in chains]
        ws_qs = [_dot(jnp.concatenate([w_ref[bi, rows, cols(h)], qd_ref[bi, rows, cols(h)]], axis=0),
                      s.astype(BF16)) for (bi, h), s in zip(chains, states)]
        av_kv = []
        for (bi, h), x in zip(chains, ws_qs):
            v_new = (u_ref[bi, rows, cols(h)] - x[:c]).astype(BF16)
            zero = jnp.zeros_like(v_new)
            v_pad = jnp.concatenate([jnp.where(odd, zero, v_new), jnp.where(odd, v_new, zero)], axis=0)
            av_kv.append(_dot(jnp.concatenate([at_ref[bi, rows, cols(h)], kdt_ref[bi, cols(h), pair]], axis=0),
                              v_pad))
        for (bi, h), s, x, y in zip(chains, states, ws_qs, av_kv):
            state_decay = jnp.exp(gcum_ref[bi, pl.ds(t0 + c - 1, 1), :])
            state_ref[bi * GDN_HEADS + h] = s * _lane_col(state_decay, SM_A0 + h) + y[c:]
            o = x[c:] + y[:c]
            o = o * lax.rsqrt(jnp.mean(o * o, axis=-1, keepdims=True) + EPS) * ng_ref[...]
            o_ref[bi, rows, cols(h)] = o * jax.nn.silu(z_ref[bi, rows, cols(h)])
        return 0

    lax.fori_loop(0, n_chunks, chunk_step, 0)


def _gdn_scan(qd, kdt, at, u, w, z, gcum, norm_g):
    b, s, _ = qd.shape
    ts = GDN_SEQ_TILE
    nb = GDN_SCAN_BATCH if b % GDN_SCAN_BATCH == 0 else 1
    tok = lambda wd: pl.BlockSpec((nb, ts, wd), lambda i, j: (i, j, 0))
    return pl.pallas_call(
        _gdn_scan_kernel,
        grid=(b // nb, s // ts),
        in_specs=[tok(GDN_WIDTH), pl.BlockSpec((nb, GDN_WIDTH, ts), lambda i, j: (i, 0, j))]
        + [tok(GDN_WIDTH)] * 4 + [tok(LANES), _const_spec((1, GDN_DH))],
        out_specs=tok(GDN_WIDTH),
        out_shape=jax.ShapeDtypeStruct((b, s, GDN_WIDTH), F32),
        scratch_shapes=[pltpu.VMEM((nb * GDN_HEADS, GDN_DH, GDN_DH), F32)],
        compiler_params=_params("parallel", "arbitrary"),
        name="gdn_scan",
    )(qd, kdt, at, u, w, z, gcum, norm_g)


def _mix_ffn_kernel(x_ref, xh_ref, on_ref, onh_ref, og_ref, ogh_ref, gn_ref, wo_ref,
                    g_ref, wup_ref, cw_ref, wdn_ref, gf_ref, o_ref, *, final_norm):
    tm = x_ref.shape[1]
    d_ff = wdn_ref.shape[0]
    width = cw_ref.shape[0]
    has_history = jnp.where(pl.program_id(1) > 0, 1.0, 0.0)
    with_halo = lambda ref, halo: jnp.concatenate([halo[0] * has_history, ref[0]], axis=0)
    on = with_halo(on_ref, onh_ref)
    hn = (on * lax.rsqrt(jnp.mean(on * on, axis=-1, keepdims=True) + EPS) * gn_ref[...]).astype(BF16)
    mixed = jnp.concatenate([hn, with_halo(og_ref, ogh_ref).astype(BF16)], axis=1)
    xin = with_halo(x_ref, xh_ref) + _dot(mixed, wo_ref[...])
    x = xin[FFN_HALO:]
    ms = jnp.mean(xin * xin, axis=-1, keepdims=True)
    h = (xin * lax.rsqrt(ms + EPS) * g_ref[...]).astype(BF16)

    def conv(u, c0):
        y = u[FFN_HALO:] * cw_ref[width - 1:width, c0:c0 + FFN_COL_CHUNK]
        for k in range(width - 1):
            shifted = pltpu.roll(u, width - 1 - k, axis=0)[FFN_HALO:]
            y = y + shifted * cw_ref[k:k + 1, c0:c0 + FFN_COL_CHUNK]
        return y

    def up_proj(j):
        c0 = j * FFN_COL_CHUNK
        return (_dot(h, wup_ref[:, c0:c0 + FFN_COL_CHUNK]),
                _dot(h, wup_ref[:, d_ff + c0:d_ff + c0 + FFN_COL_CHUNK]))

    n_chunks = d_ff // FFN_COL_CHUNK
    y = x
    acts = []
    nxt = up_proj(0)
    for j in range(n_chunks):
        c0 = j * FFN_COL_CHUNK
        u_gate, u_up = nxt
        if j + 1 < n_chunks:
            nxt = up_proj(j + 1)
        acts.append((jax.nn.silu(conv(u_gate, c0)) * conv(u_up, d_ff + c0)).astype(BF16))
        if len(acts) == FFN_DOWN_GROUP or j + 1 == n_chunks:
            k0 = c0 + FFN_COL_CHUNK - len(acts) * FFN_COL_CHUNK
            y = y + _dot(jnp.concatenate(acts, axis=1), wdn_ref[k0:c0 + FFN_COL_CHUNK, :])
            acts = []
    if final_norm:
        y = y * lax.rsqrt(jnp.mean(y * y, axis=-1, keepdims=True) + EPS) * gf_ref[...]
    o_ref[0] = y


def _mix_ffn(x3, o_nsa, o_gdn, g_nsa, wo, g, wup, cw, wdn, g_final, final_norm):
    b, s, d = x3.shape
    tm = ROW_TILE
    per = tm // FFN_HALO
    rows = lambda w: pl.BlockSpec((1, tm, w), lambda i, j: (i, j, 0))
    halo = lambda w: pl.BlockSpec((1, FFN_HALO, w), lambda i, j: (i, jnp.maximum(j * per - 1, 0), 0))
    consts = (g_nsa, wo, g, wup, cw, wdn, g_final)
    return pl.pallas_call(
        functools.partial(_mix_ffn_kernel, final_norm=final_norm),
        grid=(b, s // tm),
        in_specs=[rows(d), halo(d), rows(o_nsa.shape[2]), halo(o_nsa.shape[2]),
                  rows(o_gdn.shape[2]), halo(o_gdn.shape[2])] + [_const_spec(c.shape) for c in consts],
        out_specs=rows(d),
        out_shape=jax.ShapeDtypeStruct((b, s, d), F32),
        compiler_params=_params("parallel", "parallel"),
        name="mix_convffn",
    )(x3, x3, o_nsa, o_nsa, o_gdn, o_gdn, *consts)


def _split_in_weights(w_in):
    d = w_in.shape[0]
    o = 0
    wq = w_in[:, o:o + NSA_WIDTH]; o += NSA_WIDTH
    wkv = w_in[:, o:o + 6 * NSA_KV_WIDTH]; o += 6 * NSA_KV_WIDTH
    wgate = w_in[:, o:o + 3 * NSA_HEADS]; o += 3 * NSA_HEADS
    wgdn = w_in[:, o:o + 3 * GDN_WIDTH]; o += 3 * GDN_WIDTH
    wz = w_in[:, o:o + GDN_WIDTH]; o += GDN_WIDTH
    wb = w_in[:, o:o + GDN_HEADS]; o += GDN_HEADS
    wa = w_in[:, o:o + GDN_HEADS]
    wq = wq.reshape(d, NSA_HEADS, NSA_DH) * (NSA_DH ** -0.5 * LOG2_E)
    zero = jnp.zeros_like(wq)
    kv_head = (jnp.arange(NSA_HEADS) // NSA_GROUP)[None, :, None]
    wq_pad = jnp.concatenate([jnp.where(kv_head == 0, wq, zero), jnp.where(kv_head == 1, wq, zero)], axis=-1)
    wq_pad = wq_pad.reshape(d, NSA_HEADS * LANES)
    wsm = jnp.concatenate([wgate, wb, wa, jnp.zeros((d, LANES - SM_A0 - GDN_HEADS), w_in.dtype)], axis=-1)
    return tuple(w.astype(BF16) for w in (wq_pad, wkv, wgdn, wz, wsm))


def _lane_vec(v, offset):
    return jnp.zeros((1, LANES), F32).at[0, offset:offset + v.shape[0]].set(v.astype(F32))


def _head_padded_w2(w2):
    z = jnp.zeros_like(w2)
    return jnp.stack([jnp.concatenate([w2, z], axis=-1), jnp.concatenate([z, w2], axis=-1)]).astype(BF16)


def kernel(x, norm_mix, w_in, cmp_pos_k, cmp_pos_v, cmp_k_w1, cmp_k_b1, cmp_k_w2, cmp_v_w1, cmp_v_b1, cmp_v_w2,
           nsa_norm, gdn_conv, gdn_a_log, gdn_dt_bias, gdn_norm, w_out, norm_ffn, ffn_up, ffn_conv, ffn_down,
           norm_final):
    b, s, d = x.shape
    depth = w_in.shape[0]
    group = CMP_BLOCK // 2
    assert CMP_STRIDE == group and s >= WIN_KEYS and s // SEL_BLOCK <= LANES
    assert s % max(ROW_TILE, SEL_KEY_CHUNK, GDN_PREP_TILE, GDN_SEQ_TILE, NSA_STEP_BLOCKS * SEL_BLOCK) == 0
    x2 = x.reshape(b * s, d)
    for l in range(depth):
        wq, wkv, wgdn, wz, wsm = _split_in_weights(w_in[l])
        q, kv, gdn, z, sm, cmp_kv = _inproj(x2, norm_mix[l][None, :], wq, wkv, wgdn, wz, wsm)
        q, kv, gdn, z, sm = (a.reshape(b, s, a.shape[-1]) for a in (q, kv, gdn, z, sm))

        groups = cmp_kv.reshape(2 * NSA_KV_HEADS, b, s // group, group * NSA_DH)
        kc, vc = _compress(
            groups, cmp_pos_k[l].reshape(1, -1), cmp_pos_v[l].reshape(1, -1),
            cmp_k_w1[l].astype(BF16), cmp_k_b1[l][None, :], _head_padded_w2(cmp_k_w2[l]),
            cmp_v_w1[l].astype(BF16), cmp_v_b1[l][None, :], _head_padded_w2(cmp_v_w2[l]))
        o_nsa = _nsa_attention(q, sm, kc, vc, kv)

        alog_vec, dtb_vec = _lane_vec(gdn_a_log[l], SM_A0), _lane_vec(gdn_dt_bias[l], SM_A0)
        qd, kdt, at, gu, gw, gcum = _gdn_prep(gdn, gdn_conv[l], sm, alog_vec, dtb_vec)
        o_gdn = _gdn_scan(qd, kdt, at, gu, gw, z, gcum, gdn_norm[l][None, :])

        x2 = _mix_ffn(x2.reshape(b, s, d), o_nsa, o_gdn, nsa_norm[l][None, :], w_out[l].astype(BF16),
                      norm_ffn[l][None, :], ffn_up[l].astype(BF16), ffn_conv[l], ffn_down[l].astype(BF16),
                      norm_final[None, :], final_norm=(l == depth - 1)).reshape(b * s, d)
    return x2.reshape(b, s, d)
```
